```python
import math
import jax
import jax.numpy as jnp
from jax import lax
import numpy as np

D_MODEL = 1024
BATCH = 2
SEQ = 16384
DEPTH = 4

GRID_W = 64
CTX_LEN = 256
D_FF = 2816
N_MOD = 9
HALF = D_MODEL // 2
CHUNK = 64
LN_EPS = 1e-5
NORM_EPS = 1e-6

HY_CH = HALF
HY_ORDER = 2
HY_SHORT = 3
HY_EMB = 33
HY_FF = 64
HY_FAST_DECAY = 0.3
HY_SLOW_DECAY = 1.5
HY_TARGET = 1e-2
HY_COLS = (HY_ORDER + 1) * HY_CH

GLA_HEADS = 4
GLA_DK = HALF // 2 // GLA_HEADS
GLA_DV = HALF // GLA_HEADS
GLA_RANK = 16
GLA_TAU = 16.0
GLA_COLS = 2 * GLA_HEADS * GLA_DK + 2 * HALF + 2 * GLA_RANK

ML_HEADS = 4
ML_DQK = HALF // 2 // ML_HEADS
ML_DV = HALF // ML_HEADS
ML_COLS = 2 * ML_HEADS * ML_DQK + 2 * HALF + 4 * ML_HEADS

SSD_HEADDIM = 64
SSD_HEADS = HALF // SSD_HEADDIM
SSD_GROUPS = 2
SSD_HPG = SSD_HEADS // SSD_GROUPS
SSD_STATE = 128
SSD_CONV = 3
SSD_CONV_DIM = HALF + 2 * SSD_GROUPS * SSD_STATE
SSD_COLS = HALF + SSD_CONV_DIM + 2 * SSD_HEADS

EVEN_COLS = HY_COLS + GLA_COLS
ODD_COLS = ML_COLS + SSD_COLS
N_EVEN = (DEPTH + 1) // 2
N_ODD = DEPTH // 2
DN_ALPHA = (2.0 * DEPTH) ** 0.25
DN_BETA = (8.0 * DEPTH) ** -0.25

kernel_name = 'hybrid_hyena_gla_mlstm_ssd_dit'


def _standardize(x, eps):
    xf = x.astype(jnp.float32)
    xc = xf - jnp.mean(xf, axis=-1, keepdims=True)
    return xc * lax.rsqrt(jnp.mean(xc * xc, axis=-1, keepdims=True) + eps)


def layer_norm(x, g, b):
    return _standardize(x, LN_EPS).astype(x.dtype) * g + b


def rms_norm(x, g):
    xf = x.astype(jnp.float32)
    return (xf * lax.rsqrt(jnp.mean(xf * xf, axis=-1, keepdims=True) + NORM_EPS)).astype(x.dtype) * g


def modulate(x, shift, scale):
    return x * (1.0 + scale) + shift


def post_norm(x, y, gate, g, b):
    return layer_norm(DN_ALPHA * x + gate * y, g, b)


def swiglu(h, w1, w2):
    gt, up = jnp.split(h @ w1, 2, axis=-1)
    return (jax.nn.silu(gt) * up) @ w2


def ffn_step(x, shift, scale, gate, w1, w2, g, b):
    return post_norm(x, swiglu(modulate(x, shift, scale), w1, w2), 0.5 * gate, g, b)


def dwconv_centred(u, w, b):
    p = w.shape[0] // 2
    y = lax.conv_general_dilated(u, w[:, None, :], window_strides=(1,), padding=[(p, p)],
                                 dimension_numbers=('NWC', 'WIO', 'NWC'),
                                 feature_group_count=u.shape[-1])
    return y + b


def to_col_major(t):
    bsz, n, ch = t.shape
    rows = n // GRID_W
    return t.reshape(bsz, rows, GRID_W, ch).transpose(0, 2, 1, 3).reshape(bsz, n, ch)


def to_row_major(t):
    bsz, n, ch = t.shape
    rows = n // GRID_W
    return t.reshape(bsz, GRID_W, rows, ch).transpose(0, 2, 1, 3).reshape(bsz, n, ch)


def _tril():
    return jnp.tril(jnp.ones((CHUNK, CHUNK), dtype=bool))


def _chunks(t):
    return t.reshape(t.shape[0], t.shape[1] // CHUNK, CHUNK, *t.shape[2:])


def _unchunk(t):
    return t.reshape(t.shape[0], t.shape[1] * t.shape[2], *t.shape[3:])


def _scan_states(decay, contrib, state0):
    def step(s, inp):
        dec, add = inp
        return dec * s + add, s
    final, starts = lax.scan(step, state0, (jnp.moveaxis(decay, 1, 0), jnp.moveaxis(contrib, 1, 0)))
    return jnp.moveaxis(starts, 0, 1), final


def bidirectional(run, ctx_dirs, lat_dirs, state0, need_ctx):
    y_ctx, y_lat = None, None
    for d in range(2):
        flip = (lambda t: t[:, ::-1]) if d == 1 else (lambda t: t)
        yc, state = run(*[flip(a) for a in ctx_dirs[d]], state0, need_ctx)
        yl, _ = run(*[flip(a) for a in lat_dirs[d]], state, True)
        y_lat = flip(yl) if y_lat is None else y_lat + flip(yl)
        if need_ctx:
            y_ctx = flip(yc) if y_ctx is None else y_ctx + flip(yc)
    return y_ctx, y_lat


def hyena_filters(n, w1, b1, w2, b2, w3, freq):
    pos = jnp.arange(n, dtype=jnp.float32)[:, None]
    t = pos / max(n - 1, 1)
    bands = (HY_EMB - 1) // 2
    f = jnp.linspace(1e-4, bands - 1, bands, dtype=jnp.float32)
    ang = pos * f * (2.0 * math.pi / n)
    feats = jnp.concatenate([t, jnp.cos(ang), -jnp.sin(ang)], axis=-1)
    hid = jnp.sin(freq[0] * (feats @ w1 + b1))
    hid = jnp.sin(freq[1] * (hid @ w2 + b2))
    h = (hid @ w3).astype(jnp.float32).reshape(n, HY_ORDER, 2, HY_CH)
    deltas = jnp.abs(jnp.linspace(math.log(HY_TARGET) / HY_SLOW_DECAY, math.log(HY_TARGET) / HY_FAST_DECAY,
                                  HY_CH, dtype=jnp.float32))
    h = h * jnp.exp(-t[:, :, None, None] * deltas)
    h = h / jnp.sum(jnp.abs(h), axis=(0, 2), keepdims=True)
    fwd, bwd = h[:, :, 0], h[:, :, 1]
    return jnp.concatenate([fwd.at[0].add(bwd[0]), jnp.zeros((1, HY_ORDER, HY_CH), jnp.float32),
                            bwd[:0:-1]], axis=0)


def fft_long_conv(u, filt):
    n = u.shape[1]
    uf = jnp.fft.rfft(u, n=2 * n, axis=1)
    ff = jnp.fft.rfft(filt, axis=0)
    return jnp.fft.irfft(uf * ff, n=2 * n, axis=1)[:, :n]


def hyena_seq(p, conv_w, conv_b, filt_params, bias):
    u = dwconv_centred(p, conv_w, conv_b).astype(jnp.float32)
    v, x1, x2 = jnp.split(u, 3, axis=-1)
    filt = hyena_filters(p.shape[1], *filt_params)
    z = x1 * (fft_long_conv(v, filt[:, 0]) + bias[0] * v)
    y = x2 * (fft_long_conv(z, filt[:, 1]) + bias[1] * z)
    return y.astype(p.dtype)


def gla_run(q, k, v, log_a, state0, need_out):
    qc, kc, vc = _chunks(q), _chunks(k), _chunks(v)
    b = jnp.cumsum(_chunks(log_a), axis=2)
    b_last = b[:, :, -1]
    kv = jnp.einsum('bzchk,bzchv->bzhkv', kc * jnp.exp(b_last[:, :, None] - b), vc)
    starts, final = _scan_states(jnp.exp(b_last)[..., None], kv, state0)
    if not need_out:
        return None, final
    q_dec = qc * jnp.exp(b)
    att = jnp.einsum('bzchk,bzshk->bzhcs', q_dec, kc * jnp.exp(-b))
    att = jnp.where(_tril(), att, 0.0)
    o = jnp.einsum('bzchk,bzhkv->bzchv', q_dec, starts) + jnp.einsum('bzhcs,bzshv->bzchv', att, vc)
    return _unchunk(o), final


def gla_mixer(pc, pl, a_w, a_b, norm_w, need_ctx):
    qk = GLA_HEADS * GLA_DK

    def prep(p):
        bsz, n, _ = p.shape
        q, k, v, g, a = jnp.split(p, [qk, 2 * qk, 2 * qk + HALF, 2 * qk + 2 * HALF], axis=-1)
        q = (q.reshape(bsz, n, GLA_HEADS, GLA_DK) * GLA_DK ** -0.5).astype(jnp.float32)
        k = k.reshape(bsz, n, GLA_HEADS, GLA_DK).astype(jnp.float32)
        v = v.reshape(bsz, n, GLA_HEADS, GLA_DV).astype(jnp.float32)
        lr = jnp.einsum('bldr,drk->bldk', a.reshape(bsz, n, 2, GLA_RANK), a_w) + a_b
        log_a = jax.nn.log_sigmoid(lr.astype(jnp.float32)) / GLA_TAU
        log_a = log_a.reshape(bsz, n, 2, GLA_HEADS, GLA_DK)
        return [(q, k, v, log_a[:, :, 0]), (q, k, v, log_a[:, :, 1])], g

    ctx_dirs, g_c = prep(pc)
    lat_dirs, g_l = prep(pl)
    s0 = jnp.zeros((pc.shape[0], GLA_HEADS, GLA_DK, GLA_DV), jnp.float32)
    o_c, o_l = bidirectional(gla_run, ctx_dirs, lat_dirs, s0, need_ctx)

    def out(o, g):
        return (jax.nn.silu(g) * rms_norm(o, norm_w).reshape(g.shape)).astype(g.dtype)

    return (out(o_c, g_c) if need_ctx else None), out(o_l, g_l)


def mlstm_run(q, k, v, log_i, log_f, state0, need_out):
    qc, kc, vc, li = _chunks(q), _chunks(k), _chunks(v), _chunks(log_i)
    b = jnp.cumsum(_chunks(log_f), axis=2)
    b_last = b[:, :, -1]
    a = b_last[:, :, None] - b + li
    a_max = jnp.max(a, axis=2)
    kw = kc * jnp.exp(a - a_max[:, :, None])[..., None]
    kv = jnp.einsum('bzchk,bzchv->bzhkv', kw, vc)
    kn = jnp.sum(kw, axis=2)

    def step(carry, inp):
        cm, nv, m = carry
        bl, am, kv_c, kn_c = inp
        m_new = jnp.maximum(bl + m, am)
        s_old = jnp.exp(bl + m - m_new)
        s_new = jnp.exp(am - m_new)
        cm_new = s_old[..., None, None] * cm + s_new[..., None, None] * kv_c
        nv_new = s_old[..., None] * nv + s_new[..., None] * kn_c
        return (cm_new, nv_new, m_new), (cm, nv, m)

    final, (c_s, n_s, m_s) = lax.scan(step, state0, tuple(jnp.moveaxis(t, 1, 0) for t in (b_last, a_max, kv, kn)))
    if not need_out:
        return None, final
    c_s, n_s, m_s = jnp.moveaxis(c_s, 0, 1), jnp.moveaxis(n_s, 0, 1), jnp.moveaxis(m_s, 0, 1)
    bt = jnp.swapaxes(b, 2, 3)
    lit = jnp.swapaxes(li, 2, 3)
    dmat = jnp.where(_tril(), bt[..., :, None] - bt[..., None, :] + lit[..., None, :], -jnp.inf)
    g = bt + m_s[..., None]
    m = jnp.maximum(g, jnp.max(dmat, axis=-1))
    s = jnp.einsum('bzchk,bzshk->bzhcs', qc, kc) * jnp.exp(dmat - m[..., None])
    e = jnp.exp(g - m)
    num = jnp.einsum('bzhcs,bzshv->bzhcv', s, vc) + e[..., None] * jnp.einsum('bzchk,bzhkv->bzhcv', qc, c_s)
    den = jnp.sum(s, axis=-1) + e * jnp.einsum('bzchk,bzhk->bzhc', qc, n_s)
    h = num / jnp.maximum(jnp.abs(den), jnp.exp(-m))[..., None]
    return _unchunk(jnp.swapaxes(h, 2, 3)), final


def mlstm_mixer(pc, pl, gate_b, norm_w, need_ctx):
    qk = ML_HEADS * ML_DQK

    def prep(p):
        bsz, n, _ = p.shape
        q, k, v, o, gt = jnp.split(p, [qk, 2 * qk, 2 * qk + HALF, 2 * qk + 2 * HALF], axis=-1)
        q = q.reshape(bsz, n, ML_HEADS, ML_DQK).astype(jnp.float32)
        k = (k.reshape(bsz, n, ML_HEADS, ML_DQK) * ML_DQK ** -0.5).astype(jnp.float32)
        v = v.reshape(bsz, n, ML_HEADS, ML_DV).astype(jnp.float32)
        gt = (gt.reshape(bsz, n, 2, 2, ML_HEADS) + gate_b).astype(jnp.float32)
        log_i = gt[:, :, :, 0]
        log_f = jax.nn.log_sigmoid(gt[:, :, :, 1])
        return [(q, k, v, log_i[:, :, d], log_f[:, :, d]) for d in range(2)], o

    ctx_dirs, o_c = prep(pc)
    lat_dirs, o_l = prep(pl)
    bsz = pc.shape[0]
    s0 = (jnp.zeros((bsz, ML_HEADS, ML_DQK, ML_DV), jnp.float32),
          jnp.zeros((bsz, ML_HEADS, ML_DQK), jnp.float32),
          jnp.zeros((bsz, ML_HEADS), jnp.float32))
    h_c, h_l = bidirectional(mlstm_run, ctx_dirs, lat_dirs, s0, need_ctx)

    def out(h, o):
        return (jax.nn.sigmoid(o) * (_standardize(h, NORM_EPS) * norm_w).reshape(o.shape)).astype(o.dtype)

    return (out(h_c, o_c) if need_ctx else None), out(h_l, o_l)


def ssd_run(x, bm, cm, dt, da, state0, need_out):
    xc, bc, cc, dtc = _chunks(x), _chunks(bm), _chunks(cm), _chunks(dt)
    cum = jnp.cumsum(_chunks(da), axis=2)
    cl = cum[:, :, -1]
    wgt = jnp.exp(cl[:, :, None] - cum) * dtc
    st = jnp.einsum('bzcgn,bzcgjp->bzgjpn', bc, xc * wgt[..., None])
    starts, final = _scan_states(jnp.exp(cl)[..., None, None], st, state0)
    if not need_out:
        return None, final
    y_off = jnp.einsum('bzcgn,bzgjpn->bzcgjp', cc, starts) * jnp.exp(cum)[..., None]
    cb = jnp.einsum('bzcgn,bzsgn->bzgcs', cc, bc)
    ct = jnp.moveaxis(cum, 2, -1)
    seg = jnp.exp(jnp.where(_tril(), ct[..., :, None] - ct[..., None, :], -jnp.inf))
    mix = cb[:, :, :, None] * seg * jnp.moveaxis(dtc, 2, -1)[..., None, :]
    y_diag = jnp.einsum('bzgjcs,bzsgjp->bzcgjp', mix, xc)
    return _unchunk(y_off + y_diag), final


def ssd_mixer(pc, pl, conv_w, conv_b, dt_bias, a_log, d_skip, norm_w, need_ctx):
    gn = SSD_GROUPS * SSD_STATE

    def prep(p):
        bsz, n, _ = p.shape
        z, xbc, dt = jnp.split(p, [HALF, HALF + SSD_CONV_DIM], axis=-1)
        xbc = jax.nn.silu(dwconv_centred(xbc, conv_w, conv_b))
        xs, bm, cm = jnp.split(xbc, [HALF, HALF + gn], axis=-1)
        xs = xs.reshape(bsz, n, SSD_GROUPS, SSD_HPG, SSD_HEADDIM).astype(jnp.float32)
        bm = bm.reshape(bsz, n, SSD_GROUPS, SSD_STATE).astype(jnp.float32)
        cm = cm.reshape(bsz, n, SSD_GROUPS, SSD_STATE).astype(jnp.float32)
        dt = jax.nn.softplus(dt.reshape(bsz, n, 2, SSD_HEADS).astype(jnp.float32) + dt_bias)
        da = dt * (-jnp.exp(a_log.astype(jnp.float32)))
        dt = dt.reshape(bsz, n, 2, SSD_GROUPS, SSD_HPG)
        da = da.reshape(bsz, n, 2, SSD_GROUPS, SSD_HPG)
        return [(xs, bm, cm, dt[:, :, d], da[:, :, d]) for d in range(2)], xs, z

    ctx_dirs, xs_c, z_c = prep(pc)
    lat_dirs, xs_l, z_l = prep(pl)
    s0 = jnp.zeros((pc.shape[0], SSD_GROUPS, SSD_HPG, SSD_HEADDIM, SSD_STATE), jnp.float32)
    y_c, y_l = bidirectional(ssd_run, ctx_dirs, lat_dirs, s0, need_ctx)

    def out(y, xs, z):
        y = y + d_skip.reshape(SSD_GROUPS, SSD_HPG, 1) * xs
        y = y.reshape(z.shape) * jax.nn.silu(z.astype(jnp.float32))
        return rms_norm(y, norm_w).astype(z.dtype)

    return (out(y_c, xs_c, z_c) if need_ctx else None), out(y_l, xs_l, z_l)


def even_mixer(hc, hl, w_in, w_out, hy_p, gla_p, need_ctx):
    conv_w, conv_b, filt_params, bias = hy_p
    pl = hl @ w_in
    pc = hc @ w_in
    g_c, g_l = gla_mixer(pc[..., HY_COLS:], pl[..., HY_COLS:], *gla_p, need_ctx)
    y_l = jnp.concatenate([hyena_seq(pl[..., :HY_COLS], conv_w, conv_b, filt_params, bias), g_l], axis=-1) @ w_out
    if not need_ctx:
        return None, y_l
    y_c = jnp.concatenate([hyena_seq(pc[..., :HY_COLS], conv_w, conv_b, filt_params, bias), g_c], axis=-1) @ w_out
    return y_c, y_l


def odd_mixer(hc, hl, w_in, w_out, ml_p, ssd_p, need_ctx):
    pl = to_col_major(hl) @ w_in
    pc = hc @ w_in
    m_c, m_l = mlstm_mixer(pc[..., :ML_COLS], pl[..., :ML_COLS], *ml_p, need_ctx)
    s_c, s_l = ssd_mixer(pc[..., ML_COLS:], pl[..., ML_COLS:], *ssd_p, need_ctx)
    y_l = to_row_major(jnp.concatenate([m_l, s_l], axis=-1)) @ w_out
    if not need_ctx:
        return None, y_l
    return jnp.concatenate([m_c, s_c], axis=-1) @ w_out, y_l


def setup_inputs(seed: int = 0) -> dict:
    key = jax.random.key(seed)
    ks = list(jax.random.split(key, 40))
    f32 = jnp.float32
    d = D_MODEL

    def nrm(shape, scale):
        return scale * jax.random.normal(ks.pop(), shape, f32)

    x = nrm((BATCH, SEQ, d), 1.0)
    c = nrm((BATCH, d), 1.0)
    ctx = nrm((BATCH, CTX_LEN, d), 1.0)
    c_ctx = nrm((d,), 1.0)
    w_ada = nrm((DEPTH, d, N_MOD * d), 0.5 * d ** -0.5)
    b_ada = nrm((DEPTH, N_MOD * d), 0.02)
    ln_g = 1.0 + nrm((DEPTH, 3, d), 0.02)
    ln_b = nrm((DEPTH, 3, d), 0.02)
    w_ffn_in = nrm((DEPTH, 2, d, 2 * D_FF), d ** -0.5)
    w_ffn_out = nrm((DEPTH, 2, D_FF, d), DN_BETA * D_FF ** -0.5)
    w_in_even = nrm((N_EVEN, d, EVEN_COLS), d ** -0.5)
    w_out_even = nrm((N_EVEN, 2 * HALF, d), DN_BETA * (2 * HALF) ** -0.5)
    hy_conv_w = nrm((N_EVEN, HY_SHORT, HY_COLS), HY_SHORT ** -0.5)
    hy_conv_b = nrm((N_EVEN, HY_COLS), 0.02)
    hy_w1 = nrm((N_EVEN, HY_EMB, HY_FF), HY_EMB ** -0.5)
    hy_b1 = nrm((N_EVEN, HY_FF), 0.1)
    hy_w2 = nrm((N_EVEN, HY_FF, HY_FF), HY_FF ** -0.5)
    hy_b2 = nrm((N_EVEN, HY_FF), 0.1)
    hy_w3 = nrm((N_EVEN, HY_FF, HY_ORDER * 2 * HY_CH), HY_FF ** -0.5)
    hy_freq = 1.0 + nrm((N_EVEN, 2, HY_FF), 0.1)
    hy_bias = nrm((N_EVEN, HY_ORDER, HY_CH), 0.5)
    gla_a_w = nrm((N_EVEN, 2, GLA_RANK, GLA_HEADS * GLA_DK), GLA_RANK ** -0.5)
    gla_a_b = nrm((N_EVEN, 2, GLA_HEADS * GLA_DK), 0.02)
    gla_norm_w = 1.0 + nrm((N_EVEN, GLA_DV), 0.02)
    w_in_odd = nrm((N_ODD, d, ODD_COLS), d ** -0.5)
    w_out_odd = nrm((N_ODD, 2 * HALF, d), DN_BETA * (2 * HALF) ** -0.5)
    ib = nrm((N_ODD, 2, 1, ML_HEADS), 0.1)
    fb = jnp.linspace(3.0, 6.0, ML_HEADS, dtype=f32) + nrm((N_ODD, 2, 1, ML_HEADS), 0.1)
    ml_gate_b = jnp.concatenate([ib, fb], axis=2)
    ml_norm_w = 1.0 + nrm((N_ODD, ML_DV), 0.02)
    ssd_conv_w = nrm((N_ODD, SSD_CONV, SSD_CONV_DIM), SSD_CONV ** -0.5)
    ssd_conv_b = nrm((N_ODD, SSD_CONV_DIM), 0.02)
    dt = jnp.exp(jax.random.uniform(ks.pop(), (N_ODD, 2, SSD_HEADS), f32, math.log(1e-3), math.log(1e-1)))
    ssd_dt_bias = dt + jnp.log(-jnp.expm1(-dt))
    ssd_a_log = jnp.log(jax.random.uniform(ks.pop(), (N_ODD, 2, SSD_HEADS), f32, 1.0, 16.0))
    ssd_d = 1.0 + nrm((N_ODD, SSD_HEADS), 0.1)
    ssd_norm_w = 1.0 + nrm((N_ODD, HALF), 0.02)
    return {'x': x, 'c': c, 'ctx': ctx, 'c_ctx': c_ctx, 'w_ada': w_ada, 'b_ada': b_ada,
            'ln_g': ln_g, 'ln_b': ln_b, 'w_ffn_in': w_ffn_in, 'w_ffn_out': w_ffn_out,
            'w_in_even': w_in_even, 'w_out_even': w_out_even, 'hy_conv_w': hy_conv_w, 'hy_conv_b': hy_conv_b,
            'hy_w1': hy_w1, 'hy_b1': hy_b1, 'hy_w2': hy_w2, 'hy_b2': hy_b2, 'hy_w3': hy_w3,
            'hy_freq': hy_freq, 'hy_bias': hy_bias, 'gla_a_w': gla_a_w, 'gla_a_b': gla_a_b,
            'gla_norm_w': gla_norm_w, 'w_in_odd': w_in_odd, 'w_out_odd': w_out_odd,
            'ml_gate_b': ml_gate_b, 'ml_norm_w': ml_norm_w, 'ssd_conv_w': ssd_conv_w, 'ssd_conv_b': ssd_conv_b,
            'ssd_dt_bias': ssd_dt_bias, 'ssd_a_log': ssd_a_log, 'ssd_d': ssd_d, 'ssd_norm_w': ssd_norm_w}


def reference(x, c, ctx, c_ctx, w_ada, b_ada, ln_g, ln_b, w_ffn_in, w_ffn_out,
              w_in_even, w_out_even, hy_conv_w, hy_conv_b, hy_w1, hy_b1, hy_w2, hy_b2, hy_w3,
              hy_freq, hy_bias, gla_a_w, gla_a_b, gla_norm_w, w_in_odd, w_out_odd,
              ml_gate_b, ml_norm_w, ssd_conv_w, ssd_conv_b, ssd_dt_bias, ssd_a_log, ssd_d, ssd_norm_w):
    bsz = x.shape[0]
    cond_l = jax.nn.silu(c)
    cond_c = jax.nn.silu(c_ctx)
    xl, xc = x, ctx
    for l in range(DEPTH):
        last = l == DEPTH - 1
        mod_l = (cond_l @ w_ada[l] + b_ada[l]).reshape(bsz, N_MOD, 1, D_MODEL)
        mod_c = (cond_c @ w_ada[l] + b_ada[l]).reshape(N_MOD, D_MODEL)
        xl = ffn_step(xl, mod_l[:, 0], mod_l[:, 1], mod_l[:, 2], w_ffn_in[l, 0], w_ffn_out[l, 0], ln_g[l, 0], ln_b[l, 0])
        xc = ffn_step(xc, mod_c[0], mod_c[1], mod_c[2], w_ffn_in[l, 0], w_ffn_out[l, 0], ln_g[l, 0], ln_b[l, 0])
        hl = modulate(xl, mod_l[:, 3], mod_l[:, 4])
        hc = modulate(xc, mod_c[3], mod_c[4])
        i = l // 2
        if l % 2 == 0:
            hy_p = (hy_conv_w[i], hy_conv_b[i], (hy_w1[i], hy_b1[i], hy_w2[i], hy_b2[i], hy_w3[i], hy_freq[i]), hy_bias[i])
            gla_p = (gla_a_w[i], gla_a_b[i], gla_norm_w[i])
            y_c, y_l = even_mixer(hc, hl, w_in_even[i], w_out_even[i], hy_p, gla_p, not last)
        else:
            ml_p = (ml_gate_b[i], ml_norm_w[i])
            ssd_p = (ssd_conv_w[i], ssd_conv_b[i], ssd_dt_bias[i], ssd_a_log[i], ssd_d[i], ssd_norm_w[i])
            y_c, y_l = odd_mixer(hc, hl, w_in_odd[i], w_out_odd[i], ml_p, ssd_p, not last)
        xl = post_norm(xl, y_l, mod_l[:, 5], ln_g[l, 1], ln_b[l, 1])
        xl = ffn_step(xl, mod_l[:, 6], mod_l[:, 7], mod_l[:, 8], w_ffn_in[l, 1], w_ffn_out[l, 1], ln_g[l, 2], ln_b[l, 2])
        if not last:
            xc = post_norm(xc, y_c, mod_c[5], ln_g[l, 1], ln_b[l, 1])
            xc = ffn_step(xc, mod_c[6], mod_c[7], mod_c[8], w_ffn_in[l, 1], w_ffn_out[l, 1], ln_g[l, 2], ln_b[l, 2])
    return xl
```

```python
import functools
import math

import numpy as np
import jax
import jax.numpy as jnp
from jax import lax
from jax.experimental import pallas as pl
from jax.experimental.pallas import tpu as pltpu

F32 = jnp.float32
BF16 = jnp.bfloat16

D_MODEL = 1024
DEPTH = 4
GRID_W = 64
D_FF = 2816
N_MOD = 9
HALF = D_MODEL // 2
CHUNK = 64
LN_EPS = 1e-5
NORM_EPS = 1e-6

HY_CH = HALF
HY_ORDER = 2
HY_EMB = 33
HY_FAST_DECAY = 0.3
HY_SLOW_DECAY = 1.5
HY_TARGET = 1e-2
HY_COLS = 3 * HY_CH

GLA_HEADS = 4
GLA_DK = HALF // 2 // GLA_HEADS
GLA_DV = HALF // GLA_HEADS
GLA_RANK = 16
GLA_TAU = 16.0
GLA_QK = GLA_HEADS * GLA_DK

ML_HEADS = 4
ML_DQK = HALF // 2 // ML_HEADS
ML_DV = HALF // ML_HEADS
ML_QK = ML_HEADS * ML_DQK

SSD_HEADDIM = 64
SSD_HEADS = HALF // SSD_HEADDIM
SSD_GROUPS = 2
SSD_HPG = SSD_HEADS // SSD_GROUPS
SSD_STATE = 128
SSD_GN = SSD_GROUPS * SSD_STATE
SSD_CONV_DIM = HALF + 2 * SSD_GN

DN_ALPHA = (2.0 * DEPTH) ** 0.25

SMALL = 32
FFT_N2 = 128
VMEM_LIMIT = 56 * 1024 * 1024


def _cparams(*sem):
    return pltpu.CompilerParams(dimension_semantics=sem, vmem_limit_bytes=VMEM_LIMIT)


def _mm(a, b):
    return jnp.dot(a.astype(BF16), b.astype(BF16), preferred_element_type=F32)


def _mm_nt(a, b):
    return lax.dot_general(a.astype(BF16), b.astype(BF16), (((1,), (1,)), ((), ())),
                           preferred_element_type=F32)


def _mm_tn(a, b):
    return lax.dot_general(a.astype(BF16), b.astype(BF16), (((0,), (0,)), ((), ())),
                           preferred_element_type=F32)


def _mm_hi(a, b):
    return jnp.dot(a, b, precision=lax.Precision.HIGHEST, preferred_element_type=F32)


def _mm_nt_hi(a, b):
    return lax.dot_general(a, b, (((1,), (1,)), ((), ())), precision=lax.Precision.HIGHEST,
                           preferred_element_type=F32)


def _silu(x):
    return x * jax.nn.sigmoid(x)


def _log_sigmoid(x):
    return jnp.minimum(x, 0.0) - jnp.log1p(jnp.exp(-jnp.abs(x)))


def _softplus(x):
    return jnp.maximum(x, 0.0) + jnp.log1p(jnp.exp(-jnp.abs(x)))


def _layer_norm(y, g, b):
    yc = y - jnp.mean(y, axis=-1, keepdims=True)
    return yc * lax.rsqrt(jnp.mean(yc * yc, axis=-1, keepdims=True) + LN_EPS) * g + b


def _tri(reverse):
    row = lax.broadcasted_iota(jnp.int32, (CHUNK, CHUNK), 0)
    col = lax.broadcasted_iota(jnp.int32, (CHUNK, CHUNK), 1)
    return (row <= col) if reverse else (row >= col)


def _resident(shape):
    nd = len(shape)
    return pl.BlockSpec(shape, lambda *_: (0,) * nd, pipeline_mode=pl.Buffered(1))


def _ada_kernel(c_ref, w_ref, b_ref, o_ref):
    o_ref[0] = _mm(_silu(c_ref[...]), w_ref[0]) + b_ref[0]


def _ada_all(cond8, w_ada, b_ada):
    depth, d, nm = w_ada.shape
    tn = nm // 8
    return pl.pallas_call(
        _ada_kernel,
        grid=(depth, nm // tn),
        in_specs=[pl.BlockSpec((8, d), lambda l, j: (0, 0)),
                  pl.BlockSpec((1, d, tn), lambda l, j: (l, 0, j)),
                  pl.BlockSpec((1, 1, tn), lambda l, j: (l, 0, j))],
        out_specs=pl.BlockSpec((1, 8, tn), lambda l, j: (l, 0, j)),
        out_shape=jax.ShapeDtypeStruct((depth, 8, nm), F32),
        compiler_params=_cparams("arbitrary", "arbitrary"),
    )(cond8, w_ada, b_ada.reshape(depth, 1, nm))


def _stream_view(x, layout):
    bsz, n, d = x.shape
    if layout == "col":
        return x.reshape(bsz, n // GRID_W, GRID_W * d)
    return x


def _tile_rows(n, layout, tm):
    return n // GRID_W if layout == "col" else min(tm, n)


def _main_spec(n, d, layout, tm):
    if layout == "col":
        return pl.BlockSpec((None, n // GRID_W, d), lambda b, i: (b, 0, i))
    return pl.BlockSpec((None, tm, d), lambda b, i: (b, i, 0))


def _halo_specs(n, d, layout, tm):
    if layout == "col":
        r8 = n // GRID_W // 8
        prev = pl.BlockSpec((None, 8, d), lambda b, i: (b, r8 - 1, jnp.maximum(i - 1, 0)))
        nxt = pl.BlockSpec((None, 8, d), lambda b, i: (b, 0, jnp.minimum(i + 1, GRID_W - 1)))
    else:
        t8, n8 = tm // 8, n // 8
        prev = pl.BlockSpec((None, 8, d), lambda b, i: (b, jnp.maximum(i * t8 - 1, 0), 0))
        nxt = pl.BlockSpec((None, 8, d), lambda b, i: (b, jnp.minimum((i + 1) * t8, n8 - 1), 0))
    return prev, nxt


def _seq_spec(tm, cols):
    return pl.BlockSpec((None, tm, cols), lambda b, i: (b, i, 0))


def _ffn_kernel(x_ref, mod_ref, w1_ref, w2_ref, g_ref, b_ref, o_ref, *, mo, nk):
    x = x_ref[...]
    shift, scale, gate = mod_ref[mo:mo + 1, :], mod_ref[mo + 1:mo + 2, :], mod_ref[mo + 2:mo + 3, :]
    h = (x * (1.0 + scale) + shift).astype(BF16)
    tf = D_FF // nk
    acc = jnp.zeros(x.shape, F32)
    for k in range(nk):
        gt = jnp.dot(h, w1_ref[:, k * tf:(k + 1) * tf], preferred_element_type=F32)
        up = jnp.dot(h, w1_ref[:, D_FF + k * tf:D_FF + (k + 1) * tf], preferred_element_type=F32)
        acc = acc + _mm(_silu(gt) * up, w2_ref[k * tf:(k + 1) * tf, :])
    o_ref[...] = _layer_norm(DN_ALPHA * x + (0.5 * gate) * acc, g_ref[...], b_ref[...])


def _ffn_step(x, mod, mo, w1, w2, g, b, tm=512):
    bsz, n, d = x.shape
    tm = min(tm, n)
    return pl.pallas_call(
        functools.partial(_ffn_kernel, mo=mo, nk=2),
        grid=(bsz, n // tm),
        in_specs=[_seq_spec(tm, d),
                  pl.BlockSpec((None, N_MOD, d), lambda bb, i: (bb, 0, 0)),
                  _resident(w1.shape), _resident(w2.shape),
                  _resident((1, d)), _resident((1, d))],
        out_specs=_seq_spec(tm, d),
        out_shape=jax.ShapeDtypeStruct(x.shape, F32),
        compiler_params=_cparams("arbitrary", "arbitrary"),
    )(x, mod, w1, w2, g.reshape(1, d), b.reshape(1, d))


def _inproj_kernel(x_ref, xp_ref, xn_ref, mod_ref, wp_ref, wc_ref, cw_ref, cb_ref, *out_refs,
                   psplit, csplit, act):
    i, last = pl.program_id(1), pl.num_programs(1) - 1
    shift, scale = mod_ref[3:4, :], mod_ref[4:5, :]
    tm = x_ref.shape[0]
    h = x_ref[...] * (1.0 + scale) + shift
    pp = _mm(h, wp_ref[...])
    o = 0
    for k, w in enumerate(psplit):
        out_refs[k][...] = pp[:, o:o + w]
        o += w
    hp = (xp_ref[...] * (1.0 + scale) + shift) * jnp.where(i > 0, 1.0, 0.0)
    hn = (xn_ref[...] * (1.0 + scale) + shift) * jnp.where(i < last, 1.0, 0.0)
    pc = _mm(jnp.concatenate([hp, h, hn], axis=0), wc_ref[...])
    up = pltpu.roll(pc, 1, 0)
    dn = pltpu.roll(pc, tm + 15, 0)
    y = (cw_ref[0:1, :] * up + cw_ref[1:2, :] * pc + cw_ref[2:3, :] * dn + cb_ref[...])[8:8 + tm]
    if act:
        y = _silu(y)
    o = 0
    for k, w in enumerate(csplit):
        out_refs[len(psplit) + k][...] = y[:, o:o + w]
        o += w


def _inproj(x, mod, wp, wc, cw, cb, psplit, csplit, act, layout, tm=512):
    bsz, n, d = x.shape
    tm = _tile_rows(n, layout, tm)
    xv = _stream_view(x, layout)
    prev, nxt = _halo_specs(n, d, layout, tm)
    widths = list(psplit) + list(csplit)
    return pl.pallas_call(
        functools.partial(_inproj_kernel, psplit=tuple(psplit), csplit=tuple(csplit), act=act),
        grid=(bsz, n // tm),
        in_specs=[_main_spec(n, d, layout, tm), prev, nxt,
                  pl.BlockSpec((None, N_MOD, d), lambda bb, i: (bb, 0, 0)),
                  _resident(wp.shape), _resident(wc.shape), _resident(cw.shape), _resident((1, cb.shape[-1]))],
        out_specs=[_seq_spec(tm, w) for w in widths],
        out_shape=[jax.ShapeDtypeStruct((bsz, n, w), F32) for w in widths],
        compiler_params=_cparams("arbitrary", "arbitrary"),
    )(xv, xv, xv, mod, wp, wc, cw, cb.reshape(1, -1))


def _outproj_kernel(a_ref, b_ref, x_ref, mod_ref, w_ref, g_ref, bb_ref, o_ref):
    y = _mm(a_ref[...], w_ref[0:HALF, :]) + _mm(b_ref[...], w_ref[HALF:2 * HALF, :])
    o_ref[...] = _layer_norm(DN_ALPHA * x_ref[...] + mod_ref[5:6, :] * y, g_ref[...], bb_ref[...])


def _outproj(a, bmix, x, mod, w, g, b, layout, tm=512):
    bsz, n, d = x.shape
    tm = _tile_rows(n, layout, tm)
    xv = _stream_view(x, layout)
    out = pl.pallas_call(
        _outproj_kernel,
        grid=(bsz, n // tm),
        in_specs=[_seq_spec(tm, HALF), _seq_spec(tm, HALF), _main_spec(n, d, layout, tm),
                  pl.BlockSpec((None, N_MOD, d), lambda bb, i: (bb, 0, 0)),
                  _resident(w.shape), _resident((1, d)), _resident((1, d))],
        out_specs=_main_spec(n, d, layout, tm),
        out_shape=jax.ShapeDtypeStruct(xv.shape, F32),
        compiler_params=_cparams("arbitrary", "arbitrary"),
    )(a, bmix, xv, mod, w, g.reshape(1, d), b.reshape(1, d))
    return out.reshape(bsz, n, d)


def _scan_call(kernel, seqs, consts, states, n_out_cols, reverse, tb=512):
    bsz, n, _ = seqs[0].shape
    tb = min(tb, n)
    nblk = n // tb

    def smap(b, i):
        return (b, nblk - 1 - i, 0) if reverse else (b, i, 0)

    in_specs = [pl.BlockSpec((None, tb, s.shape[-1]), smap) for s in seqs]
    in_specs += [_resident(c.shape) for c in consts]
    st_specs = [pl.BlockSpec((None,) + s.shape[1:], lambda b, i: (b, 0, 0)) for s in states]
    outs = pl.pallas_call(
        functools.partial(kernel, reverse=reverse, nchunk=tb // CHUNK),
        grid=(bsz, nblk),
        in_specs=in_specs + st_specs,
        out_specs=[pl.BlockSpec((None, tb, n_out_cols), smap)] + st_specs,
        out_shape=[jax.ShapeDtypeStruct((bsz, n, n_out_cols), F32)]
        + [jax.ShapeDtypeStruct(s.shape, F32) for s in states],
        scratch_shapes=[pltpu.VMEM(s.shape[1:], F32) for s in states],
        compiler_params=_cparams("arbitrary", "arbitrary"),
    )(*seqs, *consts, *states)
    return outs[0], tuple(outs[1:])


def _chunk_rows(ci, nchunk, reverse):
    c = (nchunk - 1 - ci) if reverse else ci
    return pl.ds(pl.multiple_of(c * CHUNK, CHUNK), CHUNK)


def _gla_kernel(*refs, reverse, nchunk, finalize):
    if finalize:
        qk_ref, v_ref, a_ref, op_ref, g_ref, aw_ref, ab_ref, nw_ref, s0_ref, o_ref, sout_ref, s_scr = refs
    else:
        qk_ref, v_ref, a_ref, aw_ref, ab_ref, s0_ref, o_ref, sout_ref, s_scr = refs
    i = pl.program_id(1)

    @pl.when(i == 0)
    def _():
        s_scr[...] = s0_ref[...]

    tri = _tri(reverse)
    trif = tri.astype(F32)
    head = lax.broadcasted_iota(jnp.int32, (1, GLA_QK), 1) // GLA_DK

    def body(ci, carry):
        rows = _chunk_rows(ci, nchunk, reverse)
        q = qk_ref[rows, 0:GLA_QK] * (GLA_DK ** -0.5)
        k = qk_ref[rows, GLA_QK:2 * GLA_QK]
        v = v_ref[rows, :]
        log_a = _log_sigmoid(_mm(a_ref[rows, :], aw_ref[...]) + ab_ref[...]) * (1.0 / GLA_TAU)
        b = _mm_hi(trif, log_a)
        b_last = jnp.sum(log_a, axis=0, keepdims=True)
        q_dec = q * jnp.exp(b)
        k_inv = k * jnp.exp(-b)
        k_dec = k * jnp.exp(b_last - b)
        s_old = s_scr[...]
        s_new = s_old * jnp.exp(b_last)
        for h in range(GLA_HEADS):
            sel = head == h
            qm = jnp.where(sel, q_dec, 0.0)
            att = jnp.where(tri, _mm_nt(qm, k_inv), 0.0)
            vh = v[:, h * GLA_DV:(h + 1) * GLA_DV]
            oh = _mm(att, vh) + _mm_nt(qm, s_old)
            s_new = s_new + _mm_tn(vh, jnp.where(sel, k_dec, 0.0))
            if finalize:
                oh = oh + op_ref[rows, h * GLA_DV:(h + 1) * GLA_DV]
                oh = oh * lax.rsqrt(jnp.mean(oh * oh, axis=-1, keepdims=True) + NORM_EPS) * nw_ref[...]
                oh = _silu(g_ref[rows, h * GLA_DV:(h + 1) * GLA_DV]) * oh
            o_ref[rows, h * GLA_DV:(h + 1) * GLA_DV] = oh
        s_scr[...] = s_new
        return carry

    lax.fori_loop(0, nchunk, body, 0)

    @pl.when(i == pl.num_programs(1) - 1)
    def _():
        sout_ref[...] = s_scr[...]


def _gla_mixer(ctx_p, lat_p, a_w, a_b, norm_w):
    bsz = ctx_p[0].shape[0]
    state = (jnp.zeros((bsz, GLA_DV, GLA_QK), F32),)
    outs = []
    prev = [None, None]
    for d in range(2):
        aw = jnp.zeros((SMALL, GLA_QK), F32).at[d * GLA_RANK:(d + 1) * GLA_RANK].set(a_w[d])
        ab = a_b[d].reshape(1, GLA_QK)
        st = state
        for t, (qk, v, g, a) in enumerate((ctx_p, lat_p)):
            if d == 0:
                o, st = _scan_call(functools.partial(_gla_kernel, finalize=False),
                                   [qk, v, a], [aw, ab], st, HALF, reverse=False)
                prev[t] = o
            else:
                o, st = _scan_call(functools.partial(_gla_kernel, finalize=True),
                                   [qk, v, a, prev[t], g], [aw, ab, norm_w.reshape(1, GLA_DV)], st, HALF,
                                   reverse=True)
                outs.append(o)
    return outs[0], outs[1]


def _mlstm_kernel(*refs, reverse, nchunk, finalize, d):
    if finalize:
        (qk_ref, v_ref, gt_ref, hp_ref, og_ref, gb_ref, nw_ref, c0_ref, n0_ref, m0_ref,
         h_ref, cout_ref, nout_ref, mout_ref, c_scr, n_scr, m_scr) = refs
    else:
        (qk_ref, v_ref, gt_ref, gb_ref, c0_ref, n0_ref, m0_ref,
         h_ref, cout_ref, nout_ref, mout_ref, c_scr, n_scr, m_scr) = refs
    i = pl.program_id(1)

    @pl.when(i == 0)
    def _():
        c_scr[...] = c0_ref[...]
        n_scr[...] = n0_ref[...]
        m_scr[...] = m0_ref[...]

    tri = _tri(reverse)
    trif = tri.astype(F32)
    head = lax.broadcasted_iota(jnp.int32, (1, ML_QK), 1) // ML_DQK
    mlane = lax.broadcasted_iota(jnp.int32, (8, 128), 1)
    eye = (lax.broadcasted_iota(jnp.int32, (SMALL, SMALL), 0)
           == lax.broadcasted_iota(jnp.int32, (SMALL, SMALL), 1)).astype(F32)

    def body(ci, carry):
        rows = _chunk_rows(ci, nchunk, reverse)
        q = qk_ref[rows, 0:ML_QK]
        k = qk_ref[rows, ML_QK:2 * ML_QK] * (ML_DQK ** -0.5)
        v = v_ref[rows, :]
        gates = gt_ref[rows, :] + gb_ref[...]
        gates_t = _mm_nt_hi(eye, gates)
        lf, lf_t = _log_sigmoid(gates), _log_sigmoid(gates_t)
        b_col = _mm_hi(trif, lf)
        b_row = _mm_nt_hi(lf_t, trif)
        b_tot = jnp.sum(lf, axis=0, keepdims=True)
        c_old, n_old, m_all = c_scr[...], n_scr[0:1, :], m_scr[...]
        keep = jnp.zeros((1, ML_QK), F32)
        c_add = jnp.zeros(c_old.shape, F32)
        n_add = jnp.zeros((1, ML_QK), F32)
        m_next = m_all
        for h in range(ML_HEADS):
            ic, fc = d * 8 + h, d * 8 + 4 + h
            bc, br = b_col[:, fc:fc + 1], b_row[fc:fc + 1, :]
            li_c, li_r = gates[:, ic:ic + 1], gates_t[ic:ic + 1, :]
            bl = b_tot[:, fc:fc + 1]
            m_old = m_all[0:1, h:h + 1]
            sel = head == h
            qm = jnp.where(sel, q, 0.0)
            vh = v[:, h * ML_DV:(h + 1) * ML_DV]
            dmat = jnp.where(tri, bc - br + li_r, -jnp.inf)
            g = bc + m_old
            m = jnp.maximum(g, jnp.max(dmat, axis=1, keepdims=True))
            s = _mm_nt(qm, k) * jnp.exp(dmat - m)
            e = jnp.exp(g - m)
            num = _mm(s, vh) + e * _mm_nt(qm, c_old)
            den = jnp.sum(s, axis=1, keepdims=True) + e * jnp.sum(qm * n_old, axis=1, keepdims=True)
            hh = num / jnp.maximum(jnp.abs(den), jnp.exp(-m))
            a = bl - bc + li_c
            a_max = jnp.max(a, axis=0, keepdims=True)
            kw = jnp.where(sel, k, 0.0) * jnp.exp(a - a_max)
            m_new = jnp.maximum(bl + m_old, a_max)
            w_old, w_new = jnp.exp(bl + m_old - m_new), jnp.exp(a_max - m_new)
            keep = keep + jnp.where(sel, w_old, 0.0)
            c_add = c_add + w_new * _mm_tn(vh, kw)
            n_add = n_add + w_new * jnp.sum(kw, axis=0, keepdims=True)
            m_next = jnp.where(mlane == h, m_new, m_next)
            if finalize:
                hh = hh + hp_ref[rows, h * ML_DV:(h + 1) * ML_DV]
                hc = hh - jnp.mean(hh, axis=-1, keepdims=True)
                hh = hc * lax.rsqrt(jnp.mean(hc * hc, axis=-1, keepdims=True) + NORM_EPS) * nw_ref[...]
                hh = jax.nn.sigmoid(og_ref[rows, h * ML_DV:(h + 1) * ML_DV]) * hh
            h_ref[rows, h * ML_DV:(h + 1) * ML_DV] = hh
        c_scr[...] = c_old * keep + c_add
        n_scr[...] = jnp.broadcast_to(n_old * keep + n_add, n_scr.shape)
        m_scr[...] = m_next
        return carry

    lax.fori_loop(0, nchunk, body, 0)

    @pl.when(i == pl.num_programs(1) - 1)
    def _():
        cout_ref[...] = c_scr[...]
        nout_ref[...] = n_scr[...]
        mout_ref[...] = m_scr[...]


def _mlstm_mixer(ctx_p, lat_p, gate_b, norm_w):
    bsz = ctx_p[0].shape[0]
    state = (jnp.zeros((bsz, ML_DV, ML_QK), F32), jnp.zeros((bsz, 8, ML_QK), F32),
             jnp.zeros((bsz, 8, 128), F32))
    gb = jnp.zeros((1, SMALL), F32).at[0, :16].set(gate_b.reshape(16))
    outs = []
    prev = [None, None]
    for d in range(2):
        st = state
        for t, (qk, v, og, sm) in enumerate((ctx_p, lat_p)):
            if d == 0:
                o, st = _scan_call(functools.partial(_mlstm_kernel, finalize=False, d=0),
                                   [qk, v, sm], [gb], st, HALF, reverse=False)
                prev[t] = o
            else:
                o, st = _scan_call(functools.partial(_mlstm_kernel, finalize=True, d=1),
                                   [qk, v, sm, prev[t], og], [gb, norm_w.reshape(1, ML_DV)], st, HALF,
                                   reverse=True)
                outs.append(o)
    return outs[0], outs[1]


def _ssd_kernel(*refs, reverse, nchunk, finalize, d):
    if finalize:
        (x_ref, bm_ref, cm_ref, sm_ref, yp_ref, z_ref, dtb_ref, alr_ref, alc_ref, ex_ref, dsk_ref, nw_ref,
         s0_ref, y_ref, sout_ref, s_scr) = refs
    else:
        (x_ref, bm_ref, cm_ref, sm_ref, dtb_ref, alr_ref, alc_ref, ex_ref,
         s0_ref, y_ref, sout_ref, s_scr) = refs
    i = pl.program_id(1)

    @pl.when(i == 0)
    def _():
        s_scr[...] = s0_ref[...]

    tri = _tri(reverse)
    trif = tri.astype(F32)
    gw = SSD_HPG * SSD_HEADDIM
    sub = lax.broadcasted_iota(jnp.int32, (1, gw), 1) // SSD_HEADDIM
    eye = (lax.broadcasted_iota(jnp.int32, (SMALL, SMALL), 0)
           == lax.broadcasted_iota(jnp.int32, (SMALL, SMALL), 1)).astype(F32)
    expand = ex_ref[...]

    def body(ci, carry):
        rows = _chunk_rows(ci, nchunk, reverse)
        x, bm, cm = x_ref[rows, :], bm_ref[rows, :], cm_ref[rows, :]
        raw = sm_ref[rows, :] + dtb_ref[...]
        raw_t = _mm_nt_hi(eye, raw)
        dt, dt_t = _softplus(raw), _softplus(raw_t)
        da, da_t = dt * (-jnp.exp(alr_ref[...])), dt_t * (-jnp.exp(alc_ref[...]))
        cum = _mm_hi(trif, da)
        cum_t = _mm_nt_hi(da_t, trif)
        cl = jnp.sum(da, axis=0, keepdims=True)
        e_cum = _mm_hi(jnp.exp(cum), expand)
        wgt = _mm_hi(jnp.exp(cl - cum) * dt, expand)
        dec = _mm_hi(jnp.broadcast_to(jnp.exp(cl), (8, SMALL)), expand)[0:1, :]
        xw = x * wgt
        s_old = s_scr[...]
        ys = []
        for g in range(SSD_GROUPS):
            ln = slice(g * gw, (g + 1) * gw)
            cmg = cm[:, g * SSD_STATE:(g + 1) * SSD_STATE]
            bmg = bm[:, g * SSD_STATE:(g + 1) * SSD_STATE]
            xg = x[:, ln]
            cb = _mm_nt(cmg, bmg)
            yg = _mm(cmg, s_old[:, ln]) * e_cum[:, ln]
            for j in range(SSD_HPG):
                col = 16 + d * 8 + g * SSD_HPG + j
                seg = jnp.exp(jnp.where(tri, cum[:, col:col + 1] - cum_t[col:col + 1, :], -jnp.inf))
                mix = cb * seg * dt_t[col:col + 1, :]
                yg = yg + _mm(mix, jnp.where(sub == j, xg, 0.0))
            ys.append(yg)
            s_scr[:, ln] = s_old[:, ln] * dec[:, ln] + _mm_tn(bmg, xw[:, ln])
        y = jnp.concatenate(ys, axis=1)
        if finalize:
            y = y + yp_ref[rows, :] + dsk_ref[...] * x
            y = y * _silu(z_ref[rows, :])
            y = y * lax.rsqrt(jnp.mean(y * y, axis=-1, keepdims=True) + NORM_EPS) * nw_ref[...]
        y_ref[rows, :] = y
        return carry

    lax.fori_loop(0, nchunk, body, 0)

    @pl.when(i == pl.num_programs(1) - 1)
    def _():
        sout_ref[...] = s_scr[...]


def _ssd_mixer(ctx_p, lat_p, dt_bias, a_log, d_skip, norm_w):
    bsz = ctx_p[0].shape[0]
    state = (jnp.zeros((bsz, SSD_STATE, HALF), F32),)
    dtb = jnp.zeros((1, SMALL), F32).at[0, 16:].set(dt_bias.reshape(16))
    al = jnp.zeros((1, SMALL), F32).at[0, 16:].set(a_log.reshape(16).astype(F32))
    dsk = jnp.repeat(d_skip, SSD_HEADDIM).reshape(1, HALF)
    lane_head = np.arange(HALF) // SSD_HEADDIM
    outs = []
    prev = [None, None]
    for d in range(2):
        ex = jnp.asarray((np.arange(SMALL)[:, None] == (16 + d * 8 + lane_head)[None, :]).astype(np.float32))
        consts = [dtb, al, al.reshape(SMALL, 1), ex]
        st = state
        for t, (xs, bm, cm, sm, z) in enumerate((ctx_p, lat_p)):
            if d == 0:
                o, st = _scan_call(functools.partial(_ssd_kernel, finalize=False, d=0),
                                   [xs, bm, cm, sm], consts, st, HALF, reverse=False)
                prev[t] = o
            else:
                o, st = _scan_call(functools.partial(_ssd_kernel, finalize=True, d=1),
                                   [xs, bm, cm, sm, prev[t], z],
                                   consts + [dsk, norm_w.reshape(1, HALF)], st, HALF, reverse=True)
                outs.append(o)
    return outs[0], outs[1]


def _dft_tables(n):
    big = 2 * n
    n1 = big // FFT_N2
    k = np.arange(n1)

    def cis(num, den):
        ang = 2.0 * np.pi * (num % den) / den
        return np.stack([np.cos(ang), -np.sin(ang)]).astype(np.float32)

    f1 = cis(k[:, None] * k[None, :], n1)
    k2 = np.arange(FFT_N2)
    f2 = cis(k2[:, None] * k2[None, :], FFT_N2)
    tw = cis(k[:, None] * k2[None, :], big)[..., None]
    return jnp.asarray(f1), jnp.asarray(f2), jnp.asarray(tw)


def _fft1_kernel(u_ref, f_ref, o_ref, *, real):
    fr, fi = f_ref[0], f_ref[1]
    if real:
        x = u_ref[0]
        o_ref[0] = _mm(fr, x)
        o_ref[1] = _mm(fi, x)
    else:
        xr, xi = u_ref[0], u_ref[1]
        o_ref[0] = _mm(fr, xr) - _mm(fi, xi)
        o_ref[1] = _mm(fr, xi) + _mm(fi, xr)


def _fft1(u, f, real, tl=2048):
    nin, kk, ll = u.shape
    n1 = f.shape[1]
    return pl.pallas_call(
        functools.partial(_fft1_kernel, real=real),
        grid=(ll // tl,),
        in_specs=[pl.BlockSpec((nin, kk, tl), lambda j: (0, 0, j)), _resident(f.shape)],
        out_specs=pl.BlockSpec((2, n1, tl), lambda j: (0, 0, j)),
        out_shape=jax.ShapeDtypeStruct((2, n1, ll), F32),
        compiler_params=_cparams("arbitrary"),
    )(u, f)


def _fft2_kernel(*refs, kb, conv):
    if conv:
        a_ref, tw_ref, f2_ref, hf_ref, o_ref = refs
    else:
        a_ref, tw_ref, f2_ref, o_ref = refs
    f2r, f2i = f2_ref[0], f2_ref[1]

    def body(j, carry):
        ar, ai = a_ref[0, j], a_ref[1, j]
        tr, ti = tw_ref[0, j], tw_ref[1, j]
        pr, pi = ar * tr - ai * ti, ar * ti + ai * tr
        xr = _mm(f2r, pr) - _mm(f2i, pi)
        xi = _mm(f2r, pi) + _mm(f2i, pr)
        if not conv:
            o_ref[0, j] = xr
            o_ref[1, j] = xi
            return carry
        hr, hi = hf_ref[0, j], hf_ref[1, j]
        yr, yi = xr * hr - xi * hi, xr * hi + xi * hr
        zr = _mm(f2r, yr) + _mm(f2i, yi)
        zi = _mm(f2r, yi) - _mm(f2i, yr)
        o_ref[0, j] = zr * tr + zi * ti
        o_ref[1, j] = zi * tr - zr * ti
        return carry

    lax.fori_loop(0, kb, body, 0)


def _fft2(a, tw, f2, hf=None, kb=8):
    _, n1, n2, ch = a.shape
    kb = min(kb, n1)
    blk = pl.BlockSpec((2, kb, n2, ch), lambda j: (0, j, 0, 0))
    ins = [a, tw, f2] + ([hf] if hf is not None else [])
    specs = [blk, pl.BlockSpec((2, kb, n2, 1), lambda j: (0, j, 0, 0)), _resident(f2.shape)]
    specs += [blk] if hf is not None else []
    return pl.pallas_call(
        functools.partial(_fft2_kernel, kb=kb, conv=hf is not None),
        grid=(n1 // kb,),
        in_specs=specs,
        out_specs=blk,
        out_shape=jax.ShapeDtypeStruct(a.shape, F32),
        compiler_params=_cparams("arbitrary"),
    )(*ins)


def _fft3_kernel(b_ref, g_ref, x_ref, v_ref, bias_ref, o_ref):
    gr, gi = g_ref[0], g_ref[1]
    br, bi = b_ref[0], b_ref[1]
    cr = _mm(gr, br) - _mm(gi, bi)
    ci = _mm(gr, bi) + _mm(gi, br)
    o_ref[0] = x_ref[0] * (cr + bias_ref[...] * v_ref[0])
    o_ref[1] = x_ref[1] * (ci + bias_ref[...] * v_ref[1])


def _fft3(bt, g, xg, v, bias_t, tl=2048):
    _, n1, ll = bt.shape
    kk = g.shape[1]
    vec = pl.BlockSpec((2, kk, tl), lambda j: (0, 0, j))
    return pl.pallas_call(
        _fft3_kernel,
        grid=(ll // tl,),
        in_specs=[pl.BlockSpec((2, n1, tl), lambda j: (0, 0, j)), _resident(g.shape), vec, vec,
                  _resident(bias_t.shape)],
        out_specs=vec,
        out_shape=jax.ShapeDtypeStruct((2, kk, ll), F32),
        compiler_params=_cparams("arbitrary"),
    )(bt, g, xg, v, bias_t)


def _hyena_lat(v, x1, x2, filt, bias):
    bsz, n, ch = v.shape
    assert bsz == 2
    big = 2 * n
    n1 = big // FFT_N2
    k1 = n1 // 2
    ll = FFT_N2 * ch
    tl = min(2048, ll)
    f1, f2, tw = _dft_tables(n)
    f1_half = f1[:, :, :k1]
    g = jnp.stack([f1[0, :k1, :], -f1[1, :k1, :]]) * (1.0 / big)
    cur = v
    for o, xg in enumerate((x1, x2)):
        hf = _fft2(_fft1(filt[:, o, :].reshape(1, n1, ll), f1, True, tl).reshape(2, n1, FFT_N2, ch), tw, f2)
        a = _fft1(cur.reshape(2, k1, ll), f1_half, False, tl).reshape(2, n1, FFT_N2, ch)
        bt = _fft2(a, tw, f2, hf).reshape(2, n1, ll)
        bias_t = jnp.tile(bias[o].reshape(1, ch), (1, tl // ch))
        cur = _fft3(bt, g, xg.reshape(2, k1, ll), cur.reshape(2, k1, ll), bias_t, tl).reshape(2, n, ch)
    return cur


def _hy_ctx_kernel(v_ref, x1_ref, x2_ref, filt_ref, f_ref, bias_ref, o_ref, *, n):
    fr, fi = f_ref[0], f_ref[1]
    inv = 1.0 / (2 * n)
    c0, c1 = v_ref[0], v_ref[1]
    for o, x_ref in enumerate((x1_ref, x2_ref)):
        filt = filt_ref[o]
        hr, hi = _mm(fr, filt), _mm(fi, filt)
        xr = _mm(fr[:, :n], c0) - _mm(fi[:, :n], c1)
        xi = _mm(fr[:, :n], c1) + _mm(fi[:, :n], c0)
        yr, yi = xr * hr - xi * hi, xr * hi + xi * hr
        gr, gi = fr[:n, :] * inv, fi[:n, :] * (-inv)
        cr = _mm(gr, yr) - _mm(gi, yi)
        ci = _mm(gr, yi) + _mm(gi, yr)
        b = bias_ref[o:o + 1, :]
        c0, c1 = x_ref[0] * (cr + b * c0), x_ref[1] * (ci + b * c1)
    o_ref[0] = c0
    o_ref[1] = c1


def _hyena_ctx(v, x1, x2, filt, bias):
    bsz, n, ch = v.shape
    assert bsz == 2
    big = 2 * n
    k = np.arange(big)
    ang = 2.0 * np.pi * ((k[:, None] * k[None, :]) % big) / big
    f = jnp.asarray(np.stack([np.cos(ang), -np.sin(ang)]).astype(np.float32))
    args = (v, x1, x2, filt, f, bias)
    return pl.pallas_call(
        functools.partial(_hy_ctx_kernel, n=n),
        grid=(1,),
        in_specs=[_resident(a.shape) for a in args],
        out_specs=pl.BlockSpec(v.shape, lambda j: (0, 0, 0)),
        out_shape=jax.ShapeDtypeStruct(v.shape, F32),
        compiler_params=_cparams("arbitrary"),
    )(*args)


def _hyena_filters(n, w1, b1, w2, b2, w3, freq):
    pos = jnp.arange(n, dtype=F32)[:, None]
    t = pos / max(n - 1, 1)
    bands = (HY_EMB - 1) // 2
    f = jnp.linspace(1e-4, bands - 1, bands, dtype=F32)
    ang = pos * f * (2.0 * math.pi / n)
    feats = jnp.concatenate([t, jnp.cos(ang), -jnp.sin(ang)], axis=-1)
    hid = jnp.sin(freq[0] * (feats @ w1 + b1))
    hid = jnp.sin(freq[1] * (hid @ w2 + b2))
    h = (hid @ w3).astype(F32).reshape(n, HY_ORDER, 2, HY_CH)
    deltas = jnp.abs(jnp.linspace(math.log(HY_TARGET) / HY_SLOW_DECAY, math.log(HY_TARGET) / HY_FAST_DECAY,
                                  HY_CH, dtype=F32))
    h = h * jnp.exp(-t[:, :, None, None] * deltas)
    h = h / jnp.sum(jnp.abs(h), axis=(0, 2), keepdims=True)
    fwd, bwd = h[:, :, 0], h[:, :, 1]
    return jnp.concatenate([fwd.at[0].add(bwd[0]), jnp.zeros((1, HY_ORDER, HY_CH), F32), bwd[:0:-1]], axis=0)


def _even_mixer(xc, xl, mod_c, mod_l, w_in, hy_p, gla_p, need_ctx):
    conv_w, conv_b, filt_params, bias = hy_p
    wp = w_in[:, HY_COLS:].astype(BF16)
    wc = w_in[:, :HY_COLS].astype(BF16)
    psplit = (2 * GLA_QK, HALF, HALF, 2 * GLA_RANK)
    csplit = (HY_CH, HY_CH, HY_CH)
    pc = _inproj(xc, mod_c, wp, wc, conv_w, conv_b, psplit, csplit, False, "row")
    pl_ = _inproj(xl, mod_l, wp, wc, conv_w, conv_b, psplit, csplit, False, "row")
    g_c, g_l = _gla_mixer((pc[0], pc[1], pc[2], pc[3]), (pl_[0], pl_[1], pl_[2], pl_[3]), *gla_p)
    h_l = _hyena_lat(pl_[4], pl_[5], pl_[6], _hyena_filters(xl.shape[1], *filt_params), bias)
    h_c = None
    if need_ctx:
        filt_c = jnp.transpose(_hyena_filters(xc.shape[1], *filt_params), (1, 0, 2))
        h_c = _hyena_ctx(pc[4], pc[5], pc[6], filt_c, bias)
    return (h_c, g_c), (h_l, g_l)


def _odd_mixer(xc, xl, mod_c, mod_l, w_in, ml_p, ssd_p):
    gate_b, ml_norm = ml_p
    conv_w, conv_b, dt_bias, a_log, d_skip, ssd_norm = ssd_p
    ml_cols = 2 * ML_QK + 2 * HALF + 4 * ML_HEADS
    o0 = 2 * ML_QK + 2 * HALF
    s0 = ml_cols
    wp = jnp.concatenate([w_in[:, :o0], w_in[:, s0:s0 + HALF], w_in[:, o0:ml_cols],
                          w_in[:, s0 + HALF + SSD_CONV_DIM:]], axis=1).astype(BF16)
    wc = w_in[:, s0 + HALF:s0 + HALF + SSD_CONV_DIM].astype(BF16)
    psplit = (2 * ML_QK, HALF, HALF, HALF, SMALL)
    csplit = (HALF, SSD_GN, SSD_GN)
    pc = _inproj(xc, mod_c, wp, wc, conv_w, conv_b, psplit, csplit, True, "row")
    pl_ = _inproj(xl, mod_l, wp, wc, conv_w, conv_b, psplit, csplit, True, "col")
    m_c, m_l = _mlstm_mixer((pc[0], pc[1], pc[2], pc[4]), (pl_[0], pl_[1], pl_[2], pl_[4]), gate_b, ml_norm)
    s_c, s_l = _ssd_mixer((pc[5], pc[6], pc[7], pc[4], pc[3]), (pl_[5], pl_[6], pl_[7], pl_[4], pl_[3]),
                          dt_bias, a_log, d_skip, ssd_norm)
    return (m_c, s_c), (m_l, s_l)


def kernel(x, c, ctx, c_ctx, w_ada, b_ada, ln_g, ln_b, w_ffn_in, w_ffn_out, w_in_even, w_out_even, hy_conv_w, hy_conv_b, hy_w1, hy_b1, hy_w2, hy_b2, hy_w3, hy_freq, hy_bias, gla_a_w, gla_a_b, gla_norm_w, w_in_odd, w_out_odd, ml_gate_b, ml_norm_w, ssd_conv_w, ssd_conv_b, ssd_dt_bias, ssd_a_log, ssd_d, ssd_norm_w):
    bsz, n, d = x.shape
    depth = w_ada.shape[0]
    assert bsz == 2 and d == D_MODEL
    cond8 = jnp.zeros((8, d), F32).at[:bsz].set(c).at[bsz].set(c_ctx)
    mod_all = _ada_all(cond8, w_ada, b_ada)
    xl, xc = x, ctx
    for l in range(depth):
        last = l == depth - 1
        mod_l = mod_all[l, :bsz].reshape(bsz, N_MOD, d)
        mod_c = jnp.broadcast_to(mod_all[l, bsz].reshape(1, N_MOD, d), (bsz, N_MOD, d))
        w1a, w2a = w_ffn_in[l, 0].astype(BF16), w_ffn_out[l, 0].astype(BF16)
        w1b, w2b = w_ffn_in[l, 1].astype(BF16), w_ffn_out[l, 1].astype(BF16)
        xl = _ffn_step(xl, mod_l, 0, w1a, w2a, ln_g[l, 0], ln_b[l, 0])
        xc = _ffn_step(xc, mod_c, 0, w1a, w2a, ln_g[l, 0], ln_b[l, 0])
        i = l // 2
        if l % 2 == 0:
            hy_p = (hy_conv_w[i], hy_conv_b[i],
                    (hy_w1[i], hy_b1[i], hy_w2[i], hy_b2[i], hy_w3[i], hy_freq[i]), hy_bias[i])
            gla_p = (gla_a_w[i], gla_a_b[i], gla_norm_w[i])
            y_c, y_l = _even_mixer(xc, xl, mod_c, mod_l, w_in_even[i], hy_p, gla_p, not last)
            w_out, layout = w_out_even[i].astype(BF16), "row"
        else:
            ml_p = (ml_gate_b[i], ml_norm_w[i])
            ssd_p = (ssd_conv_w[i], ssd_conv_b[i], ssd_dt_bias[i], ssd_a_log[i], ssd_d[i], ssd_norm_w[i])
            y_c, y_l = _odd_mixer(xc, xl, mod_c, mod_l, w_in_odd[i], ml_p, ssd_p)
            w_out, layout = w_out_odd[i].astype(BF16), "col"
        xl = _outproj(y_l[0], y_l[1], xl, mod_l, w_out, ln_g[l, 1], ln_b[l, 1], layout)
        xl = _ffn_step(xl, mod_l, 6, w1b, w2b, ln_g[l, 2], ln_b[l, 2])
        if not last:
            xc = _outproj(y_c[0], y_c[1], xc, mod_c, w_out, ln_g[l, 1], ln_b[l, 1], "row")
            xc = _ffn_step(xc, mod_c, 6, w1b, w2b, ln_g[l, 2], ln_b[l, 2])
    return xl
```

```python
import functools
import math

import numpy as np
import jax
import jax.numpy as jnp
from jax import lax
from jax.experimental import pallas as pl
from jax.experimental.pallas import tpu as pltpu

F32 = jnp.float32
BF16 = jnp.bfloat16

D_MODEL = 1024
DEPTH = 4
GRID_W = 64
D_FF = 2816
N_MOD = 9
HALF = D_MODEL // 2
CHUNK = 64
LN_EPS = 1e-5
NORM_EPS = 1e-6

HY_CH = HALF
HY_ORDER = 2
HY_EMB = 33
HY_BANDS = (HY_EMB - 1) // 2
HY_FAST_DECAY = 0.3
HY_SLOW_DECAY = 1.5
HY_TARGET = 1e-2
HY_COLS = 3 * HY_CH
HY_ROWS = HY_ORDER * HY_CH

GLA_HEADS = 4
GLA_DK = HALF // 2 // GLA_HEADS
GLA_DV = HALF // GLA_HEADS
GLA_RANK = 16
GLA_TAU = 16.0
GLA_QK = GLA_HEADS * GLA_DK

ML_HEADS = 4
ML_DQK = HALF // 2 // ML_HEADS
ML_DV = HALF // ML_HEADS
ML_QK = ML_HEADS * ML_DQK

SSD_HEADDIM = 64
SSD_HEADS = HALF // SSD_HEADDIM
SSD_GROUPS = 2
SSD_HPG = SSD_HEADS // SSD_GROUPS
SSD_STATE = 128
SSD_GN = SSD_GROUPS * SSD_STATE
SSD_CONV_DIM = HALF + 2 * SSD_GN

DN_ALPHA = (2.0 * DEPTH) ** 0.25

SMALL = 32
LANES = 128
VMEM_LIMIT = 56 * 1024 * 1024


def _cparams(*sem):
    return pltpu.CompilerParams(dimension_semantics=sem, vmem_limit_bytes=VMEM_LIMIT)


def _mm(a, b):
    return jnp.dot(a.astype(BF16), b.astype(BF16), preferred_element_type=F32)


def _mm_nt(a, b):
    return lax.dot_general(a.astype(BF16), b.astype(BF16), (((1,), (1,)), ((), ())),
                           preferred_element_type=F32)


def _mm_tn(a, b):
    return lax.dot_general(a.astype(BF16), b.astype(BF16), (((0,), (0,)), ((), ())),
                           preferred_element_type=F32)


def _mm_hi(a, b):
    return jnp.dot(a, b, precision=lax.Precision.HIGHEST, preferred_element_type=F32)


def _mm_nt_hi(a, b):
    return lax.dot_general(a, b, (((1,), (1,)), ((), ())), precision=lax.Precision.HIGHEST,
                           preferred_element_type=F32)


def _silu(x):
    return x * jax.nn.sigmoid(x)


def _log_sigmoid(x):
    return jnp.minimum(x, 0.0) - jnp.log1p(jnp.exp(-jnp.abs(x)))


def _softplus(x):
    return jnp.maximum(x, 0.0) + jnp.log1p(jnp.exp(-jnp.abs(x)))


def _layer_norm(y, g, b):
    yc = y - jnp.mean(y, axis=-1, keepdims=True)
    return yc * lax.rsqrt(jnp.mean(yc * yc, axis=-1, keepdims=True) + LN_EPS) * g + b


def _rms(y):
    return y * lax.rsqrt(jnp.mean(y * y, axis=-1, keepdims=True) + NORM_EPS)


def _tri(reverse):
    row = lax.broadcasted_iota(jnp.int32, (CHUNK, CHUNK), 0)
    col = lax.broadcasted_iota(jnp.int32, (CHUNK, CHUNK), 1)
    return (row <= col) if reverse else (row >= col)


def _eye(n):
    return (lax.broadcasted_iota(jnp.int32, (n, n), 0) == lax.broadcasted_iota(jnp.int32, (n, n), 1)).astype(F32)


def _resident(shape):
    nd = len(shape)
    return pl.BlockSpec(shape, lambda *_: (0,) * nd, pipeline_mode=pl.Buffered(1))


def _cplx_mm(ar, ai, w):
    u, v = _mm(ar, w), _mm(ai, w)
    h = w.shape[1] // 2
    return u[:, :h] - v[:, h:], u[:, h:] + v[:, :h]


def _cplx_mm_left(fr, fi, xr, xi):
    rhs = jnp.concatenate([xr, xi], axis=1)
    p, q = _mm(fr, rhs), _mm(fi, rhs)
    h = xr.shape[1]
    return p[:, :h] - q[:, h:], p[:, h:] + q[:, :h]


def _ada_kernel(c_ref, w_ref, b_ref, o_ref):
    o_ref[0] = _mm(_silu(c_ref[...]), w_ref[0]) + b_ref[0]


def _ada_all(cond8, w_ada, b_ada):
    depth, d, nm = w_ada.shape
    tn = nm // 8
    return pl.pallas_call(
        _ada_kernel,
        grid=(depth, nm // tn),
        in_specs=[pl.BlockSpec((8, d), lambda l, j: (0, 0)),
                  pl.BlockSpec((1, d, tn), lambda l, j: (l, 0, j)),
                  pl.BlockSpec((1, 1, tn), lambda l, j: (l, 0, j))],
        out_specs=pl.BlockSpec((1, 8, tn), lambda l, j: (l, 0, j)),
        out_shape=jax.ShapeDtypeStruct((depth, 8, nm), F32),
        name="ada",
        compiler_params=_cparams("arbitrary", "arbitrary"),
    )(cond8, w_ada, b_ada.reshape(depth, 1, nm))


def _stream_view(x, layout):
    bsz, n, d = x.shape
    if layout == "col":
        return x.reshape(bsz, n // GRID_W, GRID_W * d)
    return x


def _tile_rows(n, layout, tm):
    return n // GRID_W if layout == "col" else min(tm, n)


def _main_spec(n, d, layout, tm):
    if layout == "col":
        return pl.BlockSpec((None, n // GRID_W, d), lambda b, i: (b, 0, i))
    return pl.BlockSpec((None, tm, d), lambda b, i: (b, i, 0))


def _halo_specs(n, d, layout, tm):
    if layout == "col":
        r8 = n // GRID_W // 8
        prev = pl.BlockSpec((None, 8, d), lambda b, i: (b, r8 - 1, jnp.maximum(i - 1, 0)))
        nxt = pl.BlockSpec((None, 8, d), lambda b, i: (b, 0, jnp.minimum(i + 1, GRID_W - 1)))
    else:
        t8, n8 = tm // 8, n // 8
        prev = pl.BlockSpec((None, 8, d), lambda b, i: (b, jnp.maximum(i * t8 - 1, 0), 0))
        nxt = pl.BlockSpec((None, 8, d), lambda b, i: (b, jnp.minimum((i + 1) * t8, n8 - 1), 0))
    return prev, nxt


def _seq_spec(tm, cols):
    return pl.BlockSpec((None, tm, cols), lambda b, i: (b, i, 0))


def _chan_spec(rows, tm):
    return pl.BlockSpec((None, rows, tm // LANES, LANES), lambda b, i: (b, 0, i, 0))


def _lanes_of(ref, lead, pieces):
    return jnp.concatenate([ref[lead + (slice(None), q, slice(None))] for q in range(pieces)], axis=1)


def _ffn_kernel(x_ref, mod_ref, w1_ref, w2_ref, g_ref, b_ref, o_ref, *, mo, nk):
    x = x_ref[...]
    shift, scale, gate = mod_ref[mo:mo + 1, :], mod_ref[mo + 1:mo + 2, :], mod_ref[mo + 2:mo + 3, :]
    h = (x * (1.0 + scale) + shift).astype(BF16)
    tf = D_FF // nk
    acc = jnp.zeros(x.shape, F32)
    for k in range(nk):
        gt = jnp.dot(h, w1_ref[:, k * tf:(k + 1) * tf], preferred_element_type=F32)
        up = jnp.dot(h, w1_ref[:, D_FF + k * tf:D_FF + (k + 1) * tf], preferred_element_type=F32)
        acc = acc + _mm(_silu(gt) * up, w2_ref[k * tf:(k + 1) * tf, :])
    o_ref[...] = _layer_norm(DN_ALPHA * x + (0.5 * gate) * acc, g_ref[...], b_ref[...])


def _ffn_step(x, mod, mo, w1, w2, g, b, tm=512):
    bsz, n, d = x.shape
    tm = min(tm, n)
    return pl.pallas_call(
        functools.partial(_ffn_kernel, mo=mo, nk=2),
        grid=(bsz, n // tm),
        in_specs=[_seq_spec(tm, d),
                  pl.BlockSpec((None, N_MOD, d), lambda bb, i: (bb, 0, 0)),
                  _resident(w1.shape), _resident(w2.shape),
                  _resident((1, d)), _resident((1, d))],
        out_specs=_seq_spec(tm, d),
        out_shape=jax.ShapeDtypeStruct(x.shape, F32),
        name="ffn",
        compiler_params=_cparams("arbitrary", "arbitrary"),
    )(x, mod, w1, w2, g.reshape(1, d), b.reshape(1, d))


def _inproj_kernel(x_ref, xp_ref, xn_ref, mod_ref, wp_ref, wc_ref, cw_ref, cb_ref, *out_refs,
                   psplit, csplit, act, chan_major):
    i, last = pl.program_id(1), pl.num_programs(1) - 1
    shift, scale = mod_ref[3:4, :], mod_ref[4:5, :]
    tm = x_ref.shape[0]
    h = x_ref[...] * (1.0 + scale) + shift
    hb = h.astype(BF16)
    o = 0
    for k, w in enumerate(psplit):
        out_refs[k][...] = jnp.dot(hb, wp_ref[:, o:o + w], preferred_element_type=F32).astype(out_refs[k].dtype)
        o += w
    hp = (xp_ref[...] * (1.0 + scale) + shift) * jnp.where(i > 0, 1.0, 0.0)
    hn = (xn_ref[...] * (1.0 + scale) + shift) * jnp.where(i < last, 1.0, 0.0)
    couts = out_refs[len(psplit):]
    if chan_major:
        lane = lax.broadcasted_iota(jnp.int32, (1, tm), 1)
        halo = jnp.concatenate([hp, hn, jnp.zeros((LANES - 16, hp.shape[1]), F32)], axis=0).astype(BF16)
        o = 0
        for k, w in enumerate(csplit):
            wt = wc_ref[o:o + w, :]
            p = _mm_nt(wt, hb)
            p_halo = _mm_nt(wt, halo)
            up = jnp.where(lane == 0, p_halo[:, 7:8], pltpu.roll(p, 1, 1))
            dn = jnp.where(lane == tm - 1, p_halo[:, 8:9], pltpu.roll(p, tm - 1, 1))
            cw = cw_ref[o:o + w, :]
            y = cw[:, 0:1] * up + cw[:, 1:2] * p + cw[:, 2:3] * dn + cb_ref[o:o + w, :]
            for q in range(tm // LANES):
                couts[k][:, q, :] = y[:, q * LANES:(q + 1) * LANES]
            o += w
    else:
        pc = _mm(jnp.concatenate([hp, h, hn], axis=0), wc_ref[...])
        up = pltpu.roll(pc, 1, 0)
        dn = pltpu.roll(pc, tm + 15, 0)
        y = (cw_ref[0:1, :] * up + cw_ref[1:2, :] * pc + cw_ref[2:3, :] * dn + cb_ref[...])[8:8 + tm]
        if act:
            y = _silu(y)
        o = 0
        for k, w in enumerate(csplit):
            couts[k][...] = y[:, o:o + w].astype(couts[k].dtype)
            o += w


def _inproj(x, mod, wp, wc, cw, cb, psplit, csplit, act, layout, chan_major, tm):
    bsz, n, d = x.shape
    tm = _tile_rows(n, layout, tm)
    xv = _stream_view(x, layout)
    prev, nxt = _halo_specs(n, d, layout, tm)
    pdt = [BF16 if w >= LANES else F32 for w in psplit]
    out_specs = [_seq_spec(tm, w) for w in psplit]
    out_shape = [jax.ShapeDtypeStruct((bsz, n, w), dt) for w, dt in zip(psplit, pdt)]
    if chan_major:
        out_specs += [_chan_spec(w, tm) for w in csplit]
        out_shape += [jax.ShapeDtypeStruct((bsz, w, n // LANES, LANES), F32) for w in csplit]
    else:
        out_specs += [_seq_spec(tm, w) for w in csplit]
        out_shape += [jax.ShapeDtypeStruct((bsz, n, w), BF16) for w in csplit]
    return pl.pallas_call(
        functools.partial(_inproj_kernel, psplit=tuple(psplit), csplit=tuple(csplit), act=act,
                          chan_major=chan_major),
        grid=(bsz, n // tm),
        in_specs=[_main_spec(n, d, layout, tm), prev, nxt,
                  pl.BlockSpec((None, N_MOD, d), lambda bb, i: (bb, 0, 0)),
                  _resident(wp.shape), _resident(wc.shape), _resident(cw.shape), _resident(cb.shape)],
        out_specs=out_specs,
        out_shape=out_shape,
        name="inproj_chan" if chan_major else "inproj",
        compiler_params=_cparams("arbitrary", "arbitrary"),
    )(xv, xv, xv, mod, wp, wc, cw, cb)


def _outproj_even_kernel(hy_ref, of_ref, ob_ref, g_ref, x_ref, mod_ref, w_ref, lg_ref, lb_ref, nw_ref, o_ref):
    tm = x_ref.shape[0]
    y = _mm_tn(_lanes_of(hy_ref, (), tm // LANES), w_ref[0:HALF, :])
    o = of_ref[...] + ob_ref[...]
    g = g_ref[...].astype(F32)
    heads = [_rms(o[:, h * GLA_DV:(h + 1) * GLA_DV]) * nw_ref[...] for h in range(GLA_HEADS)]
    y = y + _mm(_silu(g) * jnp.concatenate(heads, axis=1), w_ref[HALF:2 * HALF, :])
    o_ref[...] = _layer_norm(DN_ALPHA * x_ref[...] + mod_ref[5:6, :] * y, lg_ref[...], lb_ref[...])


def _outproj_odd_kernel(hf_ref, hb_ref, og_ref, yf_ref, yb_ref, xs_ref, z_ref, x_ref, mod_ref, w_ref,
                        lg_ref, lb_ref, mnw_ref, dsk_ref, snw_ref, o_ref):
    h = hf_ref[...] + hb_ref[...]
    heads = []
    for k in range(ML_HEADS):
        hk = h[:, k * ML_DV:(k + 1) * ML_DV]
        hc = hk - jnp.mean(hk, axis=-1, keepdims=True)
        heads.append(hc * lax.rsqrt(jnp.mean(hc * hc, axis=-1, keepdims=True) + NORM_EPS) * mnw_ref[...])
    m = jax.nn.sigmoid(og_ref[...].astype(F32)) * jnp.concatenate(heads, axis=1)
    s = yf_ref[...] + yb_ref[...] + dsk_ref[...] * xs_ref[...].astype(F32)
    s = _rms(s * _silu(z_ref[...].astype(F32))) * snw_ref[...]
    y = _mm(m, w_ref[0:HALF, :]) + _mm(s, w_ref[HALF:2 * HALF, :])
    o_ref[...] = _layer_norm(DN_ALPHA * x_ref[...] + mod_ref[5:6, :] * y, lg_ref[...], lb_ref[...])


def _outproj(kern, parts, part_specs, x, mod, w, g, b, consts, layout, tm):
    bsz, n, d = x.shape
    tm = _tile_rows(n, layout, tm)
    xv = _stream_view(x, layout)
    consts = [g.reshape(1, d), b.reshape(1, d)] + consts
    out = pl.pallas_call(
        kern,
        grid=(bsz, n // tm),
        in_specs=[s(tm) for s in part_specs] + [_main_spec(n, d, layout, tm),
                  pl.BlockSpec((None, N_MOD, d), lambda bb, i: (bb, 0, 0)), _resident(w.shape)]
        + [_resident(c.shape) for c in consts],
        out_specs=_main_spec(n, d, layout, tm),
        out_shape=jax.ShapeDtypeStruct(xv.shape, F32),
        name=kern.__name__.strip("_"),
        compiler_params=_cparams("arbitrary", "arbitrary"),
    )(*parts, xv, mod, w, *consts)
    return out.reshape(bsz, n, d)


def _scan_call(kernel, seqs, consts, states, out_cols, tb=512):
    bsz, n, _ = seqs[0].shape
    tb = min(tb, n)
    nblk = n // tb

    def fmap(i):
        return (0, i, 0)

    def bmap(i):
        return (0, nblk - 1 - i, 0)

    in_specs = [pl.BlockSpec((bsz, tb, s.shape[-1]), fmap) for s in seqs]
    in_specs += [pl.BlockSpec((bsz, tb, s.shape[-1]), bmap) for s in seqs]
    in_specs += [_resident(c.shape) for c in consts]
    st_specs = [pl.BlockSpec(s.shape, lambda i, nd=s.ndim: (0,) * nd) for s in states]
    outs = pl.pallas_call(
        functools.partial(kernel, nchunk=tb // CHUNK),
        grid=(nblk,),
        in_specs=in_specs + st_specs,
        out_specs=[pl.BlockSpec((bsz, tb, out_cols), fmap), pl.BlockSpec((bsz, tb, out_cols), bmap)] + st_specs,
        out_shape=[jax.ShapeDtypeStruct((bsz, n, out_cols), F32)] * 2
        + [jax.ShapeDtypeStruct(s.shape, F32) for s in states],
        scratch_shapes=[pltpu.VMEM(s.shape, F32) for s in states],
        name=kernel.__name__.strip("_"),
        compiler_params=_cparams("arbitrary"),
    )(*seqs, *seqs, *consts, *states)
    return outs[0], outs[1], tuple(outs[2:])


def _chunk_rows(ci, nchunk, reverse):
    c = (nchunk - 1 - ci) if reverse else ci
    return pl.ds(pl.multiple_of(c * CHUNK, CHUNK), CHUNK)


def _scan_states(i, s0_refs, sout_refs, scrs, body, nchunk):
    @pl.when(i == 0)
    def _():
        for s0, scr in zip(s0_refs, scrs):
            scr[...] = s0[...]

    lax.fori_loop(0, nchunk, body, 0)

    @pl.when(i == pl.num_programs(0) - 1)
    def _():
        for so, scr in zip(sout_refs, scrs):
            so[...] = scr[...]


def _gla_chunk(qk, v, a, aw, ab, s_old, tri, head):
    q = qk[:, 0:GLA_QK].astype(F32) * (GLA_DK ** -0.5)
    k = qk[:, GLA_QK:2 * GLA_QK].astype(F32)
    log_a = _log_sigmoid(_mm(a, aw) + ab) * (1.0 / GLA_TAU)
    b = _mm_hi(tri.astype(F32), log_a)
    b_last = jnp.sum(log_a, axis=0, keepdims=True)
    q_dec = q * jnp.exp(b)
    k_inv = k * jnp.exp(-b)
    k_dec = k * jnp.exp(b_last - b)
    s_new = s_old * jnp.exp(b_last)
    outs = []
    for h in range(GLA_HEADS):
        sel = head == h
        qm = jnp.where(sel, q_dec, 0.0)
        att = jnp.where(tri, _mm_nt(qm, k_inv), 0.0)
        vh = v[:, h * GLA_DV:(h + 1) * GLA_DV]
        outs.append(_mm(att, vh) + _mm_nt(qm, s_old))
        s_new = s_new + _mm_tn(vh, jnp.where(sel, k_dec, 0.0))
    return outs, s_new


def _gla_kernel(qkf_ref, vf_ref, af_ref, qkb_ref, vb_ref, ab_ref, aw_ref, abias_ref, s0_ref,
                of_ref, ob_ref, sout_ref, s_scr, *, nchunk):
    head = lax.broadcasted_iota(jnp.int32, (1, GLA_QK), 1) // GLA_DK
    tris = (_tri(False), _tri(True))
    dirs = ((qkf_ref, vf_ref, af_ref, of_ref), (qkb_ref, vb_ref, ab_ref, ob_ref))

    def body(ci, carry):
        for d, (qk_ref, v_ref, a_ref, o_ref) in enumerate(dirs):
            rows = _chunk_rows(ci, nchunk, d == 1)
            for b in range(qk_ref.shape[0]):
                outs, s_new = _gla_chunk(qk_ref[b, rows, :], v_ref[b, rows, :], a_ref[b, rows, :],
                                         aw_ref[d], abias_ref[d], s_scr[d, b], tris[d], head)
                for h, oh in enumerate(outs):
                    o_ref[b, rows, h * GLA_DV:(h + 1) * GLA_DV] = oh
                s_scr[d, b] = s_new
        return carry

    _scan_states(pl.program_id(0), [s0_ref], [sout_ref], [s_scr], body, nchunk)


def _gla_mixer(ctx_p, lat_p, a_w, a_b):
    bsz = ctx_p[0].shape[0]
    aw = jnp.zeros((2, SMALL, GLA_QK), F32)
    for d in range(2):
        aw = aw.at[d, d * GLA_RANK:(d + 1) * GLA_RANK].set(a_w[d])
    ab = a_b.reshape(2, 1, GLA_QK)
    st = (jnp.zeros((2, bsz, GLA_DV, GLA_QK), F32),)
    cf, cb, st = _scan_call(_gla_kernel, list(ctx_p), [aw, ab], st, HALF)
    lf, lb, _ = _scan_call(_gla_kernel, list(lat_p), [aw, ab], st, HALF)
    return (cf, cb), (lf, lb)


def _mlstm_chunk(qk, v, gates, c_old, n_all, m_all, tri, head, mlane, eye, d):
    q = qk[:, 0:ML_QK].astype(F32)
    k = qk[:, ML_QK:2 * ML_QK].astype(F32) * (ML_DQK ** -0.5)
    trif = tri.astype(F32)
    gates_t = _mm_nt_hi(eye, gates)
    lf, lf_t = _log_sigmoid(gates), _log_sigmoid(gates_t)
    b_col = _mm_hi(trif, lf)
    b_row = _mm_nt_hi(lf_t, trif)
    b_tot = jnp.sum(lf, axis=0, keepdims=True)
    n_old = n_all[0:1, :]
    keep = jnp.zeros((1, ML_QK), F32)
    c_add = jnp.zeros(c_old.shape, F32)
    n_add = jnp.zeros((1, ML_QK), F32)
    m_next = m_all
    outs = []
    for h in range(ML_HEADS):
        ic, fc = d * 8 + h, d * 8 + 4 + h
        bc, br = b_col[:, fc:fc + 1], b_row[fc:fc + 1, :]
        li_c, li_r = gates[:, ic:ic + 1], gates_t[ic:ic + 1, :]
        bl = b_tot[:, fc:fc + 1]
        m_old = m_all[0:1, h:h + 1]
        sel = head == h
        qm = jnp.where(sel, q, 0.0)
        vh = v[:, h * ML_DV:(h + 1) * ML_DV]
        dmat = jnp.where(tri, bc - br + li_r, -jnp.inf)
        g = bc + m_old
        m = jnp.maximum(g, jnp.max(dmat, axis=1, keepdims=True))
        s = _mm_nt(qm, k) * jnp.exp(dmat - m)
        e = jnp.exp(g - m)
        num = _mm(s, vh) + e * _mm_nt(qm, c_old)
        den = jnp.sum(s, axis=1, keepdims=True) + e * jnp.sum(qm * n_old, axis=1, keepdims=True)
        outs.append(num / jnp.maximum(jnp.abs(den), jnp.exp(-m)))
        a = bl - bc + li_c
        a_max = jnp.max(a, axis=0, keepdims=True)
        kw = jnp.where(sel, k, 0.0) * jnp.exp(a - a_max)
        m_new = jnp.maximum(bl + m_old, a_max)
        w_old, w_new = jnp.exp(bl + m_old - m_new), jnp.exp(a_max - m_new)
        keep = keep + jnp.where(sel, w_old, 0.0)
        c_add = c_add + w_new * _mm_tn(vh, kw)
        n_add = n_add + w_new * jnp.sum(kw, axis=0, keepdims=True)
        m_next = jnp.where(mlane == h, m_new, m_next)
    return outs, c_old * keep + c_add, jnp.broadcast_to(n_old * keep + n_add, n_all.shape), m_next


def _mlstm_kernel(qkf_ref, vf_ref, gf_ref, qkb_ref, vb_ref, gb_ref, bias_ref, c0_ref, n0_ref, m0_ref,
                  hf_ref, hb_ref, cout_ref, nout_ref, mout_ref, c_scr, n_scr, m_scr, *, nchunk):
    head = lax.broadcasted_iota(jnp.int32, (1, ML_QK), 1) // ML_DQK
    mlane = lax.broadcasted_iota(jnp.int32, (8, LANES), 1)
    eye = _eye(SMALL)
    tris = (_tri(False), _tri(True))
    dirs = ((qkf_ref, vf_ref, gf_ref, hf_ref), (qkb_ref, vb_ref, gb_ref, hb_ref))

    def body(ci, carry):
        for d, (qk_ref, v_ref, g_ref, h_ref) in enumerate(dirs):
            rows = _chunk_rows(ci, nchunk, d == 1)
            for b in range(qk_ref.shape[0]):
                outs, c_new, n_new, m_new = _mlstm_chunk(
                    qk_ref[b, rows, :], v_ref[b, rows, :], g_ref[b, rows, :] + bias_ref[...],
                    c_scr[d, b], n_scr[d, b], m_scr[d, b], tris[d], head, mlane, eye, d)
                for h, oh in enumerate(outs):
                    h_ref[b, rows, h * ML_DV:(h + 1) * ML_DV] = oh
                c_scr[d, b] = c_new
                n_scr[d, b] = n_new
                m_scr[d, b] = m_new
        return carry

    _scan_states(pl.program_id(0), [c0_ref, n0_ref, m0_ref], [cout_ref, nout_ref, mout_ref],
                 [c_scr, n_scr, m_scr], body, nchunk)


def _mlstm_mixer(ctx_p, lat_p, gate_b):
    bsz = ctx_p[0].shape[0]
    st = (jnp.zeros((2, bsz, ML_DV, ML_QK), F32), jnp.zeros((2, bsz, 8, ML_QK), F32),
          jnp.zeros((2, bsz, 8, LANES), F32))
    gb = jnp.zeros((1, SMALL), F32).at[0, :16].set(gate_b.reshape(16))
    cf, cb, st = _scan_call(_mlstm_kernel, list(ctx_p), [gb], st, HALF)
    lf, lb, _ = _scan_call(_mlstm_kernel, list(lat_p), [gb], st, HALF)
    return (cf, cb), (lf, lb)


def _ssd_chunk(x, bm, cm, raw, alr, alc, expand, s_old, tri, sub, eye, d):
    x, bm, cm = x.astype(F32), bm.astype(F32), cm.astype(F32)
    gw = SSD_HPG * SSD_HEADDIM
    trif = tri.astype(F32)
    raw_t = _mm_nt_hi(eye, raw)
    dt, dt_t = _softplus(raw), _softplus(raw_t)
    da, da_t = dt * (-jnp.exp(alr)), dt_t * (-jnp.exp(alc))
    cum = _mm_hi(trif, da)
    cum_t = _mm_nt_hi(da_t, trif)
    cl = jnp.sum(da, axis=0, keepdims=True)
    e_cum = _mm_hi(jnp.exp(cum), expand)
    wgt = _mm_hi(jnp.exp(cl - cum) * dt, expand)
    dec = _mm_hi(jnp.broadcast_to(jnp.exp(cl), (8, SMALL)), expand)[0:1, :]
    xw = x * wgt
    ys, news = [], []
    for g in range(SSD_GROUPS):
        ln = slice(g * gw, (g + 1) * gw)
        cmg = cm[:, g * SSD_STATE:(g + 1) * SSD_STATE]
        bmg = bm[:, g * SSD_STATE:(g + 1) * SSD_STATE]
        xg = x[:, ln]
        cb = _mm_nt(cmg, bmg)
        yg = _mm(cmg, s_old[:, ln]) * e_cum[:, ln]
        for j in range(SSD_HPG):
            col = 16 + d * 8 + g * SSD_HPG + j
            seg = jnp.exp(jnp.where(tri, cum[:, col:col + 1] - cum_t[col:col + 1, :], -jnp.inf))
            mix = cb * seg * dt_t[col:col + 1, :]
            yg = yg + _mm(mix, jnp.where(sub == j, xg, 0.0))
        ys.append(yg)
        news.append(s_old[:, ln] * dec[:, ln] + _mm_tn(bmg, xw[:, ln]))
    return ys, news


def _ssd_kernel(xf_ref, bmf_ref, cmf_ref, rf_ref, xb_ref, bmb_ref, cmb_ref, rb_ref,
                dtb_ref, alr_ref, alc_ref, ex_ref, s0_ref, yf_ref, yb_ref, sout_ref, s_scr, *, nchunk):
    gw = SSD_HPG * SSD_HEADDIM
    sub = lax.broadcasted_iota(jnp.int32, (1, gw), 1) // SSD_HEADDIM
    eye = _eye(SMALL)
    tris = (_tri(False), _tri(True))
    dirs = ((xf_ref, bmf_ref, cmf_ref, rf_ref, yf_ref), (xb_ref, bmb_ref, cmb_ref, rb_ref, yb_ref))

    def body(ci, carry):
        for d, (x_ref, bm_ref, cm_ref, r_ref, y_ref) in enumerate(dirs):
            rows = _chunk_rows(ci, nchunk, d == 1)
            for b in range(x_ref.shape[0]):
                ys, news = _ssd_chunk(x_ref[b, rows, :], bm_ref[b, rows, :], cm_ref[b, rows, :],
                                      r_ref[b, rows, :] + dtb_ref[...], alr_ref[...], alc_ref[...],
                                      ex_ref[d], s_scr[d, b], tris[d], sub, eye, d)
                for g in range(SSD_GROUPS):
                    y_ref[b, rows, g * gw:(g + 1) * gw] = ys[g]
                    s_scr[d, b, :, g * gw:(g + 1) * gw] = news[g]
        return carry

    _scan_states(pl.program_id(0), [s0_ref], [sout_ref], [s_scr], body, nchunk)


def _ssd_mixer(ctx_p, lat_p, dt_bias, a_log):
    bsz = ctx_p[0].shape[0]
    st = (jnp.zeros((2, bsz, SSD_STATE, HALF), F32),)
    dtb = jnp.zeros((1, SMALL), F32).at[0, 16:].set(dt_bias.reshape(16))
    al = jnp.zeros((1, SMALL), F32).at[0, 16:].set(a_log.reshape(16).astype(F32))
    lane_head = np.arange(HALF) // SSD_HEADDIM
    ex = jnp.asarray(np.stack([(np.arange(SMALL)[:, None] == (16 + d * 8 + lane_head)[None, :])
                               for d in range(2)]).astype(np.float32))
    consts = [dtb, al, al.reshape(SMALL, 1), ex]
    cf, cb, st = _scan_call(_ssd_kernel, list(ctx_p), consts, st, HALF)
    lf, lb, _ = _scan_call(_ssd_kernel, list(lat_p), consts, st, HALF)
    return (cf, cb), (lf, lb)


def _hy_filter_kernel(f_ref, w1t_ref, w1c_ref, w1s_ref, b1_ref, w2_ref, b2_ref, fq1_ref, fq2_ref, w3s_ref, w3b_ref,
                      dl_ref, o_ref, s_ref, *, n, npiece):
    side, jt = pl.program_id(0), pl.program_id(1)
    tnorm = 1.0 / max(n - 1, 1)
    first = (side == 0) & (jt == 0)

    @pl.when(first)
    def _():
        s_ref[...] = jnp.zeros(s_ref.shape, F32)

    def mlp(pos):
        ang = pos * f_ref[...] * (2.0 * math.pi / n)
        pre = (pos * tnorm) * w1t_ref[...] + _mm_hi(jnp.cos(ang), w1c_ref[...]) \
            + _mm_hi(-jnp.sin(ang), w1s_ref[...]) + b1_ref[...]
        hid = jnp.sin(fq1_ref[...] * pre)
        return jnp.sin(fq2_ref[...] * (_mm_hi(hid, w2_ref[...]) + b2_ref[...]))

    h00 = _mm_nt(w3b_ref[...], mlp(jnp.zeros((LANES, 1), F32)))[:, 0:1]
    acc = jnp.where(first, jnp.abs(h00), 0.0)
    for q in range(npiece):
        tau0 = (jt * npiece + q) * LANES
        tau_c = tau0 + lax.broadcasted_iota(jnp.int32, (LANES, 1), 0)
        tau_r = tau0 + lax.broadcasted_iota(jnp.int32, (1, LANES), 1)
        pos_c = (side * n + (1 - 2 * side) * tau_c).astype(F32)
        pos_r = (side * n + (1 - 2 * side) * tau_r).astype(F32)
        ht = _mm_nt(w3s_ref[...], mlp(pos_c))
        ht = ht * jnp.exp(-(pos_r * tnorm) * dl_ref[...])
        ht = jnp.where((tau_r == 0) & (side == 1), 0.0, ht)
        acc = acc + jnp.sum(jnp.abs(ht), axis=1, keepdims=True)
        o_ref[:, q, :] = ht + jnp.where((tau_r == 0) & (side == 0), h00, 0.0)
    s_ref[...] += acc


def _hyena_filter(n, w1, b1, w2, b2, w3, freq):
    kk = n // LANES
    npiece = min(8, kk)
    ff = w1.shape[1]
    pad = LANES - ff
    bands = jnp.linspace(1e-4, HY_BANDS - 1, HY_BANDS, dtype=F32).reshape(1, HY_BANDS)
    deltas = jnp.abs(jnp.linspace(math.log(HY_TARGET) / HY_SLOW_DECAY, math.log(HY_TARGET) / HY_FAST_DECAY,
                                  HY_CH, dtype=F32))
    dl = jnp.tile(deltas, HY_ORDER).reshape(HY_ROWS, 1)
    w3t = jnp.transpose(w3.reshape(ff, HY_ORDER, 2, HY_CH), (2, 1, 3, 0)).reshape(2, HY_ROWS, ff)
    w3t = jnp.pad(w3t, ((0, 0), (0, 0), (0, pad)))
    args = (bands, w1[0:1], w1[1:1 + HY_BANDS], w1[1 + HY_BANDS:], b1.reshape(1, ff),
            jnp.pad(w2, ((0, 0), (0, pad))), jnp.pad(b2.reshape(1, ff), ((0, 0), (0, pad))),
            freq[0:1], jnp.pad(freq[1:2], ((0, 0), (0, pad))))
    filt, norm = pl.pallas_call(
        functools.partial(_hy_filter_kernel, n=n, npiece=npiece),
        grid=(2, kk // npiece),
        in_specs=[_resident(a.shape) for a in args]
        + [pl.BlockSpec((None, HY_ROWS, LANES), lambda sd, j: (sd, 0, 0)),
           pl.BlockSpec((None, HY_ROWS, LANES), lambda sd, j: (1, 0, 0)), _resident(dl.shape)],
        out_specs=[pl.BlockSpec((HY_ROWS, None, npiece, LANES), lambda sd, j: (0, sd, j, 0)),
                   pl.BlockSpec((HY_ROWS, 1), lambda sd, j: (0, 0))],
        out_shape=[jax.ShapeDtypeStruct((HY_ROWS, 2, kk, LANES), F32),
                   jax.ShapeDtypeStruct((HY_ROWS, 1), F32)],
        name="hy_filter",
        compiler_params=_cparams("arbitrary", "arbitrary"),
    )(*args, w3t, w3t, dl)
    return filt.reshape(HY_ROWS, 2 * kk, LANES), norm


def _dft_tables(n):
    big = 2 * n
    n1 = big // LANES
    k = np.arange(n1)

    def cis(num, den):
        ang = 2.0 * np.pi * (num % den) / den
        return np.stack([np.cos(ang), -np.sin(ang)]).astype(np.float32)

    f1 = cis(k[:, None] * k[None, :], n1)
    k2 = np.arange(LANES)
    f2 = cis(k2[:, None] * k2[None, :], LANES)
    tw = cis(k[:, None] * k2[None, :], big)
    w2 = np.concatenate([f2[0], f2[1]], axis=1)
    w2c = np.concatenate([f2[0], -f2[1]], axis=1)
    ginv = np.stack([f1[0, :n1 // 2, :], -f1[1, :n1 // 2, :]]) / big
    return tuple(jnp.asarray(a) for a in (f1, tw, w2, w2c, ginv))


def _hy_spec_kernel(f_ref, s_ref, f1_ref, tw_ref, w2_ref, o_ref, *, cb):
    f1r, f1i = f1_ref[0], f1_ref[1]
    twr, twi = tw_ref[0], tw_ref[1]
    n1 = f1r.shape[0]

    def body(gi, carry):
        c0 = gi * 4
        trs, tis = [], []
        for pair in range(2):
            ca = c0 + 2 * pair
            rhs = jnp.concatenate([f_ref[ca], f_ref[ca + 1]], axis=1)
            pr, pi = _mm(f1r, rhs), _mm(f1i, rhs)
            for e in range(2):
                ar, ai = pr[:, e * LANES:(e + 1) * LANES], pi[:, e * LANES:(e + 1) * LANES]
                trs.append(ar * twr - ai * twi)
                tis.append(ar * twi + ai * twr)
        xr, xi = _cplx_mm(jnp.concatenate(trs, axis=0), jnp.concatenate(tis, axis=0), w2_ref[...])
        for e in range(4):
            inv = 1.0 / s_ref[c0 + e]
            o_ref[0, c0 + e] = xr[e * n1:(e + 1) * n1] * inv
            o_ref[1, c0 + e] = xi[e * n1:(e + 1) * n1] * inv
        return carry

    lax.fori_loop(0, cb // 4, body, 0)


def _hy_spectrum(filt, norm, f1, tw, w2, cb=16):
    rows, n1, _ = filt.shape
    return pl.pallas_call(
        functools.partial(_hy_spec_kernel, cb=cb),
        grid=(rows // cb,),
        in_specs=[pl.BlockSpec((cb, n1, LANES), lambda j: (j, 0, 0)),
                  pl.BlockSpec((cb, 1, 1), lambda j: (j, 0, 0)),
                  _resident(f1.shape), _resident(tw.shape), _resident(w2.shape)],
        out_specs=pl.BlockSpec((2, cb, n1, LANES), lambda j: (0, j, 0, 0)),
        out_shape=jax.ShapeDtypeStruct((2, rows, n1, LANES), F32),
        name="hy_spectrum",
        compiler_params=_cparams("arbitrary"),
    )(filt, norm.reshape(rows, 1, 1), f1, tw, w2)


def _hy_conv_kernel(v_ref, x_ref, hf_ref, bias_ref, f1_ref, g_ref, tw_ref, w2_ref, w2c_ref, o_ref, *, cb, grp):
    f1r, f1i = f1_ref[0], f1_ref[1]
    gr, gi = g_ref[0], g_ref[1]
    twr, twi = tw_ref[0], tw_ref[1]
    n1 = f1r.shape[0]

    def body(t, carry):
        c0 = t * grp
        trs, tis = [], []
        for e in range(grp):
            ar, ai = _cplx_mm_left(f1r, f1i, v_ref[0, c0 + e], v_ref[1, c0 + e])
            trs.append(ar * twr - ai * twi)
            tis.append(ar * twi + ai * twr)
        xr, xi = _cplx_mm(jnp.concatenate(trs, axis=0), jnp.concatenate(tis, axis=0), w2_ref[...])
        hr = hf_ref[0, pl.ds(c0, grp)].reshape(grp * n1, LANES)
        hi = hf_ref[1, pl.ds(c0, grp)].reshape(grp * n1, LANES)
        zr, zi = _cplx_mm(xr * hr - xi * hi, xr * hi + xi * hr, w2c_ref[...])
        for e in range(grp):
            zre, zie = zr[e * n1:(e + 1) * n1], zi[e * n1:(e + 1) * n1]
            cr, ci = _cplx_mm_left(gr, gi, zre * twr + zie * twi, zie * twr - zre * twi)
            bz = bias_ref[c0 + e]
            v0, v1 = v_ref[0, c0 + e], v_ref[1, c0 + e]
            o_ref[0, c0 + e] = x_ref[0, c0 + e] * (cr + bz * v0)
            o_ref[1, c0 + e] = x_ref[1, c0 + e] * (ci + bz * v1)
        return carry

    lax.fori_loop(0, cb // grp, body, 0)


def _hy_conv(v, xg, hf, bias3, order, tabs, cb=16, grp=4):
    f1, tw, w2, w2c, ginv = tabs
    bsz, ch, k1, _ = v.shape
    n1 = 2 * k1
    nb = ch // cb
    f1h = f1[:, :, :k1]
    dat = pl.BlockSpec((2, cb, k1, LANES), lambda j: (0, j, 0, 0))
    return pl.pallas_call(
        functools.partial(_hy_conv_kernel, cb=cb, grp=grp),
        grid=(nb,),
        in_specs=[dat, dat,
                  pl.BlockSpec((2, cb, n1, LANES), lambda j: (0, order * nb + j, 0, 0)),
                  pl.BlockSpec((cb, 1, 1), lambda j: (order * nb + j, 0, 0)),
                  _resident(f1h.shape), _resident(ginv.shape), _resident(tw.shape),
                  _resident(w2.shape), _resident(w2c.shape)],
        out_specs=dat,
        out_shape=jax.ShapeDtypeStruct(v.shape, F32),
        name="hy_conv",
        compiler_params=_cparams("arbitrary"),
    )(v, xg, hf, bias3, f1h, ginv, tw, w2, w2c)


def _hy_ctx_kernel(v_ref, x1_ref, x2_ref, filt_ref, s_ref, bias_ref, f_ref, o_ref, *, n):
    k, big = n // LANES, 2 * n
    fr, fi = f_ref[0], f_ref[1]
    c0, c1 = _lanes_of(v_ref, (0,), k), _lanes_of(v_ref, (1,), k)
    for o, x_ref in enumerate((x1_ref, x2_ref)):
        rows = slice(o * HY_CH, (o + 1) * HY_CH)
        filt = jnp.concatenate([filt_ref[rows, q, :] for q in range(2 * k)], axis=1)
        inv = 1.0 / s_ref[rows, :]
        hr, hi = _mm(filt, fr) * inv, _mm(filt, fi) * inv
        xr = _mm(c0, fr[:n]) - _mm(c1, fi[:n])
        xi = _mm(c0, fi[:n]) + _mm(c1, fr[:n])
        yr, yi = xr * hr - xi * hi, xr * hi + xi * hr
        gr, gi = fr[:, :n] * (1.0 / big), fi[:, :n] * (-1.0 / big)
        cr = _mm(yr, gr) - _mm(yi, gi)
        ci = _mm(yr, gi) + _mm(yi, gr)
        b = bias_ref[rows, :]
        c0 = _lanes_of(x_ref, (0,), k) * (cr + b * c0)
        c1 = _lanes_of(x_ref, (1,), k) * (ci + b * c1)
    for q in range(k):
        o_ref[0, :, q, :] = c0[:, q * LANES:(q + 1) * LANES]
        o_ref[1, :, q, :] = c1[:, q * LANES:(q + 1) * LANES]


def _hyena_ctx(v, x1, x2, filt, norm, bias):
    bsz, ch, k, _ = v.shape
    n = k * LANES
    big = 2 * n
    idx = np.arange(big)
    ang = 2.0 * np.pi * ((idx[:, None] * idx[None, :]) % big) / big
    f = jnp.asarray(np.stack([np.cos(ang), -np.sin(ang)]).astype(np.float32))
    args = (v, x1, x2, filt, norm, bias.reshape(HY_ROWS, 1), f)
    return pl.pallas_call(
        functools.partial(_hy_ctx_kernel, n=n),
        grid=(1,),
        in_specs=[_resident(a.shape) for a in args],
        out_specs=pl.BlockSpec(v.shape, lambda j: (0, 0, 0, 0)),
        out_shape=jax.ShapeDtypeStruct(v.shape, F32),
        name="hy_ctx",
        compiler_params=_cparams("arbitrary"),
    )(*args)


def _hyena_lat(v, x1, x2, filt, norm, bias):
    assert v.shape[0] == 2
    n = v.shape[2] * LANES
    tabs = _dft_tables(n)
    hf = _hy_spectrum(filt, norm, tabs[0], tabs[1], tabs[2])
    bias3 = bias.reshape(HY_ROWS, 1, 1)
    z = _hy_conv(v, x1, hf, bias3, 0, tabs)
    return _hy_conv(z, x2, hf, bias3, 1, tabs)


def _even_layer_mix(xc, xl, mod_c, mod_l, w_in, w_out, hy_p, gla_p, ln_g, ln_b, need_ctx):
    conv_w, conv_b, filt_params, bias = hy_p
    a_w, a_b, gla_norm = gla_p
    wp = w_in[:, HY_COLS:].astype(BF16)
    wc = jnp.transpose(w_in[:, :HY_COLS]).astype(BF16)
    cw, cb = jnp.transpose(conv_w), conv_b.reshape(HY_COLS, 1)
    psplit = (2 * GLA_QK, HALF, HALF, SMALL)
    csplit = (HY_CH, HY_CH, HY_CH)
    pc = _inproj(xc, mod_c, wp, wc, cw, cb, psplit, csplit, False, "row", True, 1024)
    pl_ = _inproj(xl, mod_l, wp, wc, cw, cb, psplit, csplit, False, "row", True, 1024)
    g_c, g_l = _gla_mixer((pc[0], pc[1], pc[3]), (pl_[0], pl_[1], pl_[3]), a_w, a_b)
    filt, norm = _hyena_filter(xl.shape[1], *filt_params)
    h_l = _hyena_lat(pl_[4], pl_[5], pl_[6], filt, norm, bias)
    nw = [gla_norm.reshape(1, GLA_DV)]
    specs = [lambda tm: _chan_spec(HY_CH, tm)] + [lambda tm: _seq_spec(tm, HALF)] * 3
    xl = _outproj(_outproj_even_kernel, [h_l, g_l[0], g_l[1], pl_[2]], specs, xl, mod_l, w_out,
                  ln_g, ln_b, nw, "row", 1024)
    if need_ctx:
        filt_c, norm_c = _hyena_filter(xc.shape[1], *filt_params)
        h_c = _hyena_ctx(pc[4], pc[5], pc[6], filt_c, norm_c, bias)
        xc = _outproj(_outproj_even_kernel, [h_c, g_c[0], g_c[1], pc[2]], specs, xc, mod_c, w_out,
                      ln_g, ln_b, nw, "row", 1024)
    return xc, xl


def _odd_layer_mix(xc, xl, mod_c, mod_l, w_in, w_out, ml_p, ssd_p, ln_g, ln_b, need_ctx):
    gate_b, ml_norm = ml_p
    conv_w, conv_b, dt_bias, a_log, d_skip, ssd_norm = ssd_p
    ml_cols = 2 * ML_QK + 2 * HALF + 4 * ML_HEADS
    o0 = 2 * ML_QK + 2 * HALF
    s0 = ml_cols
    wp = jnp.concatenate([w_in[:, :o0], w_in[:, s0:s0 + HALF], w_in[:, o0:ml_cols],
                          w_in[:, s0 + HALF + SSD_CONV_DIM:]], axis=1).astype(BF16)
    wc = w_in[:, s0 + HALF:s0 + HALF + SSD_CONV_DIM].astype(BF16)
    cb = conv_b.reshape(1, SSD_CONV_DIM)
    psplit = (2 * ML_QK, HALF, HALF, HALF, SMALL)
    csplit = (HALF, SSD_GN, SSD_GN)
    pc = _inproj(xc, mod_c, wp, wc, conv_w, cb, psplit, csplit, True, "row", False, 512)
    pl_ = _inproj(xl, mod_l, wp, wc, conv_w, cb, psplit, csplit, True, "col", False, 512)
    m_c, m_l = _mlstm_mixer((pc[0], pc[1], pc[4]), (pl_[0], pl_[1], pl_[4]), gate_b)
    s_c, s_l = _ssd_mixer((pc[5], pc[6], pc[7], pc[4]), (pl_[5], pl_[6], pl_[7], pl_[4]), dt_bias, a_log)
    consts = [ml_norm.reshape(1, ML_DV), jnp.repeat(d_skip, SSD_HEADDIM).reshape(1, HALF),
              ssd_norm.reshape(1, HALF)]
    specs = [lambda tm: _seq_spec(tm, HALF)] * 7
    xl = _outproj(_outproj_odd_kernel, [m_l[0], m_l[1], pl_[2], s_l[0], s_l[1], pl_[5], pl_[3]], specs,
                  xl, mod_l, w_out, ln_g, ln_b, consts, "col", 512)
    if need_ctx:
        xc = _outproj(_outproj_odd_kernel, [m_c[0], m_c[1], pc[2], s_c[0], s_c[1], pc[5], pc[3]], specs,
                      xc, mod_c, w_out, ln_g, ln_b, consts, "row", 512)
    return xc, xl


def kernel(x, c, ctx, c_ctx, w_ada, b_ada, ln_g, ln_b, w_ffn_in, w_ffn_out, w_in_even, w_out_even, hy_conv_w, hy_conv_b, hy_w1, hy_b1, hy_w2, hy_b2, hy_w3, hy_freq, hy_bias, gla_a_w, gla_a_b, gla_norm_w, w_in_odd, w_out_odd, ml_gate_b, ml_norm_w, ssd_conv_w, ssd_conv_b, ssd_dt_bias, ssd_a_log, ssd_d, ssd_norm_w):
    bsz, n, d = x.shape
    depth = w_ada.shape[0]
    assert bsz == 2 and d == D_MODEL
    cond8 = jnp.zeros((8, d), F32).at[:bsz].set(c).at[bsz].set(c_ctx)
    mod_all = _ada_all(cond8, w_ada, b_ada)
    xl, xc = x, ctx
    for l in range(depth):
        last = l == depth - 1
        mod_l = mod_all[l, :bsz].reshape(bsz, N_MOD, d)
        mod_c = jnp.broadcast_to(mod_all[l, bsz].reshape(1, N_MOD, d), (bsz, N_MOD, d))
        w1a, w2a = w_ffn_in[l, 0].astype(BF16), w_ffn_out[l, 0].astype(BF16)
        w1b, w2b = w_ffn_in[l, 1].astype(BF16), w_ffn_out[l, 1].astype(BF16)
        xl = _ffn_step(xl, mod_l, 0, w1a, w2a, ln_g[l, 0], ln_b[l, 0])
        xc = _ffn_step(xc, mod_c, 0, w1a, w2a, ln_g[l, 0], ln_b[l, 0])
        i = l // 2
        if l % 2 == 0:
            hy_p = (hy_conv_w[i], hy_conv_b[i],
                    (hy_w1[i], hy_b1[i], hy_w2[i], hy_b2[i], hy_w3[i], hy_freq[i]), hy_bias[i])
            gla_p = (gla_a_w[i], gla_a_b[i], gla_norm_w[i])
            xc, xl = _even_layer_mix(xc, xl, mod_c, mod_l, w_in_even[i], w_out_even[i].astype(BF16),
                                     hy_p, gla_p, ln_g[l, 1], ln_b[l, 1], not last)
        else:
            ml_p = (ml_gate_b[i], ml_norm_w[i])
            ssd_p = (ssd_conv_w[i], ssd_conv_b[i], ssd_dt_bias[i], ssd_a_log[i], ssd_d[i], ssd_norm_w[i])
            xc, xl = _odd_layer_mix(xc, xl, mod_c, mod_l, w_in_odd[i], w_out_odd[i].astype(BF16),
                                    ml_p, ssd_p, ln_g[l, 2 - 1], ln_b[l, 1], not last)
        xl = _ffn_step(xl, mod_l, 6, w1b, w2b, ln_g[l, 2], ln_b[l, 2])
        if not last:
            xc = _ffn_step(xc, mod_c, 6, w1b, w2b, ln_g[l, 2], ln_b[l, 2])
    return xl
```

```python
import functools
import math

import numpy as np
import jax
import jax.numpy as jnp
from jax import lax
from jax.experimental import pallas as pl
from jax.experimental.pallas import tpu as pltpu

F32 = jnp.float32
BF16 = jnp.bfloat16

D_MODEL = 1024
DEPTH = 4
GRID_W = 64
D_FF = 2816
N_MOD = 9
HALF = D_MODEL // 2
CHUNK = 64
LN_EPS = 1e-5
NORM_EPS = 1e-6

HY_CH = HALF
HY_ORDER = 2
HY_EMB = 33
HY_BANDS = (HY_EMB - 1) // 2
HY_FAST_DECAY = 0.3
HY_SLOW_DECAY = 1.5
HY_TARGET = 1e-2
HY_COLS = 3 * HY_CH
HY_ROWS = HY_ORDER * HY_CH

GLA_HEADS = 4
GLA_DK = HALF // 2 // GLA_HEADS
GLA_DV = HALF // GLA_HEADS
GLA_RANK = 16
GLA_TAU = 16.0
GLA_QK = GLA_HEADS * GLA_DK

ML_HEADS = 4
ML_DQK = HALF // 2 // ML_HEADS
ML_DV = HALF // ML_HEADS
ML_QK = ML_HEADS * ML_DQK

SSD_HEADDIM = 64
SSD_HEADS = HALF // SSD_HEADDIM
SSD_GROUPS = 2
SSD_HPG = SSD_HEADS // SSD_GROUPS
SSD_STATE = 128
SSD_GN = SSD_GROUPS * SSD_STATE
SSD_CONV_DIM = HALF + 2 * SSD_GN

DN_ALPHA = (2.0 * DEPTH) ** 0.25

SMALL = 32
LANES = 128
VMEM_LIMIT = 56 * 1024 * 1024


def _cparams(*sem):
    return pltpu.CompilerParams(dimension_semantics=sem, vmem_limit_bytes=VMEM_LIMIT)


def _mm(a, b):
    return jnp.dot(a.astype(BF16), b.astype(BF16), preferred_element_type=F32)


def _mm_nt(a, b):
    return lax.dot_general(a.astype(BF16), b.astype(BF16), (((1,), (1,)), ((), ())),
                           preferred_element_type=F32)


def _mm_tn(a, b):
    return lax.dot_general(a.astype(BF16), b.astype(BF16), (((0,), (0,)), ((), ())),
                           preferred_element_type=F32)


def _mm_hi(a, b):
    return jnp.dot(a, b, precision=lax.Precision.HIGHEST, preferred_element_type=F32)


def _mm_nt_hi(a, b):
    return lax.dot_general(a, b, (((1,), (1,)), ((), ())), precision=lax.Precision.HIGHEST,
                           preferred_element_type=F32)


def _silu(x):
    return x * jax.nn.sigmoid(x)


def _log_sigmoid(x):
    return jnp.minimum(x, 0.0) - jnp.log1p(jnp.exp(-jnp.abs(x)))


def _softplus(x):
    return jnp.maximum(x, 0.0) + jnp.log1p(jnp.exp(-jnp.abs(x)))


def _layer_norm(y, g, b):
    yc = y - jnp.mean(y, axis=-1, keepdims=True)
    return yc * lax.rsqrt(jnp.mean(yc * yc, axis=-1, keepdims=True) + LN_EPS) * g + b


def _rms(y):
    return y * lax.rsqrt(jnp.mean(y * y, axis=-1, keepdims=True) + NORM_EPS)


def _tri(reverse):
    row = lax.broadcasted_iota(jnp.int32, (CHUNK, CHUNK), 0)
    col = lax.broadcasted_iota(jnp.int32, (CHUNK, CHUNK), 1)
    return (row <= col) if reverse else (row >= col)


def _eye(n):
    return (lax.broadcasted_iota(jnp.int32, (n, n), 0) == lax.broadcasted_iota(jnp.int32, (n, n), 1)).astype(F32)


def _resident(shape):
    nd = len(shape)
    return pl.BlockSpec(shape, lambda *_: (0,) * nd, pipeline_mode=pl.Buffered(1))


def _cplx_mm(ar, ai, w):
    u, v = _mm(ar, w), _mm(ai, w)
    h = w.shape[1] // 2
    return u[:, :h] - v[:, h:], u[:, h:] + v[:, :h]


def _cplx_mm_left(fr, fi, xr, xi):
    rhs = jnp.concatenate([xr, xi], axis=1)
    p, q = _mm(fr, rhs), _mm(fi, rhs)
    h = xr.shape[1]
    return p[:, :h] - q[:, h:], p[:, h:] + q[:, :h]


def _ada_kernel(c_ref, w_ref, b_ref, o_ref):
    o_ref[0] = _mm(_silu(c_ref[...]), w_ref[0]) + b_ref[0]


def _ada_all(cond8, w_ada, b_ada):
    depth, d, nm = w_ada.shape
    tn = nm // 8
    return pl.pallas_call(
        _ada_kernel,
        grid=(depth, nm // tn),
        in_specs=[pl.BlockSpec((8, d), lambda l, j: (0, 0)),
                  pl.BlockSpec((1, d, tn), lambda l, j: (l, 0, j)),
                  pl.BlockSpec((1, 1, tn), lambda l, j: (l, 0, j))],
        out_specs=pl.BlockSpec((1, 8, tn), lambda l, j: (l, 0, j)),
        out_shape=jax.ShapeDtypeStruct((depth, 8, nm), F32),
        name="ada",
        compiler_params=_cparams("arbitrary", "arbitrary"),
    )(cond8, w_ada, b_ada.reshape(depth, 1, nm))


def _stream_dims(x, layout):
    if layout == "col":
        return x.shape[0], x.shape[1] * GRID_W, x.shape[2] // GRID_W
    return x.shape


def _tile_rows(n, layout, tm):
    return n // GRID_W if layout == "col" else min(tm, n)


def _main_spec(n, d, layout, tm):
    if layout == "col":
        return pl.BlockSpec((None, n // GRID_W, d), lambda b, i: (b, 0, i))
    return pl.BlockSpec((None, tm, d), lambda b, i: (b, i, 0))


def _halo_specs(n, d, layout, tm):
    if layout == "col":
        r8 = n // GRID_W // 8
        prev = pl.BlockSpec((None, 8, d), lambda b, i: (b, r8 - 1, jnp.maximum(i - 1, 0)))
        nxt = pl.BlockSpec((None, 8, d), lambda b, i: (b, 0, jnp.minimum(i + 1, GRID_W - 1)))
    else:
        t8, n8 = tm // 8, n // 8
        prev = pl.BlockSpec((None, 8, d), lambda b, i: (b, jnp.maximum(i * t8 - 1, 0), 0))
        nxt = pl.BlockSpec((None, 8, d), lambda b, i: (b, jnp.minimum((i + 1) * t8, n8 - 1), 0))
    return prev, nxt


def _seq_spec(tm, cols):
    return pl.BlockSpec((None, tm, cols), lambda b, i: (b, i, 0))


def _chan_spec(rows, tm):
    return pl.BlockSpec((None, rows, tm // LANES, LANES), lambda b, i: (b, 0, i, 0))


def _lanes_of(ref, lead, pieces):
    return jnp.concatenate([ref[lead + (slice(None), q, slice(None))] for q in range(pieces)], axis=1)


def _ffn_kernel(x_ref, mod_ref, w1_ref, w2_ref, g_ref, b_ref, o_ref, *, mo, nk, in_cv, out_cv):
    d = w2_ref.shape[1]
    pieces = in_cv or out_cv
    if in_cv:
        x = jnp.concatenate([x_ref[:, c * d:(c + 1) * d] for c in range(GRID_W)], axis=0)
    elif pieces:
        x = jnp.concatenate([x_ref[:, c, :] for c in range(GRID_W)], axis=0)
    else:
        x = x_ref[...]
    shift, scale, gate = mod_ref[mo:mo + 1, :], mod_ref[mo + 1:mo + 2, :], mod_ref[mo + 2:mo + 3, :]
    h = (x * (1.0 + scale) + shift).astype(BF16)
    tf = D_FF // nk
    acc = jnp.zeros(x.shape, F32)
    for k in range(nk):
        gt = jnp.dot(h, w1_ref[:, k * tf:(k + 1) * tf], preferred_element_type=F32)
        up = jnp.dot(h, w1_ref[:, D_FF + k * tf:D_FF + (k + 1) * tf], preferred_element_type=F32)
        acc = acc + _mm(_silu(gt) * up, w2_ref[k * tf:(k + 1) * tf, :])
    y = _layer_norm(DN_ALPHA * x + (0.5 * gate) * acc, g_ref[...], b_ref[...])
    if out_cv:
        for c in range(GRID_W):
            o_ref[:, c * d:(c + 1) * d] = y[8 * c:8 * c + 8]
    elif pieces:
        for c in range(GRID_W):
            o_ref[:, c, :] = y[8 * c:8 * c + 8]
    else:
        o_ref[...] = y


def _ffn_step(x, mod, mo, w1, w2, g, b, in_cv=False, out_cv=False, tm=512):
    bsz, n, d = _stream_dims(x, "col" if in_cv else "row")
    rows = n // GRID_W
    if in_cv or out_cv:
        tm = 8 * GRID_W
        cv_spec = pl.BlockSpec((None, 8, GRID_W * d), lambda bb, i: (bb, i, 0))
        tok_spec = pl.BlockSpec((None, 8, GRID_W, d), lambda bb, i: (bb, i, 0, 0))
        x = x if in_cv else x.reshape(bsz, rows, GRID_W, d)
        in_spec, out_spec = (cv_spec if in_cv else tok_spec), (cv_spec if out_cv else tok_spec)
        out_dims = (bsz, rows, GRID_W * d) if out_cv else (bsz, rows, GRID_W, d)
    else:
        tm = min(tm, n)
        in_spec = out_spec = _seq_spec(tm, d)
        out_dims = (bsz, n, d)
    out = pl.pallas_call(
        functools.partial(_ffn_kernel, mo=mo, nk=2, in_cv=in_cv, out_cv=out_cv),
        grid=(bsz, n // tm),
        in_specs=[in_spec,
                  pl.BlockSpec((None, N_MOD, d), lambda bb, i: (bb, 0, 0)),
                  _resident(w1.shape), _resident(w2.shape),
                  _resident((1, d)), _resident((1, d))],
        out_specs=out_spec,
        out_shape=jax.ShapeDtypeStruct(out_dims, F32),
        name="ffn",
        compiler_params=_cparams("arbitrary", "arbitrary"),
    )(x, mod, w1, w2, g.reshape(1, d), b.reshape(1, d))
    return out if out_cv else out.reshape(bsz, n, d)


def _inproj_kernel(x_ref, xp_ref, xn_ref, mod_ref, wp_ref, wc_ref, cw_ref, cb_ref, *out_refs,
                   psplit, csplit, act, chan_major):
    i, last = pl.program_id(1), pl.num_programs(1) - 1
    shift, scale = mod_ref[3:4, :], mod_ref[4:5, :]
    tm = x_ref.shape[0]
    h = x_ref[...] * (1.0 + scale) + shift
    hb = h.astype(BF16)
    o = 0
    for k, w in enumerate(psplit):
        out_refs[k][...] = jnp.dot(hb, wp_ref[:, o:o + w], preferred_element_type=F32).astype(out_refs[k].dtype)
        o += w
    hp = (xp_ref[...] * (1.0 + scale) + shift) * jnp.where(i > 0, 1.0, 0.0)
    hn = (xn_ref[...] * (1.0 + scale) + shift) * jnp.where(i < last, 1.0, 0.0)
    couts = out_refs[len(psplit):]
    if chan_major:
        lane = lax.broadcasted_iota(jnp.int32, (1, tm), 1)
        halo = jnp.concatenate([hp, hn, jnp.zeros((LANES - 16, hp.shape[1]), F32)], axis=0).astype(BF16)
        o = 0
        for k, w in enumerate(csplit):
            wt = wc_ref[o:o + w, :]
            p = _mm_nt(wt, hb)
            p_halo = _mm_nt(wt, halo)
            up = jnp.where(lane == 0, p_halo[:, 7:8], pltpu.roll(p, 1, 1))
            dn = jnp.where(lane == tm - 1, p_halo[:, 8:9], pltpu.roll(p, tm - 1, 1))
            cw = cw_ref[o:o + w, :]
            y = cw[:, 0:1] * up + cw[:, 1:2] * p + cw[:, 2:3] * dn + cb_ref[o:o + w, :]
            for q in range(tm // LANES):
                couts[k][:, q, :] = y[:, q * LANES:(q + 1) * LANES]
            o += w
    else:
        pc = _mm(jnp.concatenate([hp, h, hn], axis=0), wc_ref[...])
        up = pltpu.roll(pc, 1, 0)
        dn = pltpu.roll(pc, tm + 15, 0)
        y = (cw_ref[0:1, :] * up + cw_ref[1:2, :] * pc + cw_ref[2:3, :] * dn + cb_ref[...])[8:8 + tm]
        if act:
            y = _silu(y)
        o = 0
        for k, w in enumerate(csplit):
            couts[k][...] = y[:, o:o + w].astype(couts[k].dtype)
            o += w


def _inproj(xv, mod, wp, wc, cw, cb, psplit, csplit, act, layout, chan_major, tm):
    bsz, n, d = _stream_dims(xv, layout)
    tm = _tile_rows(n, layout, tm)
    prev, nxt = _halo_specs(n, d, layout, tm)
    pdt = [BF16 if w >= LANES else F32 for w in psplit]
    out_specs = [_seq_spec(tm, w) for w in psplit]
    out_shape = [jax.ShapeDtypeStruct((bsz, n, w), dt) for w, dt in zip(psplit, pdt)]
    if chan_major:
        out_specs += [_chan_spec(w, tm) for w in csplit]
        out_shape += [jax.ShapeDtypeStruct((bsz, w, n // LANES, LANES), F32) for w in csplit]
    else:
        out_specs += [_seq_spec(tm, w) for w in csplit]
        out_shape += [jax.ShapeDtypeStruct((bsz, n, w), BF16) for w in csplit]
    return pl.pallas_call(
        functools.partial(_inproj_kernel, psplit=tuple(psplit), csplit=tuple(csplit), act=act,
                          chan_major=chan_major),
        grid=(bsz, n // tm),
        in_specs=[_main_spec(n, d, layout, tm), prev, nxt,
                  pl.BlockSpec((None, N_MOD, d), lambda bb, i: (bb, 0, 0)),
                  _resident(wp.shape), _resident(wc.shape), _resident(cw.shape), _resident(cb.shape)],
        out_specs=out_specs,
        out_shape=out_shape,
        name="inproj_chan" if chan_major else "inproj",
        compiler_params=_cparams("arbitrary", "arbitrary"),
    )(xv, xv, xv, mod, wp, wc, cw, cb)


def _outproj_even_kernel(hy_ref, of_ref, ob_ref, g_ref, x_ref, mod_ref, w_ref, lg_ref, lb_ref, nw_ref, o_ref):
    tm = x_ref.shape[0]
    y = _mm_tn(_lanes_of(hy_ref, (), tm // LANES), w_ref[0:HALF, :])
    o = of_ref[...].astype(F32) + ob_ref[...].astype(F32)
    g = g_ref[...].astype(F32)
    heads = [_rms(o[:, h * GLA_DV:(h + 1) * GLA_DV]) * nw_ref[...] for h in range(GLA_HEADS)]
    y = y + _mm(_silu(g) * jnp.concatenate(heads, axis=1), w_ref[HALF:2 * HALF, :])
    o_ref[...] = _layer_norm(DN_ALPHA * x_ref[...] + mod_ref[5:6, :] * y, lg_ref[...], lb_ref[...])


def _outproj_odd_kernel(hf_ref, hb_ref, og_ref, yf_ref, yb_ref, xs_ref, z_ref, x_ref, mod_ref, w_ref,
                        lg_ref, lb_ref, mnw_ref, dsk_ref, snw_ref, o_ref):
    h = hf_ref[...].astype(F32) + hb_ref[...].astype(F32)
    heads = []
    for k in range(ML_HEADS):
        hk = h[:, k * ML_DV:(k + 1) * ML_DV]
        hc = hk - jnp.mean(hk, axis=-1, keepdims=True)
        heads.append(hc * lax.rsqrt(jnp.mean(hc * hc, axis=-1, keepdims=True) + NORM_EPS) * mnw_ref[...])
    m = jax.nn.sigmoid(og_ref[...].astype(F32)) * jnp.concatenate(heads, axis=1)
    s = yf_ref[...].astype(F32) + yb_ref[...].astype(F32) + dsk_ref[...] * xs_ref[...].astype(F32)
    s = _rms(s * _silu(z_ref[...].astype(F32))) * snw_ref[...]
    y = _mm(m, w_ref[0:HALF, :]) + _mm(s, w_ref[HALF:2 * HALF, :])
    o_ref[...] = _layer_norm(DN_ALPHA * x_ref[...] + mod_ref[5:6, :] * y, lg_ref[...], lb_ref[...])


def _outproj(kern, parts, part_specs, xv, mod, w, g, b, consts, layout, tm):
    bsz, n, d = _stream_dims(xv, layout)
    tm = _tile_rows(n, layout, tm)
    consts = [g.reshape(1, d), b.reshape(1, d)] + consts
    return pl.pallas_call(
        kern,
        grid=(bsz, n // tm),
        in_specs=[s(tm) for s in part_specs] + [_main_spec(n, d, layout, tm),
                  pl.BlockSpec((None, N_MOD, d), lambda bb, i: (bb, 0, 0)), _resident(w.shape)]
        + [_resident(c.shape) for c in consts],
        out_specs=_main_spec(n, d, layout, tm),
        out_shape=jax.ShapeDtypeStruct(xv.shape, F32),
        name=kern.__name__.strip("_"),
        compiler_params=_cparams("arbitrary", "arbitrary"),
    )(*parts, xv, mod, w, *consts)


def _scan_call(kernel, seqs, consts, states, out_cols, tb=512):
    bsz, n, _ = seqs[0].shape
    tb = min(tb, n)
    nblk = n // tb

    def fmap(i):
        return (0, i, 0)

    def bmap(i):
        return (0, nblk - 1 - i, 0)

    in_specs = [pl.BlockSpec((bsz, tb, s.shape[-1]), fmap) for s in seqs]
    in_specs += [pl.BlockSpec((bsz, tb, s.shape[-1]), bmap) for s in seqs]
    in_specs += [_resident(c.shape) for c in consts]
    st_specs = [pl.BlockSpec(s.shape, lambda i, nd=s.ndim: (0,) * nd) for s in states]
    outs = pl.pallas_call(
        functools.partial(kernel, nchunk=tb // CHUNK),
        grid=(nblk,),
        in_specs=in_specs + st_specs,
        out_specs=[pl.BlockSpec((bsz, tb, out_cols), fmap), pl.BlockSpec((bsz, tb, out_cols), bmap)] + st_specs,
        out_shape=[jax.ShapeDtypeStruct((bsz, n, out_cols), BF16)] * 2
        + [jax.ShapeDtypeStruct(s.shape, F32) for s in states],
        scratch_shapes=[pltpu.VMEM(s.shape, F32) for s in states],
        name=kernel.__name__.strip("_"),
        compiler_params=_cparams("arbitrary"),
    )(*seqs, *seqs, *consts, *states)
    return outs[0], outs[1], tuple(outs[2:])


def _chunk_rows(ci, nchunk, reverse):
    c = (nchunk - 1 - ci) if reverse else ci
    return pl.ds(pl.multiple_of(c * CHUNK, CHUNK), CHUNK)


def _chains(ci, nchunk, bsz, unroll):
    return [(d, b, _chunk_rows(ci * unroll + u, nchunk, d == 1))
            for u in range(unroll) for d in range(2) for b in range(bsz)]


def _interleave(gens):
    results = [None] * len(gens)
    live = list(enumerate(gens))
    while live:
        still = []
        for idx, g in live:
            try:
                next(g)
                still.append((idx, g))
            except StopIteration as stop:
                results[idx] = stop.value
        live = still
    return results


def _scan_states(i, s0_refs, sout_refs, scrs, body, trips):
    @pl.when(i == 0)
    def _():
        for s0, scr in zip(s0_refs, scrs):
            scr[...] = s0[...]

    lax.fori_loop(0, trips, body, 0)

    @pl.when(i == pl.num_programs(0) - 1)
    def _():
        for so, scr in zip(sout_refs, scrs):
            so[...] = scr[...]


def _gla_chunk(qk, v, a, aw, ab, state, tri, head):
    heads = range(GLA_HEADS)
    q = qk[:, 0:GLA_QK].astype(F32) * (GLA_DK ** -0.5)
    k = qk[:, GLA_QK:2 * GLA_QK].astype(F32)
    lr = _mm(a, aw)
    yield
    log_a = _log_sigmoid(lr + ab) * (1.0 / GLA_TAU)
    b = _mm_hi(tri.astype(F32), log_a)
    yield
    b_last = jnp.sum(log_a, axis=0, keepdims=True)
    q_dec = q * jnp.exp(b)
    k_inv = k * jnp.exp(-b)
    k_dec = k * jnp.exp(b_last - b)
    vs = [v[:, h * GLA_DV:(h + 1) * GLA_DV] for h in heads]
    qms = [jnp.where(head == h, q_dec, 0.0) for h in heads]
    atts = [_mm_nt(qm, k_inv) for qm in qms]
    kvs = [_mm_tn(vs[h], jnp.where(head == h, k_dec, 0.0)) for h in heads]
    yield
    s_old = state[0]
    inter = [_mm_nt(qm, s_old) for qm in qms]
    state[0] = s_old * jnp.exp(b_last) + sum(kvs[1:], kvs[0])
    outs = [_mm(jnp.where(tri, atts[h], 0.0), vs[h]) for h in heads]
    yield
    return [outs[h] + inter[h] for h in heads]


def _gla_kernel(qkf_ref, vf_ref, af_ref, qkb_ref, vb_ref, ab_ref, aw_ref, abias_ref, s0_ref,
                of_ref, ob_ref, sout_ref, s_scr, *, nchunk):
    head = lax.broadcasted_iota(jnp.int32, (1, GLA_QK), 1) // GLA_DK
    tris = (_tri(False), _tri(True))
    dirs = ((qkf_ref, vf_ref, af_ref, of_ref), (qkb_ref, vb_ref, ab_ref, ob_ref))
    bsz, unroll = qkf_ref.shape[0], 2

    def body(ci, carry):
        chains = _chains(ci, nchunk, bsz, unroll)
        states = {(d, b): [s_scr[d, b]] for d in range(2) for b in range(bsz)}
        done = _interleave([_gla_chunk(dirs[d][0][b, rows, :], dirs[d][1][b, rows, :], dirs[d][2][b, rows, :],
                                       aw_ref[d], abias_ref[d], states[d, b], tris[d], head)
                            for d, b, rows in chains])
        for (d, b, rows), outs in zip(chains, done):
            for h, oh in enumerate(outs):
                dirs[d][3][b, rows, h * GLA_DV:(h + 1) * GLA_DV] = oh.astype(BF16)
        for (d, b), st in states.items():
            s_scr[d, b] = st[0]
        return carry

    _scan_states(pl.program_id(0), [s0_ref], [sout_ref], [s_scr], body, nchunk // unroll)


def _gla_mixer(ctx_p, lat_p, a_w, a_b):
    bsz = ctx_p[0].shape[0]
    aw = jnp.zeros((2, SMALL, GLA_QK), F32)
    for d in range(2):
        aw = aw.at[d, d * GLA_RANK:(d + 1) * GLA_RANK].set(a_w[d])
    ab = a_b.reshape(2, 1, GLA_QK)
    st = (jnp.zeros((2, bsz, GLA_DV, GLA_QK), F32),)
    cf, cb, st = _scan_call(_gla_kernel, list(ctx_p), [aw, ab], st, HALF)
    lf, lb, _ = _scan_call(_gla_kernel, list(lat_p), [aw, ab], st, HALF)
    return (cf, cb), (lf, lb)


def _mlstm_chunk(qk, v, gates, state, tri, head, mlane, eye, d):
    q = qk[:, 0:ML_QK].astype(F32)
    k = qk[:, ML_QK:2 * ML_QK].astype(F32) * (ML_DQK ** -0.5)
    trif = tri.astype(F32)
    heads = range(ML_HEADS)
    gates_t = _mm_nt_hi(eye, gates)
    lf = _log_sigmoid(gates)
    b_col = _mm_hi(trif, lf)
    yield
    b_row = _mm_nt_hi(_log_sigmoid(gates_t), trif)
    b_tot = jnp.sum(lf, axis=0, keepdims=True)
    vs = [v[:, h * ML_DV:(h + 1) * ML_DV] for h in heads]
    qms = [jnp.where(head == h, q, 0.0) for h in heads]
    s_raw = [_mm_nt(qm, k) for qm in qms]
    c_old, n_all, m_all = state
    n_old = n_all[0:1, :]
    inter = [_mm_nt(qm, c_old) for qm in qms]
    keep = jnp.zeros((1, ML_QK), F32)
    c_add = jnp.zeros(c_old.shape, F32)
    n_add = jnp.zeros((1, ML_QK), F32)
    m_next = m_all
    for h in heads:
        ic, fc = d * 8 + h, d * 8 + 4 + h
        bl, m_old = b_tot[:, fc:fc + 1], m_all[0:1, h:h + 1]
        a = bl - b_col[:, fc:fc + 1] + gates[:, ic:ic + 1]
        a_max = jnp.max(a, axis=0, keepdims=True)
        kw = jnp.where(head == h, k, 0.0) * jnp.exp(a - a_max)
        m_new = jnp.maximum(bl + m_old, a_max)
        w_old, w_new = jnp.exp(bl + m_old - m_new), jnp.exp(a_max - m_new)
        keep = keep + jnp.where(head == h, w_old, 0.0)
        c_add = c_add + w_new * _mm_tn(vs[h], kw)
        n_add = n_add + w_new * jnp.sum(kw, axis=0, keepdims=True)
        m_next = jnp.where(mlane == h, m_new, m_next)
    state[:] = [c_old * keep + c_add, jnp.broadcast_to(n_old * keep + n_add, n_all.shape), m_next]
    yield
    ms, es, ss = [], [], []
    for h in heads:
        ic, fc = d * 8 + h, d * 8 + 4 + h
        bc = b_col[:, fc:fc + 1]
        dmat = jnp.where(tri, bc - b_row[fc:fc + 1, :] + gates_t[ic:ic + 1, :], -jnp.inf)
        g = bc + m_all[0:1, h:h + 1]
        m = jnp.maximum(g, jnp.max(dmat, axis=1, keepdims=True))
        ms.append(m)
        es.append(jnp.exp(g - m))
        ss.append(s_raw[h] * jnp.exp(dmat - m))
    nums = [_mm(ss[h], vs[h]) for h in heads]
    yield
    outs = []
    for h in heads:
        num = nums[h] + es[h] * inter[h]
        den = jnp.sum(ss[h], axis=1, keepdims=True) + es[h] * jnp.sum(qms[h] * n_old, axis=1, keepdims=True)
        outs.append(num / jnp.maximum(jnp.abs(den), jnp.exp(-ms[h])))
    return outs


def _mlstm_kernel(qkf_ref, vf_ref, gf_ref, qkb_ref, vb_ref, gb_ref, bias_ref, c0_ref, n0_ref, m0_ref,
                  hf_ref, hb_ref, cout_ref, nout_ref, mout_ref, c_scr, n_scr, m_scr, *, nchunk):
    head = lax.broadcasted_iota(jnp.int32, (1, ML_QK), 1) // ML_DQK
    mlane = lax.broadcasted_iota(jnp.int32, (8, LANES), 1)
    eye = _eye(SMALL)
    tris = (_tri(False), _tri(True))
    dirs = ((qkf_ref, vf_ref, gf_ref, hf_ref), (qkb_ref, vb_ref, gb_ref, hb_ref))
    bsz, unroll = qkf_ref.shape[0], 1

    def body(ci, carry):
        chains = _chains(ci, nchunk, bsz, unroll)
        states = {(d, b): [c_scr[d, b], n_scr[d, b], m_scr[d, b]] for d in range(2) for b in range(bsz)}
        done = _interleave([_mlstm_chunk(
            dirs[d][0][b, rows, :], dirs[d][1][b, rows, :], dirs[d][2][b, rows, :] + bias_ref[...],
            states[d, b], tris[d], head, mlane, eye, d) for d, b, rows in chains])
        for (d, b, rows), outs in zip(chains, done):
            for h, oh in enumerate(outs):
                dirs[d][3][b, rows, h * ML_DV:(h + 1) * ML_DV] = oh.astype(BF16)
        for (d, b), st in states.items():
            c_scr[d, b], n_scr[d, b], m_scr[d, b] = st
        return carry

    _scan_states(pl.program_id(0), [c0_ref, n0_ref, m0_ref], [cout_ref, nout_ref, mout_ref],
                 [c_scr, n_scr, m_scr], body, nchunk // unroll)


def _mlstm_mixer(ctx_p, lat_p, gate_b):
    bsz = ctx_p[0].shape[0]
    st = (jnp.zeros((2, bsz, ML_DV, ML_QK), F32), jnp.zeros((2, bsz, 8, ML_QK), F32),
          jnp.zeros((2, bsz, 8, LANES), F32))
    gb = jnp.zeros((1, SMALL), F32).at[0, :16].set(gate_b.reshape(16))
    cf, cb, st = _scan_call(_mlstm_kernel, list(ctx_p), [gb], st, HALF)
    lf, lb, _ = _scan_call(_mlstm_kernel, list(lat_p), [gb], st, HALF)
    return (cf, cb), (lf, lb)


def _ssd_chunk(x, bm, cm, raw, alr, alc, expand, state, tri, sub, eye, d):
    x, bm, cm = x.astype(F32), bm.astype(F32), cm.astype(F32)
    gw = SSD_HPG * SSD_HEADDIM
    trif = tri.astype(F32)
    groups = range(SSD_GROUPS)
    lns = [slice(g * gw, (g + 1) * gw) for g in groups]
    cms = [cm[:, g * SSD_STATE:(g + 1) * SSD_STATE] for g in groups]
    bms = [bm[:, g * SSD_STATE:(g + 1) * SSD_STATE] for g in groups]
    raw_t = _mm_nt_hi(eye, raw)
    dt = _softplus(raw)
    da = dt * (-jnp.exp(alr))
    cum = _mm_hi(trif, da)
    cbs = [_mm_nt(cms[g], bms[g]) for g in groups]
    yield
    dt_t = _softplus(raw_t)
    cum_t = _mm_nt_hi(dt_t * (-jnp.exp(alc)), trif)
    cl = jnp.sum(da, axis=0, keepdims=True)
    e_cum = _mm_hi(jnp.exp(cum), expand)
    wgt = _mm_hi(jnp.exp(cl - cum) * dt, expand)
    dec = _mm_hi(jnp.broadcast_to(jnp.exp(cl), (8, SMALL)), expand)[0:1, :]
    yield
    xw = x * wgt
    kvs = [_mm_tn(bms[g], xw[:, lns[g]]) for g in groups]
    diags = []
    for g in groups:
        xg = x[:, lns[g]]
        parts = []
        for j in range(SSD_HPG):
            col = 16 + d * 8 + g * SSD_HPG + j
            seg = jnp.exp(jnp.where(tri, cum[:, col:col + 1] - cum_t[col:col + 1, :], -jnp.inf))
            parts.append(_mm(cbs[g] * seg * dt_t[col:col + 1, :], jnp.where(sub == j, xg, 0.0)))
        diags.append(parts)
    yield
    offs = [_mm(cms[g], state[g]) for g in groups]
    state[:] = [state[g] * dec[:, lns[g]] + kvs[g] for g in groups]
    yield
    return [offs[g] * e_cum[:, lns[g]] + sum(diags[g][1:], diags[g][0]) for g in groups]


def _ssd_kernel(xf_ref, bmf_ref, cmf_ref, rf_ref, xb_ref, bmb_ref, cmb_ref, rb_ref,
                dtb_ref, alr_ref, alc_ref, ex_ref, s0_ref, yf_ref, yb_ref, sout_ref, s_scr, *, nchunk):
    gw = SSD_HPG * SSD_HEADDIM
    sub = lax.broadcasted_iota(jnp.int32, (1, gw), 1) // SSD_HEADDIM
    eye = _eye(SMALL)
    tris = (_tri(False), _tri(True))
    dirs = ((xf_ref, bmf_ref, cmf_ref, rf_ref, yf_ref), (xb_ref, bmb_ref, cmb_ref, rb_ref, yb_ref))

    bsz, unroll = xf_ref.shape[0], 2

    def body(ci, carry):
        chains = _chains(ci, nchunk, bsz, unroll)
        states = {(d, b): [s_scr[d, b, :, g * gw:(g + 1) * gw] for g in range(SSD_GROUPS)]
                  for d in range(2) for b in range(bsz)}
        done = _interleave([_ssd_chunk(
            dirs[d][0][b, rows, :], dirs[d][1][b, rows, :], dirs[d][2][b, rows, :],
            dirs[d][3][b, rows, :] + dtb_ref[...], alr_ref[...], alc_ref[...],
            ex_ref[d], states[d, b], tris[d], sub, eye, d) for d, b, rows in chains])
        for (d, b, rows), ys in zip(chains, done):
            for g in range(SSD_GROUPS):
                dirs[d][4][b, rows, g * gw:(g + 1) * gw] = ys[g].astype(BF16)
        for (d, b), st in states.items():
            for g in range(SSD_GROUPS):
                s_scr[d, b, :, g * gw:(g + 1) * gw] = st[g]
        return carry

    _scan_states(pl.program_id(0), [s0_ref], [sout_ref], [s_scr], body, nchunk // unroll)


def _ssd_mixer(ctx_p, lat_p, dt_bias, a_log):
    bsz = ctx_p[0].shape[0]
    st = (jnp.zeros((2, bsz, SSD_STATE, HALF), F32),)
    dtb = jnp.zeros((1, SMALL), F32).at[0, 16:].set(dt_bias.reshape(16))
    al = jnp.zeros((1, SMALL), F32).at[0, 16:].set(a_log.reshape(16).astype(F32))
    lane_head = np.arange(HALF) // SSD_HEADDIM
    ex = jnp.asarray(np.stack([(np.arange(SMALL)[:, None] == (16 + d * 8 + lane_head)[None, :])
                               for d in range(2)]).astype(np.float32))
    consts = [dtb, al, al.reshape(SMALL, 1), ex]
    cf, cb, st = _scan_call(_ssd_kernel, list(ctx_p), consts, st, HALF)
    lf, lb, _ = _scan_call(_ssd_kernel, list(lat_p), consts, st, HALF)
    return (cf, cb), (lf, lb)


def _hy_filter_kernel(f_ref, w1t_ref, w1c_ref, w1s_ref, b1_ref, w2_ref, b2_ref, fq1_ref, fq2_ref, w3s_ref, w3b_ref,
                      dl_ref, o_ref, s_ref, *, n, npiece):
    side, jt = pl.program_id(0), pl.program_id(1)
    tnorm = 1.0 / max(n - 1, 1)
    first = (side == 0) & (jt == 0)

    @pl.when(first)
    def _():
        s_ref[...] = jnp.zeros(s_ref.shape, F32)

    def mlp(pos):
        ang = pos * f_ref[...] * (2.0 * math.pi / n)
        pre = (pos * tnorm) * w1t_ref[...] + _mm_hi(jnp.cos(ang), w1c_ref[...]) \
            + _mm_hi(-jnp.sin(ang), w1s_ref[...]) + b1_ref[...]
        hid = jnp.sin(fq1_ref[...] * pre)
        return jnp.sin(fq2_ref[...] * (_mm_hi(hid, w2_ref[...]) + b2_ref[...]))

    h00 = _mm_nt(w3b_ref[...], mlp(jnp.zeros((LANES, 1), F32)))[:, 0:1]
    acc = jnp.where(first, jnp.abs(h00), 0.0)
    for q in range(npiece):
        tau0 = (jt * npiece + q) * LANES
        tau_c = tau0 + lax.broadcasted_iota(jnp.int32, (LANES, 1), 0)
        tau_r = tau0 + lax.broadcasted_iota(jnp.int32, (1, LANES), 1)
        pos_c = (side * n + (1 - 2 * side) * tau_c).astype(F32)
        pos_r = (side * n + (1 - 2 * side) * tau_r).astype(F32)
        ht = _mm_nt(w3s_ref[...], mlp(pos_c))
        ht = ht * jnp.exp(-(pos_r * tnorm) * dl_ref[...])
        ht = jnp.where((tau_r == 0) & (side == 1), 0.0, ht)
        acc = acc + jnp.sum(jnp.abs(ht), axis=1, keepdims=True)
        o_ref[:, q, :] = ht + jnp.where((tau_r == 0) & (side == 0), h00, 0.0)
    s_ref[...] += acc


def _hyena_filter(n, w1, b1, w2, b2, w3, freq):
    kk = n // LANES
    npiece = min(8, kk)
    ff = w1.shape[1]
    pad = LANES - ff
    bands = jnp.linspace(1e-4, HY_BANDS - 1, HY_BANDS, dtype=F32).reshape(1, HY_BANDS)
    deltas = jnp.abs(jnp.linspace(math.log(HY_TARGET) / HY_SLOW_DECAY, math.log(HY_TARGET) / HY_FAST_DECAY,
                                  HY_CH, dtype=F32))
    dl = jnp.tile(deltas, HY_ORDER).reshape(HY_ROWS, 1)
    w3t = jnp.transpose(w3.reshape(ff, HY_ORDER, 2, HY_CH), (2, 1, 3, 0)).reshape(2, HY_ROWS, ff)
    w3t = jnp.pad(w3t, ((0, 0), (0, 0), (0, pad)))
    args = (bands, w1[0:1], w1[1:1 + HY_BANDS], w1[1 + HY_BANDS:], b1.reshape(1, ff),
            jnp.pad(w2, ((0, 0), (0, pad))), jnp.pad(b2.reshape(1, ff), ((0, 0), (0, pad))),
            freq[0:1], jnp.pad(freq[1:2], ((0, 0), (0, pad))))
    filt, norm = pl.pallas_call(
        functools.partial(_hy_filter_kernel, n=n, npiece=npiece),
        grid=(2, kk // npiece),
        in_specs=[_resident(a.shape) for a in args]
        + [pl.BlockSpec((None, HY_ROWS, LANES), lambda sd, j: (sd, 0, 0)),
           pl.BlockSpec((None, HY_ROWS, LANES), lambda sd, j: (1, 0, 0)), _resident(dl.shape)],
        out_specs=[pl.BlockSpec((HY_ROWS, None, npiece, LANES), lambda sd, j: (0, sd, j, 0)),
                   pl.BlockSpec((HY_ROWS, 1), lambda sd, j: (0, 0))],
        out_shape=[jax.ShapeDtypeStruct((HY_ROWS, 2, kk, LANES), F32),
                   jax.ShapeDtypeStruct((HY_ROWS, 1), F32)],
        name="hy_filter",
        compiler_params=_cparams("arbitrary", "arbitrary"),
    )(*args, w3t, w3t, dl)
    return filt.reshape(HY_ROWS, 2 * kk, LANES), norm


def _dft_tables(n):
    big = 2 * n
    n1 = big // LANES
    k = np.arange(n1)

    def cis(num, den):
        ang = 2.0 * np.pi * (num % den) / den
        return np.stack([np.cos(ang), -np.sin(ang)]).astype(np.float32)

    f1 = cis(k[:, None] * k[None, :], n1)
    k2 = np.arange(LANES)
    f2 = cis(k2[:, None] * k2[None, :], LANES)
    tw = cis(k[:, None] * k2[None, :], big)
    w2 = np.concatenate([f2[0], f2[1]], axis=1)
    w2c = np.concatenate([f2[0], -f2[1]], axis=1)
    ginv = np.stack([f1[0, :n1 // 2, :], -f1[1, :n1 // 2, :]]) / big
    return tuple(jnp.asarray(a) for a in (f1, tw, w2, w2c, ginv))


def _hy_spec_kernel(f_ref, s_ref, f1_ref, tw_ref, w2_ref, o_ref, *, cb):
    f1r, f1i = f1_ref[0], f1_ref[1]
    twr, twi = tw_ref[0], tw_ref[1]
    n1 = f1r.shape[0]

    def body(gi, carry):
        c0 = gi * 4
        trs, tis = [], []
        for pair in range(2):
            ca = c0 + 2 * pair
            rhs = jnp.concatenate([f_ref[ca], f_ref[ca + 1]], axis=1)
            pr, pi = _mm(f1r, rhs), _mm(f1i, rhs)
            for e in range(2):
                ar, ai = pr[:, e * LANES:(e + 1) * LANES], pi[:, e * LANES:(e + 1) * LANES]
                trs.append(ar * twr - ai * twi)
                tis.append(ar * twi + ai * twr)
        xr, xi = _cplx_mm(jnp.concatenate(trs, axis=0), jnp.concatenate(tis, axis=0), w2_ref[...])
        for e in range(4):
            inv = 1.0 / s_ref[c0 + e]
            o_ref[0, c0 + e] = xr[e * n1:(e + 1) * n1] * inv
            o_ref[1, c0 + e] = xi[e * n1:(e + 1) * n1] * inv
        return carry

    lax.fori_loop(0, cb // 4, body, 0)


def _hy_spectrum(filt, norm, f1, tw, w2, cb=16):
    rows, n1, _ = filt.shape
    return pl.pallas_call(
        functools.partial(_hy_spec_kernel, cb=cb),
        grid=(rows // cb,),
        in_specs=[pl.BlockSpec((cb, n1, LANES), lambda j: (j, 0, 0)),
                  pl.BlockSpec((cb, 1, 1), lambda j: (j, 0, 0)),
                  _resident(f1.shape), _resident(tw.shape), _resident(w2.shape)],
        out_specs=pl.BlockSpec((2, cb, n1, LANES), lambda j: (0, j, 0, 0)),
        out_shape=jax.ShapeDtypeStruct((2, rows, n1, LANES), F32),
        name="hy_spectrum",
        compiler_params=_cparams("arbitrary"),
    )(filt, norm.reshape(rows, 1, 1), f1, tw, w2)


def _hy_conv_kernel(v_ref, x_ref, hf_ref, bias_ref, f1_ref, g_ref, tw_ref, w2_ref, w2c_ref, o_ref, *, cb, grp):
    f1r, f1i = f1_ref[0], f1_ref[1]
    gr, gi = g_ref[0], g_ref[1]
    twr, twi = tw_ref[0], tw_ref[1]
    n1 = f1r.shape[0]

    def body(t, carry):
        c0 = t * grp
        trs, tis = [], []
        for e in range(grp):
            ar, ai = _cplx_mm_left(f1r, f1i, v_ref[0, c0 + e], v_ref[1, c0 + e])
            trs.append(ar * twr - ai * twi)
            tis.append(ar * twi + ai * twr)
        xr, xi = _cplx_mm(jnp.concatenate(trs, axis=0), jnp.concatenate(tis, axis=0), w2_ref[...])
        hr = hf_ref[0, pl.ds(c0, grp)].reshape(grp * n1, LANES)
        hi = hf_ref[1, pl.ds(c0, grp)].reshape(grp * n1, LANES)
        zr, zi = _cplx_mm(xr * hr - xi * hi, xr * hi + xi * hr, w2c_ref[...])
        for e in range(grp):
            zre, zie = zr[e * n1:(e + 1) * n1], zi[e * n1:(e + 1) * n1]
            cr, ci = _cplx_mm_left(gr, gi, zre * twr + zie * twi, zie * twr - zre * twi)
            bz = bias_ref[c0 + e]
            v0, v1 = v_ref[0, c0 + e], v_ref[1, c0 + e]
            o_ref[0, c0 + e] = x_ref[0, c0 + e] * (cr + bz * v0)
            o_ref[1, c0 + e] = x_ref[1, c0 + e] * (ci + bz * v1)
        return carry

    lax.fori_loop(0, cb // grp, body, 0)


def _hy_conv(v, xg, hf, bias3, order, tabs, cb=16, grp=4):
    f1, tw, w2, w2c, ginv = tabs
    bsz, ch, k1, _ = v.shape
    n1 = 2 * k1
    nb = ch // cb
    f1h = f1[:, :, :k1]
    dat = pl.BlockSpec((2, cb, k1, LANES), lambda j: (0, j, 0, 0))
    return pl.pallas_call(
        functools.partial(_hy_conv_kernel, cb=cb, grp=grp),
        grid=(nb,),
        in_specs=[dat, dat,
                  pl.BlockSpec((2, cb, n1, LANES), lambda j: (0, order * nb + j, 0, 0)),
                  pl.BlockSpec((cb, 1, 1), lambda j: (order * nb + j, 0, 0)),
                  _resident(f1h.shape), _resident(ginv.shape), _resident(tw.shape),
                  _resident(w2.shape), _resident(w2c.shape)],
        out_specs=dat,
        out_shape=jax.ShapeDtypeStruct(v.shape, F32),
        name="hy_conv",
        compiler_params=_cparams("arbitrary"),
    )(v, xg, hf, bias3, f1h, ginv, tw, w2, w2c)


def _hy_ctx_kernel(v_ref, x1_ref, x2_ref, filt_ref, s_ref, bias_ref, f_ref, o_ref, *, n):
    k, big = n // LANES, 2 * n
    fr, fi = f_ref[0], f_ref[1]
    c0, c1 = _lanes_of(v_ref, (0,), k), _lanes_of(v_ref, (1,), k)
    for o, x_ref in enumerate((x1_ref, x2_ref)):
        rows = slice(o * HY_CH, (o + 1) * HY_CH)
        filt = jnp.concatenate([filt_ref[rows, q, :] for q in range(2 * k)], axis=1)
        inv = 1.0 / s_ref[rows, :]
        hr, hi = _mm(filt, fr) * inv, _mm(filt, fi) * inv
        xr = _mm(c0, fr[:n]) - _mm(c1, fi[:n])
        xi = _mm(c0, fi[:n]) + _mm(c1, fr[:n])
        yr, yi = xr * hr - xi * hi, xr * hi + xi * hr
        gr, gi = fr[:, :n] * (1.0 / big), fi[:, :n] * (-1.0 / big)
        cr = _mm(yr, gr) - _mm(yi, gi)
        ci = _mm(yr, gi) + _mm(yi, gr)
        b = bias_ref[rows, :]
        c0 = _lanes_of(x_ref, (0,), k) * (cr + b * c0)
        c1 = _lanes_of(x_ref, (1,), k) * (ci + b * c1)
    for q in range(k):
        o_ref[0, :, q, :] = c0[:, q * LANES:(q + 1) * LANES]
        o_ref[1, :, q, :] = c1[:, q * LANES:(q + 1) * LANES]


def _hyena_ctx(v, x1, x2, filt, norm, bias):
    bsz, ch, k, _ = v.shape
    n = k * LANES
    big = 2 * n
    idx = np.arange(big)
    ang = 2.0 * np.pi * ((idx[:, None] * idx[None, :]) % big) / big
    f = jnp.asarray(np.stack([np.cos(ang), -np.sin(ang)]).astype(np.float32))
    args = (v, x1, x2, filt, norm, bias.reshape(HY_ROWS, 1), f)
    return pl.pallas_call(
        functools.partial(_hy_ctx_kernel, n=n),
        grid=(1,),
        in_specs=[_resident(a.shape) for a in args],
        out_specs=pl.BlockSpec(v.shape, lambda j: (0, 0, 0, 0)),
        out_shape=jax.ShapeDtypeStruct(v.shape, F32),
        name="hy_ctx",
        compiler_params=_cparams("arbitrary"),
    )(*args)


def _hyena_lat(v, x1, x2, filt, norm, bias):
    assert v.shape[0] == 2
    n = v.shape[2] * LANES
    tabs = _dft_tables(n)
    hf = _hy_spectrum(filt, norm, tabs[0], tabs[1], tabs[2])
    bias3 = bias.reshape(HY_ROWS, 1, 1)
    z = _hy_conv(v, x1, hf, bias3, 0, tabs)
    return _hy_conv(z, x2, hf, bias3, 1, tabs)


def _even_layer_mix(xc, xl, mod_c, mod_l, w_in, w_out, hy_p, gla_p, ln_g, ln_b, need_ctx):
    conv_w, conv_b, filt_params, bias = hy_p
    a_w, a_b, gla_norm = gla_p
    wp = w_in[:, HY_COLS:].astype(BF16)
    wc = jnp.transpose(w_in[:, :HY_COLS]).astype(BF16)
    cw, cb = jnp.transpose(conv_w), conv_b.reshape(HY_COLS, 1)
    psplit = (2 * GLA_QK, HALF, HALF, SMALL)
    csplit = (HY_CH, HY_CH, HY_CH)
    pc = _inproj(xc, mod_c, wp, wc, cw, cb, psplit, csplit, False, "row", True, 1024)
    pl_ = _inproj(xl, mod_l, wp, wc, cw, cb, psplit, csplit, False, "row", True, 1024)
    g_c, g_l = _gla_mixer((pc[0], pc[1], pc[3]), (pl_[0], pl_[1], pl_[3]), a_w, a_b)
    filt, norm = _hyena_filter(xl.shape[1], *filt_params)
    h_l = _hyena_lat(pl_[4], pl_[5], pl_[6], filt, norm, bias)
    nw = [gla_norm.reshape(1, GLA_DV)]
    specs = [lambda tm: _chan_spec(HY_CH, tm)] + [lambda tm: _seq_spec(tm, HALF)] * 3
    xl = _outproj(_outproj_even_kernel, [h_l, g_l[0], g_l[1], pl_[2]], specs, xl, mod_l, w_out,
                  ln_g, ln_b, nw, "row", 1024)
    if need_ctx:
        filt_c, norm_c = _hyena_filter(xc.shape[1], *filt_params)
        h_c = _hyena_ctx(pc[4], pc[5], pc[6], filt_c, norm_c, bias)
        xc = _outproj(_outproj_even_kernel, [h_c, g_c[0], g_c[1], pc[2]], specs, xc, mod_c, w_out,
                      ln_g, ln_b, nw, "row", 1024)
    return xc, xl


def _odd_layer_mix(xc, xl, mod_c, mod_l, w_in, w_out, ml_p, ssd_p, ln_g, ln_b, need_ctx):
    gate_b, ml_norm = ml_p
    conv_w, conv_b, dt_bias, a_log, d_skip, ssd_norm = ssd_p
    ml_cols = 2 * ML_QK + 2 * HALF + 4 * ML_HEADS
    o0 = 2 * ML_QK + 2 * HALF
    s0 = ml_cols
    wp = jnp.concatenate([w_in[:, :o0], w_in[:, s0:s0 + HALF], w_in[:, o0:ml_cols],
                          w_in[:, s0 + HALF + SSD_CONV_DIM:]], axis=1).astype(BF16)
    wc = w_in[:, s0 + HALF:s0 + HALF + SSD_CONV_DIM].astype(BF16)
    cb = conv_b.reshape(1, SSD_CONV_DIM)
    psplit = (2 * ML_QK, HALF, HALF, HALF, SMALL)
    csplit = (HALF, SSD_GN, SSD_GN)
    pc = _inproj(xc, mod_c, wp, wc, conv_w, cb, psplit, csplit, True, "row", False, 512)
    pl_ = _inproj(xl, mod_l, wp, wc, conv_w, cb, psplit, csplit, True, "col", False, 512)
    m_c, m_l = _mlstm_mixer((pc[0], pc[1], pc[4]), (pl_[0], pl_[1], pl_[4]), gate_b)
    s_c, s_l = _ssd_mixer((pc[5], pc[6], pc[7], pc[4]), (pl_[5], pl_[6], pl_[7], pl_[4]), dt_bias, a_log)
    consts = [ml_norm.reshape(1, ML_DV), jnp.repeat(d_skip, SSD_HEADDIM).reshape(1, HALF),
              ssd_norm.reshape(1, HALF)]
    specs = [lambda tm: _seq_spec(tm, HALF)] * 7
    xl = _outproj(_outproj_odd_kernel, [m_l[0], m_l[1], pl_[2], s_l[0], s_l[1], pl_[5], pl_[3]], specs,
                  xl, mod_l, w_out, ln_g, ln_b, consts, "col", 512)
    if need_ctx:
        xc = _outproj(_outproj_odd_kernel, [m_c[0], m_c[1], pc[2], s_c[0], s_c[1], pc[5], pc[3]], specs,
                      xc, mod_c, w_out, ln_g, ln_b, consts, "row", 512)
    return xc, xl


def kernel(x, c, ctx, c_ctx, w_ada, b_ada, ln_g, ln_b, w_ffn_in, w_ffn_out, w_in_even, w_out_even, hy_conv_w, hy_conv_b, hy_w1, hy_b1, hy_w2, hy_b2, hy_w3, hy_freq, hy_bias, gla_a_w, gla_a_b, gla_norm_w, w_in_odd, w_out_odd, ml_gate_b, ml_norm_w, ssd_conv_w, ssd_conv_b, ssd_dt_bias, ssd_a_log, ssd_d, ssd_norm_w):
    bsz, n, d = x.shape
    depth = w_ada.shape[0]
    assert bsz == 2 and d == D_MODEL
    cond8 = jnp.zeros((8, d), F32).at[:bsz].set(c).at[bsz].set(c_ctx)
    mod_all = _ada_all(cond8, w_ada, b_ada)
    xl, xc = x, ctx
    for l in range(depth):
        last = l == depth - 1
        mod_l = mod_all[l, :bsz].reshape(bsz, N_MOD, d)
        mod_c = jnp.broadcast_to(mod_all[l, bsz].reshape(1, N_MOD, d), (bsz, N_MOD, d))
        w1a, w2a = w_ffn_in[l, 0].astype(BF16), w_ffn_out[l, 0].astype(BF16)
        w1b, w2b = w_ffn_in[l, 1].astype(BF16), w_ffn_out[l, 1].astype(BF16)
        odd = l % 2 == 1
        xl = _ffn_step(xl, mod_l, 0, w1a, w2a, ln_g[l, 0], ln_b[l, 0], out_cv=odd)
        xc = _ffn_step(xc, mod_c, 0, w1a, w2a, ln_g[l, 0], ln_b[l, 0])
        i = l // 2
        if not odd:
            hy_p = (hy_conv_w[i], hy_conv_b[i],
                    (hy_w1[i], hy_b1[i], hy_w2[i], hy_b2[i], hy_w3[i], hy_freq[i]), hy_bias[i])
            gla_p = (gla_a_w[i], gla_a_b[i], gla_norm_w[i])
            xc, xl = _even_layer_mix(xc, xl, mod_c, mod_l, w_in_even[i], w_out_even[i].astype(BF16),
                                     hy_p, gla_p, ln_g[l, 1], ln_b[l, 1], not last)
        else:
            ml_p = (ml_gate_b[i], ml_norm_w[i])
            ssd_p = (ssd_conv_w[i], ssd_conv_b[i], ssd_dt_bias[i], ssd_a_log[i], ssd_d[i], ssd_norm_w[i])
            xc, xl = _odd_layer_mix(xc, xl, mod_c, mod_l, w_in_odd[i], w_out_odd[i].astype(BF16),
                                    ml_p, ssd_p, ln_g[l, 1], ln_b[l, 1], not last)
        xl = _ffn_step(xl, mod_l, 6, w1b, w2b, ln_g[l, 2], ln_b[l, 2], in_cv=odd)
        if not last:
            xc = _ffn_step(xc, mod_c, 6, w1b, w2b, ln_g[l, 2], ln_b[l, 2])
    return xl
```

```python
import functools
import math

import numpy as np
import jax
import jax.numpy as jnp
from jax import lax
from jax.experimental import pallas as pl
from jax.experimental.pallas import tpu as pltpu

F32 = jnp.float32
BF16 = jnp.bfloat16

D_MODEL = 1024
DEPTH = 4
GRID_W = 64
D_FF = 2816
N_MOD = 9
HALF = D_MODEL // 2
CHUNK = 64
LN_EPS = 1e-5
NORM_EPS = 1e-6

HY_CH = HALF
HY_ORDER = 2
HY_EMB = 33
HY_BANDS = (HY_EMB - 1) // 2
HY_FAST_DECAY = 0.3
HY_SLOW_DECAY = 1.5
HY_TARGET = 1e-2
HY_COLS = 3 * HY_CH
HY_ROWS = HY_ORDER * HY_CH

GLA_HEADS = 4
GLA_DK = HALF // 2 // GLA_HEADS
GLA_DV = HALF // GLA_HEADS
GLA_RANK = 16
GLA_TAU = 16.0
GLA_QK = GLA_HEADS * GLA_DK

ML_HEADS = 4
ML_DQK = HALF // 2 // ML_HEADS
ML_DV = HALF // ML_HEADS
ML_QK = ML_HEADS * ML_DQK

SSD_HEADDIM = 64
SSD_HEADS = HALF // SSD_HEADDIM
SSD_GROUPS = 2
SSD_HPG = SSD_HEADS // SSD_GROUPS
SSD_STATE = 128
SSD_GN = SSD_GROUPS * SSD_STATE
SSD_CONV_DIM = HALF + 2 * SSD_GN

DN_ALPHA = (2.0 * DEPTH) ** 0.25

SMALL = 32
LANES = 128
VMEM_LIMIT = 56 * 1024 * 1024


def _cparams(*sem):
    return pltpu.CompilerParams(dimension_semantics=sem, vmem_limit_bytes=VMEM_LIMIT)


def _mm(a, b):
    return jnp.dot(a.astype(BF16), b.astype(BF16), preferred_element_type=F32)


def _mm_nt(a, b):
    return lax.dot_general(a.astype(BF16), b.astype(BF16), (((1,), (1,)), ((), ())),
                           preferred_element_type=F32)


def _mm_tn(a, b):
    return lax.dot_general(a.astype(BF16), b.astype(BF16), (((0,), (0,)), ((), ())),
                           preferred_element_type=F32)


def _mm_hi(a, b):
    return jnp.dot(a, b, precision=lax.Precision.HIGHEST, preferred_element_type=F32)


def _mm_nt_hi(a, b):
    return lax.dot_general(a, b, (((1,), (1,)), ((), ())), precision=lax.Precision.HIGHEST,
                           preferred_element_type=F32)


def _silu(x):
    return x * jax.nn.sigmoid(x)


def _log_sigmoid(x):
    return jnp.minimum(x, 0.0) - jnp.log1p(jnp.exp(-jnp.abs(x)))


def _softplus(x):
    return jnp.maximum(x, 0.0) + jnp.log1p(jnp.exp(-jnp.abs(x)))


def _layer_norm(y, g, b):
    yc = y - jnp.mean(y, axis=-1, keepdims=True)
    return yc * lax.rsqrt(jnp.mean(yc * yc, axis=-1, keepdims=True) + LN_EPS) * g + b


def _rms(y):
    return y * lax.rsqrt(jnp.mean(y * y, axis=-1, keepdims=True) + NORM_EPS)


def _tri(reverse):
    row = lax.broadcasted_iota(jnp.int32, (CHUNK, CHUNK), 0)
    col = lax.broadcasted_iota(jnp.int32, (CHUNK, CHUNK), 1)
    return (row <= col) if reverse else (row >= col)


def _eye(n):
    return (lax.broadcasted_iota(jnp.int32, (n, n), 0) == lax.broadcasted_iota(jnp.int32, (n, n), 1)).astype(F32)


def _resident(shape):
    nd = len(shape)
    return pl.BlockSpec(shape, lambda *_: (0,) * nd, pipeline_mode=pl.Buffered(1))


def _cplx_mm(ar, ai, w):
    u, v = _mm(ar, w), _mm(ai, w)
    h = w.shape[1] // 2
    return u[:, :h] - v[:, h:], u[:, h:] + v[:, :h]


def _cplx_mm_left(fr, fi, xr, xi):
    rhs = jnp.concatenate([xr, xi], axis=1)
    p, q = _mm(fr, rhs), _mm(fi, rhs)
    h = xr.shape[1]
    return p[:, :h] - q[:, h:], p[:, h:] + q[:, :h]


def _ada_kernel(c_ref, w_ref, b_ref, o_ref):
    o_ref[0] = _mm(_silu(c_ref[...]), w_ref[0]) + b_ref[0]


def _ada_all(cond8, w_ada, b_ada):
    depth, d, nm = w_ada.shape
    tn = nm // 8
    return pl.pallas_call(
        _ada_kernel,
        grid=(depth, nm // tn),
        in_specs=[pl.BlockSpec((8, d), lambda l, j: (0, 0)),
                  pl.BlockSpec((1, d, tn), lambda l, j: (l, 0, j)),
                  pl.BlockSpec((1, 1, tn), lambda l, j: (l, 0, j))],
        out_specs=pl.BlockSpec((1, 8, tn), lambda l, j: (l, 0, j)),
        out_shape=jax.ShapeDtypeStruct((depth, 8, nm), F32),
        name="ada",
        compiler_params=_cparams("arbitrary", "arbitrary"),
    )(cond8, w_ada, b_ada.reshape(depth, 1, nm))


def _stream_dims(x, layout):
    if layout == "col":
        return x.shape[0], x.shape[1] * GRID_W, x.shape[2] // GRID_W
    return x.shape


def _tile_rows(n, layout, tm):
    return n // GRID_W if layout == "col" else min(tm, n)


def _main_spec(n, d, layout, tm):
    if layout == "col":
        return pl.BlockSpec((None, n // GRID_W, d), lambda b, i: (b, 0, i))
    return pl.BlockSpec((None, tm, d), lambda b, i: (b, i, 0))


def _halo_specs(n, d, layout, tm):
    if layout == "col":
        r8 = n // GRID_W // 8
        prev = pl.BlockSpec((None, 8, d), lambda b, i: (b, r8 - 1, jnp.maximum(i - 1, 0)))
        nxt = pl.BlockSpec((None, 8, d), lambda b, i: (b, 0, jnp.minimum(i + 1, GRID_W - 1)))
    else:
        t8, n8 = tm // 8, n // 8
        prev = pl.BlockSpec((None, 8, d), lambda b, i: (b, jnp.maximum(i * t8 - 1, 0), 0))
        nxt = pl.BlockSpec((None, 8, d), lambda b, i: (b, jnp.minimum((i + 1) * t8, n8 - 1), 0))
    return prev, nxt


def _seq_spec(tm, cols):
    return pl.BlockSpec((None, tm, cols), lambda b, i: (b, i, 0))


def _chan_spec(rows, tm):
    return pl.BlockSpec((None, rows, tm // LANES, LANES), lambda b, i: (b, 0, i, 0))


def _lanes_of(ref, lead, pieces):
    return jnp.concatenate([ref[lead + (slice(None), q, slice(None))] for q in range(pieces)], axis=1)


def _ffn_kernel(x_ref, mod_ref, w1_ref, w2_ref, g_ref, b_ref, o_ref, *, mo, nk, in_cv, out_cv):
    d = w2_ref.shape[1]
    pieces = in_cv or out_cv
    if in_cv:
        x = jnp.concatenate([x_ref[:, c * d:(c + 1) * d] for c in range(GRID_W)], axis=0)
    elif pieces:
        x = jnp.concatenate([x_ref[:, c, :] for c in range(GRID_W)], axis=0)
    else:
        x = x_ref[...]
    shift, scale, gate = mod_ref[mo:mo + 1, :], mod_ref[mo + 1:mo + 2, :], mod_ref[mo + 2:mo + 3, :]
    h = (x * (1.0 + scale) + shift).astype(BF16)
    tf = D_FF // nk
    acc = jnp.zeros(x.shape, F32)
    for k in range(nk):
        gt = jnp.dot(h, w1_ref[:, k * tf:(k + 1) * tf], preferred_element_type=F32)
        up = jnp.dot(h, w1_ref[:, D_FF + k * tf:D_FF + (k + 1) * tf], preferred_element_type=F32)
        acc = acc + _mm(_silu(gt) * up, w2_ref[k * tf:(k + 1) * tf, :])
    y = _layer_norm(DN_ALPHA * x + (0.5 * gate) * acc, g_ref[...], b_ref[...])
    if out_cv:
        for c in range(GRID_W):
            o_ref[:, c * d:(c + 1) * d] = y[8 * c:8 * c + 8]
    elif pieces:
        for c in range(GRID_W):
            o_ref[:, c, :] = y[8 * c:8 * c + 8]
    else:
        o_ref[...] = y


def _ffn_step(x, mod, mo, w1, w2, g, b, in_cv=False, out_cv=False, tm=512):
    bsz, n, d = _stream_dims(x, "col" if in_cv else "row")
    rows = n // GRID_W
    if in_cv or out_cv:
        tm = 8 * GRID_W
        cv_spec = pl.BlockSpec((None, 8, GRID_W * d), lambda bb, i: (bb, i, 0))
        tok_spec = pl.BlockSpec((None, 8, GRID_W, d), lambda bb, i: (bb, i, 0, 0))
        x = x if in_cv else x.reshape(bsz, rows, GRID_W, d)
        in_spec, out_spec = (cv_spec if in_cv else tok_spec), (cv_spec if out_cv else tok_spec)
        out_dims = (bsz, rows, GRID_W * d) if out_cv else (bsz, rows, GRID_W, d)
    else:
        tm = min(tm, n)
        in_spec = out_spec = _seq_spec(tm, d)
        out_dims = (bsz, n, d)
    out = pl.pallas_call(
        functools.partial(_ffn_kernel, mo=mo, nk=2, in_cv=in_cv, out_cv=out_cv),
        grid=(bsz, n // tm),
        in_specs=[in_spec,
                  pl.BlockSpec((None, N_MOD, d), lambda bb, i: (bb, 0, 0)),
                  _resident(w1.shape), _resident(w2.shape),
                  _resident((1, d)), _resident((1, d))],
        out_specs=out_spec,
        out_shape=jax.ShapeDtypeStruct(out_dims, F32),
        name="ffn",
        compiler_params=_cparams("arbitrary", "arbitrary"),
    )(x, mod, w1, w2, g.reshape(1, d), b.reshape(1, d))
    return out if out_cv else out.reshape(bsz, n, d)


def _inproj_kernel(x_ref, xp_ref, xn_ref, mod_ref, wp_ref, wc_ref, cw_ref, cb_ref, *out_refs,
                   psplit, csplit, act, chan_major):
    i, last = pl.program_id(1), pl.num_programs(1) - 1
    shift, scale = mod_ref[3:4, :], mod_ref[4:5, :]
    tm = x_ref.shape[0]
    h = x_ref[...] * (1.0 + scale) + shift
    hb = h.astype(BF16)
    o = 0
    for k, w in enumerate(psplit):
        out_refs[k][...] = jnp.dot(hb, wp_ref[:, o:o + w], preferred_element_type=F32).astype(out_refs[k].dtype)
        o += w
    hp = (xp_ref[...] * (1.0 + scale) + shift) * jnp.where(i > 0, 1.0, 0.0)
    hn = (xn_ref[...] * (1.0 + scale) + shift) * jnp.where(i < last, 1.0, 0.0)
    couts = out_refs[len(psplit):]
    if chan_major:
        lane = lax.broadcasted_iota(jnp.int32, (1, tm), 1)
        halo = jnp.concatenate([hp, hn, jnp.zeros((LANES - 16, hp.shape[1]), F32)], axis=0).astype(BF16)
        o = 0
        for k, w in enumerate(csplit):
            wt = wc_ref[o:o + w, :]
            p = _mm_nt(wt, hb)
            p_halo = _mm_nt(wt, halo)
            up = jnp.where(lane == 0, p_halo[:, 7:8], pltpu.roll(p, 1, 1))
            dn = jnp.where(lane == tm - 1, p_halo[:, 8:9], pltpu.roll(p, tm - 1, 1))
            cw = cw_ref[o:o + w, :]
            y = cw[:, 0:1] * up + cw[:, 1:2] * p + cw[:, 2:3] * dn + cb_ref[o:o + w, :]
            for q in range(tm // LANES):
                couts[k][:, q, :] = y[:, q * LANES:(q + 1) * LANES]
            o += w
    else:
        pc = _mm(jnp.concatenate([hp, h, hn], axis=0), wc_ref[...])
        up = pltpu.roll(pc, 1, 0)
        dn = pltpu.roll(pc, tm + 15, 0)
        y = (cw_ref[0:1, :] * up + cw_ref[1:2, :] * pc + cw_ref[2:3, :] * dn + cb_ref[...])[8:8 + tm]
        if act:
            y = _silu(y)
        o = 0
        for k, w in enumerate(csplit):
            couts[k][...] = y[:, o:o + w].astype(couts[k].dtype)
            o += w


def _inproj(xv, mod, wp, wc, cw, cb, psplit, csplit, act, layout, chan_major, tm):
    bsz, n, d = _stream_dims(xv, layout)
    tm = _tile_rows(n, layout, tm)
    prev, nxt = _halo_specs(n, d, layout, tm)
    pdt = [BF16 if w >= LANES else F32 for w in psplit]
    out_specs = [_seq_spec(tm, w) for w in psplit]
    out_shape = [jax.ShapeDtypeStruct((bsz, n, w), dt) for w, dt in zip(psplit, pdt)]
    if chan_major:
        out_specs += [_chan_spec(w, tm) for w in csplit]
        out_shape += [jax.ShapeDtypeStruct((bsz, w, n // LANES, LANES), F32) for w in csplit]
    else:
        out_specs += [_seq_spec(tm, w) for w in csplit]
        out_shape += [jax.ShapeDtypeStruct((bsz, n, w), BF16) for w in csplit]
    return pl.pallas_call(
        functools.partial(_inproj_kernel, psplit=tuple(psplit), csplit=tuple(csplit), act=act,
                          chan_major=chan_major),
        grid=(bsz, n // tm),
        in_specs=[_main_spec(n, d, layout, tm), prev, nxt,
                  pl.BlockSpec((None, N_MOD, d), lambda bb, i: (bb, 0, 0)),
                  _resident(wp.shape), _resident(wc.shape), _resident(cw.shape), _resident(cb.shape)],
        out_specs=out_specs,
        out_shape=out_shape,
        name="inproj_chan" if chan_major else "inproj",
        compiler_params=_cparams("arbitrary", "arbitrary"),
    )(xv, xv, xv, mod, wp, wc, cw, cb)


def _outproj_even_kernel(hy_ref, of_ref, ob_ref, g_ref, x_ref, mod_ref, w_ref, lg_ref, lb_ref, nw_ref, o_ref):
    tm = x_ref.shape[0]
    y = _mm_tn(_lanes_of(hy_ref, (), tm // LANES), w_ref[0:HALF, :])
    o = of_ref[...].astype(F32) + ob_ref[...].astype(F32)
    g = g_ref[...].astype(F32)
    heads = [_rms(o[:, h * GLA_DV:(h + 1) * GLA_DV]) * nw_ref[...] for h in range(GLA_HEADS)]
    y = y + _mm(_silu(g) * jnp.concatenate(heads, axis=1), w_ref[HALF:2 * HALF, :])
    o_ref[...] = _layer_norm(DN_ALPHA * x_ref[...] + mod_ref[5:6, :] * y, lg_ref[...], lb_ref[...])


def _outproj_odd_kernel(hf_ref, hb_ref, og_ref, yf_ref, yb_ref, xs_ref, z_ref, x_ref, mod_ref, w_ref,
                        lg_ref, lb_ref, mnw_ref, dsk_ref, snw_ref, o_ref):
    h = hf_ref[...].astype(F32) + hb_ref[...].astype(F32)
    heads = []
    for k in range(ML_HEADS):
        hk = h[:, k * ML_DV:(k + 1) * ML_DV]
        hc = hk - jnp.mean(hk, axis=-1, keepdims=True)
        heads.append(hc * lax.rsqrt(jnp.mean(hc * hc, axis=-1, keepdims=True) + NORM_EPS) * mnw_ref[...])
    m = jax.nn.sigmoid(og_ref[...].astype(F32)) * jnp.concatenate(heads, axis=1)
    s = yf_ref[...].astype(F32) + yb_ref[...].astype(F32) + dsk_ref[...] * xs_ref[...].astype(F32)
    s = _rms(s * _silu(z_ref[...].astype(F32))) * snw_ref[...]
    y = _mm(m, w_ref[0:HALF, :]) + _mm(s, w_ref[HALF:2 * HALF, :])
    o_ref[...] = _layer_norm(DN_ALPHA * x_ref[...] + mod_ref[5:6, :] * y, lg_ref[...], lb_ref[...])


def _outproj(kern, parts, part_specs, xv, mod, w, g, b, consts, layout, tm):
    bsz, n, d = _stream_dims(xv, layout)
    tm = _tile_rows(n, layout, tm)
    consts = [g.reshape(1, d), b.reshape(1, d)] + consts
    return pl.pallas_call(
        kern,
        grid=(bsz, n // tm),
        in_specs=[s(tm) for s in part_specs] + [_main_spec(n, d, layout, tm),
                  pl.BlockSpec((None, N_MOD, d), lambda bb, i: (bb, 0, 0)), _resident(w.shape)]
        + [_resident(c.shape) for c in consts],
        out_specs=_main_spec(n, d, layout, tm),
        out_shape=jax.ShapeDtypeStruct(xv.shape, F32),
        name=kern.__name__.strip("_"),
        compiler_params=_cparams("arbitrary", "arbitrary"),
    )(*parts, xv, mod, w, *consts)


def _scan_call(kernel, seqs, consts, states, out_cols, tb=512):
    bsz, n, _ = seqs[0].shape
    tb = min(tb, n)
    nblk = n // tb

    def fmap(i):
        return (0, i, 0)

    def bmap(i):
        return (0, nblk - 1 - i, 0)

    in_specs = [pl.BlockSpec((bsz, tb, s.shape[-1]), fmap) for s in seqs]
    in_specs += [pl.BlockSpec((bsz, tb, s.shape[-1]), bmap) for s in seqs]
    in_specs += [_resident(c.shape) for c in consts]
    st_specs = [pl.BlockSpec(s.shape, lambda i, nd=s.ndim: (0,) * nd) for s in states]
    outs = pl.pallas_call(
        functools.partial(kernel, nchunk=tb // CHUNK),
        grid=(nblk,),
        in_specs=in_specs + st_specs,
        out_specs=[pl.BlockSpec((bsz, tb, out_cols), fmap), pl.BlockSpec((bsz, tb, out_cols), bmap)] + st_specs,
        out_shape=[jax.ShapeDtypeStruct((bsz, n, out_cols), BF16)] * 2
        + [jax.ShapeDtypeStruct(s.shape, F32) for s in states],
        scratch_shapes=[pltpu.VMEM(s.shape, F32) for s in states],
        name=kernel.__name__.strip("_"),
        compiler_params=_cparams("arbitrary"),
    )(*seqs, *seqs, *consts, *states)
    return outs[0], outs[1], tuple(outs[2:])


def _chunk_rows(ci, nchunk, reverse):
    c = (nchunk - 1 - ci) if reverse else ci
    return pl.ds(pl.multiple_of(c * CHUNK, CHUNK), CHUNK)


def _chains(ci, nchunk, bsz, unroll):
    return [(d, b, _chunk_rows(ci * unroll + u, nchunk, d == 1))
            for u in range(unroll) for d in range(2) for b in range(bsz)]


def _interleave(gens):
    results = [None] * len(gens)
    live = list(enumerate(gens))
    while live:
        still = []
        for idx, g in live:
            try:
                next(g)
                still.append((idx, g))
            except StopIteration as stop:
                results[idx] = stop.value
        live = still
    return results


def _scan_states(i, s0_refs, sout_refs, scrs, body, trips):
    @pl.when(i == 0)
    def _():
        for s0, scr in zip(s0_refs, scrs):
            scr[...] = s0[...]

    lax.fori_loop(0, trips, body, 0)

    @pl.when(i == pl.num_programs(0) - 1)
    def _():
        for so, scr in zip(sout_refs, scrs):
            so[...] = scr[...]


def _gla_chunk(qk, v, a, aw, ab, state, tri, head):
    heads = range(GLA_HEADS)
    q = qk[:, 0:GLA_QK].astype(F32) * (GLA_DK ** -0.5)
    k = qk[:, GLA_QK:2 * GLA_QK].astype(F32)
    lr = _mm(a, aw)
    yield
    log_a = _log_sigmoid(lr + ab) * (1.0 / GLA_TAU)
    b = _mm_hi(tri.astype(F32), log_a)
    yield
    b_last = jnp.sum(log_a, axis=0, keepdims=True)
    q_dec = q * jnp.exp(b)
    k_inv = k * jnp.exp(-b)
    k_dec = k * jnp.exp(b_last - b)
    vs = [v[:, h * GLA_DV:(h + 1) * GLA_DV] for h in heads]
    q4 = jnp.concatenate([jnp.where(head == h, q_dec, 0.0) for h in heads], axis=0).astype(BF16)
    att4 = _mm_nt(q4, k_inv)
    kv = _mm_tn(jnp.concatenate(vs, axis=0),
                jnp.concatenate([jnp.where(head == h, k_dec, 0.0) for h in heads], axis=0))
    yield
    s_old = state[0]
    inter4 = _mm_nt(q4, s_old)
    state[0] = s_old * jnp.exp(b_last) + kv
    outs = [_mm(jnp.where(tri, att4[h * CHUNK:(h + 1) * CHUNK], 0.0), vs[h]) for h in heads]
    yield
    return [outs[h] + inter4[h * CHUNK:(h + 1) * CHUNK] for h in heads]


def _gla_kernel(qkf_ref, vf_ref, af_ref, qkb_ref, vb_ref, ab_ref, aw_ref, abias_ref, s0_ref,
                of_ref, ob_ref, sout_ref, s_scr, *, nchunk):
    head = lax.broadcasted_iota(jnp.int32, (1, GLA_QK), 1) // GLA_DK
    tris = (_tri(False), _tri(True))
    dirs = ((qkf_ref, vf_ref, af_ref, of_ref), (qkb_ref, vb_ref, ab_ref, ob_ref))
    bsz, unroll = qkf_ref.shape[0], 2

    def body(ci, carry):
        chains = _chains(ci, nchunk, bsz, unroll)
        states = {(d, b): [s_scr[d, b]] for d in range(2) for b in range(bsz)}
        done = _interleave([_gla_chunk(dirs[d][0][b, rows, :], dirs[d][1][b, rows, :], dirs[d][2][b, rows, :],
                                       aw_ref[d], abias_ref[d], states[d, b], tris[d], head)
                            for d, b, rows in chains])
        for (d, b, rows), outs in zip(chains, done):
            for h, oh in enumerate(outs):
                dirs[d][3][b, rows, h * GLA_DV:(h + 1) * GLA_DV] = oh.astype(BF16)
        for (d, b), st in states.items():
            s_scr[d, b] = st[0]
        return carry

    _scan_states(pl.program_id(0), [s0_ref], [sout_ref], [s_scr], body, nchunk // unroll)


def _gla_mixer(ctx_p, lat_p, a_w, a_b):
    bsz = ctx_p[0].shape[0]
    aw = jnp.zeros((2, SMALL, GLA_QK), F32)
    for d in range(2):
        aw = aw.at[d, d * GLA_RANK:(d + 1) * GLA_RANK].set(a_w[d])
    ab = a_b.reshape(2, 1, GLA_QK)
    st = (jnp.zeros((2, bsz, GLA_DV, GLA_QK), F32),)
    cf, cb, st = _scan_call(_gla_kernel, list(ctx_p), [aw, ab], st, HALF)
    lf, lb, _ = _scan_call(_gla_kernel, list(lat_p), [aw, ab], st, HALF)
    return (cf, cb), (lf, lb)


def _mlstm_chunk(qk, v, gates, state, tri, head, mlane, eye, d):
    q = qk[:, 0:ML_QK].astype(F32)
    k = qk[:, ML_QK:2 * ML_QK].astype(F32) * (ML_DQK ** -0.5)
    trif = tri.astype(F32)
    heads = range(ML_HEADS)
    gates_t = _mm_nt_hi(eye, gates)
    lf = _log_sigmoid(gates)
    b_col = _mm_hi(trif, lf)
    yield
    b_row = _mm_nt_hi(_log_sigmoid(gates_t), trif)
    b_tot = jnp.sum(lf, axis=0, keepdims=True)
    vs = [v[:, h * ML_DV:(h + 1) * ML_DV] for h in heads]
    qms = [jnp.where(head == h, q, 0.0) for h in heads]
    s_raw = [_mm_nt(qm, k) for qm in qms]
    c_old, n_all, m_all = state
    n_old = n_all[0:1, :]
    inter = [_mm_nt(qm, c_old) for qm in qms]
    keep = jnp.zeros((1, ML_QK), F32)
    n_add = jnp.zeros((1, ML_QK), F32)
    m_next = m_all
    kws = []
    for h in heads:
        ic, fc = d * 8 + h, d * 8 + 4 + h
        bl, m_old = b_tot[:, fc:fc + 1], m_all[0:1, h:h + 1]
        a = bl - b_col[:, fc:fc + 1] + gates[:, ic:ic + 1]
        a_max = jnp.max(a, axis=0, keepdims=True)
        m_new = jnp.maximum(bl + m_old, a_max)
        w_old, w_new = jnp.exp(bl + m_old - m_new), jnp.exp(a_max - m_new)
        kw = jnp.where(head == h, k, 0.0) * (jnp.exp(a - a_max) * w_new)
        keep = keep + jnp.where(head == h, w_old, 0.0)
        kws.append(kw)
        n_add = n_add + jnp.sum(kw, axis=0, keepdims=True)
        m_next = jnp.where(mlane == h, m_new, m_next)
    c_add = _mm_tn(jnp.concatenate(vs, axis=0), jnp.concatenate(kws, axis=0))
    state[:] = [c_old * keep + c_add, jnp.broadcast_to(n_old * keep + n_add, n_all.shape), m_next]
    yield
    ms, es, ss = [], [], []
    for h in heads:
        ic, fc = d * 8 + h, d * 8 + 4 + h
        bc = b_col[:, fc:fc + 1]
        dmat = jnp.where(tri, bc - b_row[fc:fc + 1, :] + gates_t[ic:ic + 1, :], -jnp.inf)
        g = bc + m_all[0:1, h:h + 1]
        m = jnp.maximum(g, jnp.max(dmat, axis=1, keepdims=True))
        ms.append(m)
        es.append(jnp.exp(g - m))
        ss.append(s_raw[h] * jnp.exp(dmat - m))
    nums = [_mm(ss[h], vs[h]) for h in heads]
    yield
    outs = []
    for h in heads:
        num = nums[h] + es[h] * inter[h]
        den = jnp.sum(ss[h], axis=1, keepdims=True) + es[h] * jnp.sum(qms[h] * n_old, axis=1, keepdims=True)
        outs.append(num / jnp.maximum(jnp.abs(den), jnp.exp(-ms[h])))
    return outs


def _mlstm_kernel(qkf_ref, vf_ref, gf_ref, qkb_ref, vb_ref, gb_ref, bias_ref, c0_ref, n0_ref, m0_ref,
                  hf_ref, hb_ref, cout_ref, nout_ref, mout_ref, c_scr, n_scr, m_scr, *, nchunk):
    head = lax.broadcasted_iota(jnp.int32, (1, ML_QK), 1) // ML_DQK
    mlane = lax.broadcasted_iota(jnp.int32, (8, LANES), 1)
    eye = _eye(SMALL)
    tris = (_tri(False), _tri(True))
    dirs = ((qkf_ref, vf_ref, gf_ref, hf_ref), (qkb_ref, vb_ref, gb_ref, hb_ref))
    bsz, unroll = qkf_ref.shape[0], 1

    def body(ci, carry):
        chains = _chains(ci, nchunk, bsz, unroll)
        states = {(d, b): [c_scr[d, b], n_scr[d, b], m_scr[d, b]] for d in range(2) for b in range(bsz)}
        done = _interleave([_mlstm_chunk(
            dirs[d][0][b, rows, :], dirs[d][1][b, rows, :], dirs[d][2][b, rows, :] + bias_ref[...],
            states[d, b], tris[d], head, mlane, eye, d) for d, b, rows in chains])
        for (d, b, rows), outs in zip(chains, done):
            for h, oh in enumerate(outs):
                dirs[d][3][b, rows, h * ML_DV:(h + 1) * ML_DV] = oh.astype(BF16)
        for (d, b), st in states.items():
            c_scr[d, b], n_scr[d, b], m_scr[d, b] = st
        return carry

    _scan_states(pl.program_id(0), [c0_ref, n0_ref, m0_ref], [cout_ref, nout_ref, mout_ref],
                 [c_scr, n_scr, m_scr], body, nchunk // unroll)


def _mlstm_mixer(ctx_p, lat_p, gate_b):
    bsz = ctx_p[0].shape[0]
    st = (jnp.zeros((2, bsz, ML_DV, ML_QK), F32), jnp.zeros((2, bsz, 8, ML_QK), F32),
          jnp.zeros((2, bsz, 8, LANES), F32))
    gb = jnp.zeros((1, SMALL), F32).at[0, :16].set(gate_b.reshape(16))
    cf, cb, st = _scan_call(_mlstm_kernel, list(ctx_p), [gb], st, HALF)
    lf, lb, _ = _scan_call(_mlstm_kernel, list(lat_p), [gb], st, HALF)
    return (cf, cb), (lf, lb)


def _ssd_chunk(x, bm, cm, raw, alr, alc, expand, state, tri, sub, eye, d):
    x, bm, cm = x.astype(F32), bm.astype(F32), cm.astype(F32)
    gw = SSD_HPG * SSD_HEADDIM
    trif = tri.astype(F32)
    groups = range(SSD_GROUPS)
    lns = [slice(g * gw, (g + 1) * gw) for g in groups]
    cms = [cm[:, g * SSD_STATE:(g + 1) * SSD_STATE] for g in groups]
    bms = [bm[:, g * SSD_STATE:(g + 1) * SSD_STATE] for g in groups]
    raw_t = _mm_nt_hi(eye, raw)
    dt = _softplus(raw)
    da = dt * (-jnp.exp(alr))
    cum = _mm_hi(trif, da)
    cbs = [_mm_nt(cms[g], bms[g]) for g in groups]
    yield
    dt_t = _softplus(raw_t)
    cum_t = _mm_nt_hi(dt_t * (-jnp.exp(alc)), trif)
    cl = jnp.sum(da, axis=0, keepdims=True)
    e_cum = _mm_hi(jnp.exp(cum), expand)
    wgt = _mm_hi(jnp.exp(cl - cum) * dt, expand)
    dec = _mm_hi(jnp.broadcast_to(jnp.exp(cl), (8, SMALL)), expand)[0:1, :]
    yield
    xw = x * wgt
    kvs = [_mm_tn(bms[g], xw[:, lns[g]]) for g in groups]
    diags = []
    for g in groups:
        xg = x[:, lns[g]]
        mixes = []
        for j in range(SSD_HPG):
            col = 16 + d * 8 + g * SSD_HPG + j
            seg = jnp.exp(jnp.where(tri, cum[:, col:col + 1] - cum_t[col:col + 1, :], -jnp.inf))
            mixes.append(cbs[g] * seg * dt_t[col:col + 1, :])
        diags.append(_mm(jnp.concatenate(mixes, axis=1),
                         jnp.concatenate([jnp.where(sub == j, xg, 0.0) for j in range(SSD_HPG)], axis=0)))
    yield
    offs = [_mm(cms[g], state[g]) for g in groups]
    state[:] = [state[g] * dec[:, lns[g]] + kvs[g] for g in groups]
    yield
    return [offs[g] * e_cum[:, lns[g]] + diags[g] for g in groups]


def _ssd_kernel(xf_ref, bmf_ref, cmf_ref, rf_ref, xb_ref, bmb_ref, cmb_ref, rb_ref,
                dtb_ref, alr_ref, alc_ref, ex_ref, s0_ref, yf_ref, yb_ref, sout_ref, s_scr, *, nchunk):
    gw = SSD_HPG * SSD_HEADDIM
    sub = lax.broadcasted_iota(jnp.int32, (1, gw), 1) // SSD_HEADDIM
    eye = _eye(SMALL)
    tris = (_tri(False), _tri(True))
    dirs = ((xf_ref, bmf_ref, cmf_ref, rf_ref, yf_ref), (xb_ref, bmb_ref, cmb_ref, rb_ref, yb_ref))

    bsz, unroll = xf_ref.shape[0], 2

    def body(ci, carry):
        chains = _chains(ci, nchunk, bsz, unroll)
        states = {(d, b): [s_scr[d, b, :, g * gw:(g + 1) * gw] for g in range(SSD_GROUPS)]
                  for d in range(2) for b in range(bsz)}
        done = _interleave([_ssd_chunk(
            dirs[d][0][b, rows, :], dirs[d][1][b, rows, :], dirs[d][2][b, rows, :],
            dirs[d][3][b, rows, :] + dtb_ref[...], alr_ref[...], alc_ref[...],
            ex_ref[d], states[d, b], tris[d], sub, eye, d) for d, b, rows in chains])
        for (d, b, rows), ys in zip(chains, done):
            for g in range(SSD_GROUPS):
                dirs[d][4][b, rows, g * gw:(g + 1) * gw] = ys[g].astype(BF16)
        for (d, b), st in states.items():
            for g in range(SSD_GROUPS):
                s_scr[d, b, :, g * gw:(g + 1) * gw] = st[g]
        return carry

    _scan_states(pl.program_id(0), [s0_ref], [sout_ref], [s_scr], body, nchunk // unroll)


def _ssd_mixer(ctx_p, lat_p, dt_bias, a_log):
    bsz = ctx_p[0].shape[0]
    st = (jnp.zeros((2, bsz, SSD_STATE, HALF), F32),)
    dtb = jnp.zeros((1, SMALL), F32).at[0, 16:].set(dt_bias.reshape(16))
    al = jnp.zeros((1, SMALL), F32).at[0, 16:].set(a_log.reshape(16).astype(F32))
    lane_head = np.arange(HALF) // SSD_HEADDIM
    ex = jnp.asarray(np.stack([(np.arange(SMALL)[:, None] == (16 + d * 8 + lane_head)[None, :])
                               for d in range(2)]).astype(np.float32))
    consts = [dtb, al, al.reshape(SMALL, 1), ex]
    cf, cb, st = _scan_call(_ssd_kernel, list(ctx_p), consts, st, HALF)
    lf, lb, _ = _scan_call(_ssd_kernel, list(lat_p), consts, st, HALF)
    return (cf, cb), (lf, lb)


def _hy_filter_kernel(f_ref, w1t_ref, w1c_ref, w1s_ref, b1_ref, w2_ref, b2_ref, fq1_ref, fq2_ref, w3s_ref, w3b_ref,
                      dl_ref, o_ref, s_ref, *, n, npiece):
    side, jt = pl.program_id(0), pl.program_id(1)
    tnorm = 1.0 / max(n - 1, 1)
    first = (side == 0) & (jt == 0)
    width = npiece * LANES

    @pl.when(first)
    def _():
        s_ref[...] = jnp.zeros(s_ref.shape, F32)

    def mlp(pos):
        ang = pos * f_ref[...] * (2.0 * math.pi / n)
        pre = (pos * tnorm) * w1t_ref[...] + _mm_hi(w1c_ref[...], jnp.cos(ang)) \
            + _mm_hi(w1s_ref[...], -jnp.sin(ang)) + b1_ref[...]
        hid = jnp.sin(fq1_ref[...] * pre)
        return jnp.sin(fq2_ref[...] * (_mm_hi(w2_ref[...], hid) + b2_ref[...]))

    h00 = _mm(w3b_ref[...], mlp(jnp.zeros((1, LANES), F32)))[:, 0:1]
    tau = jt * width + lax.broadcasted_iota(jnp.int32, (1, width), 1)
    pos = (side * n + (1 - 2 * side) * tau).astype(F32)
    ht = _mm(w3s_ref[...], mlp(pos))
    ht = ht * jnp.exp(-(pos * tnorm) * dl_ref[...])
    ht = jnp.where((tau == 0) & (side == 1), 0.0, ht)
    s_ref[...] += jnp.sum(jnp.abs(ht), axis=1, keepdims=True) + jnp.where(first, jnp.abs(h00), 0.0)
    ht = ht + jnp.where((tau == 0) & (side == 0), h00, 0.0)
    for q in range(npiece):
        o_ref[q] = ht[:, q * LANES:(q + 1) * LANES]


def _hyena_filter(n, w1, b1, w2, b2, w3, freq):
    kk = n // LANES
    npiece = min(8, kk)
    ff = w1.shape[1]
    pad = LANES - ff
    bands = jnp.linspace(1e-4, HY_BANDS - 1, HY_BANDS, dtype=F32).reshape(HY_BANDS, 1)
    deltas = jnp.abs(jnp.linspace(math.log(HY_TARGET) / HY_SLOW_DECAY, math.log(HY_TARGET) / HY_FAST_DECAY,
                                  HY_CH, dtype=F32))
    dl = jnp.tile(deltas, HY_ORDER).reshape(HY_ROWS, 1)
    w3t = jnp.transpose(w3.reshape(ff, HY_ORDER, 2, HY_CH), (2, 1, 3, 0)).reshape(2, HY_ROWS, ff)
    w3t = jnp.pad(w3t, ((0, 0), (0, 0), (0, pad)))
    args = (bands, w1[0].reshape(ff, 1), jnp.transpose(w1[1:1 + HY_BANDS]), jnp.transpose(w1[1 + HY_BANDS:]),
            b1.reshape(ff, 1), jnp.pad(jnp.transpose(w2), ((0, pad), (0, 0))),
            jnp.pad(b2.reshape(ff, 1), ((0, pad), (0, 0))),
            freq[0].reshape(ff, 1), jnp.pad(freq[1].reshape(ff, 1), ((0, pad), (0, 0))))
    nt = kk // npiece
    return pl.pallas_call(
        functools.partial(_hy_filter_kernel, n=n, npiece=npiece),
        grid=(2, nt),
        in_specs=[_resident(a.shape) for a in args]
        + [pl.BlockSpec((None, HY_ROWS, LANES), lambda sd, j: (sd, 0, 0)),
           pl.BlockSpec((None, HY_ROWS, LANES), lambda sd, j: (1, 0, 0)), _resident(dl.shape)],
        out_specs=[pl.BlockSpec((npiece, HY_ROWS, LANES), lambda sd, j: (sd * nt + j, 0, 0)),
                   pl.BlockSpec((HY_ROWS, 1), lambda sd, j: (0, 0))],
        out_shape=[jax.ShapeDtypeStruct((2 * kk, HY_ROWS, LANES), F32),
                   jax.ShapeDtypeStruct((HY_ROWS, 1), F32)],
        name="hy_filter",
        compiler_params=_cparams("arbitrary", "arbitrary"),
    )(*args, w3t, w3t, dl)


def _dft_tables(n):
    big = 2 * n
    n1 = big // LANES
    k = np.arange(n1)

    def cis(num, den):
        ang = 2.0 * np.pi * (num % den) / den
        return np.stack([np.cos(ang), -np.sin(ang)]).astype(np.float32)

    f1 = cis(k[:, None] * k[None, :], n1)
    k2 = np.arange(LANES)
    f2 = cis(k2[:, None] * k2[None, :], LANES)
    tw = cis(k[:, None] * k2[None, :], big)
    w2 = np.concatenate([f2[0], f2[1]], axis=1)
    w2c = np.concatenate([f2[0], -f2[1]], axis=1)
    ginv = np.stack([f1[0, :n1 // 2, :], -f1[1, :n1 // 2, :]]) / big
    return tuple(jnp.asarray(a) for a in (f1, tw, w2, w2c, ginv))


HY_CB = 8
HY_GRP = 4


def _hy_spec_kernel(f_ref, s_ref, f1_ref, tw_ref, w2_ref, o_ref):
    f1r, f1i = f1_ref[0], f1_ref[1]
    twr, twi = tw_ref[0], tw_ref[1]
    n1 = f1r.shape[0]
    for c0 in range(0, HY_CB, HY_GRP):
        trs, tis = [], []
        for ca in range(c0, c0 + HY_GRP, 2):
            rhs = jnp.concatenate([f_ref[:, ca, :], f_ref[:, ca + 1, :]], axis=1)
            pr, pi = _mm(f1r, rhs), _mm(f1i, rhs)
            for e in range(2):
                ar, ai = pr[:, e * LANES:(e + 1) * LANES], pi[:, e * LANES:(e + 1) * LANES]
                trs.append(ar * twr - ai * twi)
                tis.append(ar * twi + ai * twr)
        xr, xi = _cplx_mm(jnp.concatenate(trs, axis=0), jnp.concatenate(tis, axis=0), w2_ref[...])
        for e in range(HY_GRP):
            inv = 1.0 / s_ref[c0 + e]
            o_ref[0, c0 + e] = xr[e * n1:(e + 1) * n1] * inv
            o_ref[1, c0 + e] = xi[e * n1:(e + 1) * n1] * inv


def _hy_spectrum(filt, norm, f1, tw, w2):
    n1, rows, _ = filt.shape
    return pl.pallas_call(
        _hy_spec_kernel,
        grid=(rows // HY_CB,),
        in_specs=[pl.BlockSpec((n1, HY_CB, LANES), lambda j: (0, j, 0)),
                  pl.BlockSpec((HY_CB, 1, 1), lambda j: (j, 0, 0)),
                  _resident(f1.shape), _resident(tw.shape), _resident(w2.shape)],
        out_specs=pl.BlockSpec((2, HY_CB, n1, LANES), lambda j: (0, j, 0, 0)),
        out_shape=jax.ShapeDtypeStruct((2, rows, n1, LANES), F32),
        name="hy_spectrum",
        compiler_params=_cparams("arbitrary"),
    )(filt, norm.reshape(rows, 1, 1), f1, tw, w2)


def _hy_conv_kernel(v_ref, x_ref, hf_ref, bias_ref, f1_ref, g_ref, tw_ref, w2_ref, w2c_ref, o_ref):
    f1r, f1i = f1_ref[0], f1_ref[1]
    gr, gi = g_ref[0], g_ref[1]
    twr, twi = tw_ref[0], tw_ref[1]
    n1 = f1r.shape[0]
    for c0 in range(0, HY_CB, HY_GRP):
        vs = [(v_ref[0, c], v_ref[1, c]) for c in range(c0, c0 + HY_GRP)]
        trs, tis = [], []
        for v0, v1 in vs:
            ar, ai = _cplx_mm_left(f1r, f1i, v0, v1)
            trs.append(ar * twr - ai * twi)
            tis.append(ar * twi + ai * twr)
        xr, xi = _cplx_mm(jnp.concatenate(trs, axis=0), jnp.concatenate(tis, axis=0), w2_ref[...])
        hr = hf_ref[0, c0:c0 + HY_GRP].reshape(HY_GRP * n1, LANES)
        hi = hf_ref[1, c0:c0 + HY_GRP].reshape(HY_GRP * n1, LANES)
        zr, zi = _cplx_mm(xr * hr - xi * hi, xr * hi + xi * hr, w2c_ref[...])
        for e, (v0, v1) in enumerate(vs):
            zre, zie = zr[e * n1:(e + 1) * n1], zi[e * n1:(e + 1) * n1]
            cr, ci = _cplx_mm_left(gr, gi, zre * twr + zie * twi, zie * twr - zre * twi)
            bz = bias_ref[c0 + e]
            o_ref[0, c0 + e] = x_ref[0, c0 + e] * (cr + bz * v0)
            o_ref[1, c0 + e] = x_ref[1, c0 + e] * (ci + bz * v1)


def _hy_conv(v, xg, hf, bias3, order, tabs):
    f1, tw, w2, w2c, ginv = tabs
    bsz, ch, k1, _ = v.shape
    n1 = 2 * k1
    nb = ch // HY_CB
    f1h = f1[:, :, :k1]
    dat = pl.BlockSpec((2, HY_CB, k1, LANES), lambda j: (0, j, 0, 0))
    return pl.pallas_call(
        _hy_conv_kernel,
        grid=(nb,),
        in_specs=[dat, dat,
                  pl.BlockSpec((2, HY_CB, n1, LANES), lambda j: (0, order * nb + j, 0, 0)),
                  pl.BlockSpec((HY_CB, 1, 1), lambda j: (order * nb + j, 0, 0)),
                  _resident(f1h.shape), _resident(ginv.shape), _resident(tw.shape),
                  _resident(w2.shape), _resident(w2c.shape)],
        out_specs=dat,
        out_shape=jax.ShapeDtypeStruct(v.shape, F32),
        name="hy_conv",
        compiler_params=_cparams("arbitrary"),
    )(v, xg, hf, bias3, f1h, ginv, tw, w2, w2c)


def _hy_ctx_kernel(v_ref, x1_ref, x2_ref, filt_ref, s_ref, bias_ref, f_ref, o_ref, *, n):
    k, big = n // LANES, 2 * n
    fr, fi = f_ref[0], f_ref[1]
    c0, c1 = _lanes_of(v_ref, (0,), k), _lanes_of(v_ref, (1,), k)
    for o, x_ref in enumerate((x1_ref, x2_ref)):
        rows = slice(o * HY_CH, (o + 1) * HY_CH)
        filt = jnp.concatenate([filt_ref[q, rows, :] for q in range(2 * k)], axis=1)
        inv = 1.0 / s_ref[rows, :]
        hr, hi = _mm(filt, fr) * inv, _mm(filt, fi) * inv
        xr = _mm(c0, fr[:n]) - _mm(c1, fi[:n])
        xi = _mm(c0, fi[:n]) + _mm(c1, fr[:n])
        yr, yi = xr * hr - xi * hi, xr * hi + xi * hr
        gr, gi = fr[:, :n] * (1.0 / big), fi[:, :n] * (-1.0 / big)
        cr = _mm(yr, gr) - _mm(yi, gi)
        ci = _mm(yr, gi) + _mm(yi, gr)
        b = bias_ref[rows, :]
        c0 = _lanes_of(x_ref, (0,), k) * (cr + b * c0)
        c1 = _lanes_of(x_ref, (1,), k) * (ci + b * c1)
    for q in range(k):
        o_ref[0, :, q, :] = c0[:, q * LANES:(q + 1) * LANES]
        o_ref[1, :, q, :] = c1[:, q * LANES:(q + 1) * LANES]


def _hyena_ctx(v, x1, x2, filt, norm, bias):
    bsz, ch, k, _ = v.shape
    n = k * LANES
    big = 2 * n
    idx = np.arange(big)
    ang = 2.0 * np.pi * ((idx[:, None] * idx[None, :]) % big) / big
    f = jnp.asarray(np.stack([np.cos(ang), -np.sin(ang)]).astype(np.float32))
    args = (v, x1, x2, filt, norm, bias.reshape(HY_ROWS, 1), f)
    return pl.pallas_call(
        functools.partial(_hy_ctx_kernel, n=n),
        grid=(1,),
        in_specs=[_resident(a.shape) for a in args],
        out_specs=pl.BlockSpec(v.shape, lambda j: (0, 0, 0, 0)),
        out_shape=jax.ShapeDtypeStruct(v.shape, F32),
        name="hy_ctx",
        compiler_params=_cparams("arbitrary"),
    )(*args)


def _hyena_lat(v, x1, x2, filt, norm, bias):
    assert v.shape[0] == 2
    n = v.shape[2] * LANES
    tabs = _dft_tables(n)
    hf = _hy_spectrum(filt, norm, tabs[0], tabs[1], tabs[2])
    bias3 = bias.reshape(HY_ROWS, 1, 1)
    z = _hy_conv(v, x1, hf, bias3, 0, tabs)
    return _hy_conv(z, x2, hf, bias3, 1, tabs)


def _even_layer_mix(xc, xl, mod_c, mod_l, w_in, w_out, hy_p, gla_p, ln_g, ln_b, need_ctx):
    conv_w, conv_b, filt_params, bias = hy_p
    a_w, a_b, gla_norm = gla_p
    wp = w_in[:, HY_COLS:].astype(BF16)
    wc = jnp.transpose(w_in[:, :HY_COLS]).astype(BF16)
    cw, cb = jnp.transpose(conv_w), conv_b.reshape(HY_COLS, 1)
    psplit = (2 * GLA_QK, HALF, HALF, SMALL)
    csplit = (HY_CH, HY_CH, HY_CH)
    pc = _inproj(xc, mod_c, wp, wc, cw, cb, psplit, csplit, False, "row", True, 1024)
    pl_ = _inproj(xl, mod_l, wp, wc, cw, cb, psplit, csplit, False, "row", True, 1024)
    g_c, g_l = _gla_mixer((pc[0], pc[1], pc[3]), (pl_[0], pl_[1], pl_[3]), a_w, a_b)
    filt, norm = _hyena_filter(xl.shape[1], *filt_params)
    h_l = _hyena_lat(pl_[4], pl_[5], pl_[6], filt, norm, bias)
    nw = [gla_norm.reshape(1, GLA_DV)]
    specs = [lambda tm: _chan_spec(HY_CH, tm)] + [lambda tm: _seq_spec(tm, HALF)] * 3
    xl = _outproj(_outproj_even_kernel, [h_l, g_l[0], g_l[1], pl_[2]], specs, xl, mod_l, w_out,
                  ln_g, ln_b, nw, "row", 1024)
    if need_ctx:
        filt_c, norm_c = _hyena_filter(xc.shape[1], *filt_params)
        h_c = _hyena_ctx(pc[4], pc[5], pc[6], filt_c, norm_c, bias)
        xc = _outproj(_outproj_even_kernel, [h_c, g_c[0], g_c[1], pc[2]], specs, xc, mod_c, w_out,
                      ln_g, ln_b, nw, "row", 1024)
    return xc, xl


def _odd_layer_mix(xc, xl, mod_c, mod_l, w_in, w_out, ml_p, ssd_p, ln_g, ln_b, need_ctx):
    gate_b, ml_norm = ml_p
    conv_w, conv_b, dt_bias, a_log, d_skip, ssd_norm = ssd_p
    ml_cols = 2 * ML_QK + 2 * HALF + 4 * ML_HEADS
    o0 = 2 * ML_QK + 2 * HALF
    s0 = ml_cols
    wp = jnp.concatenate([w_in[:, :o0], w_in[:, s0:s0 + HALF], w_in[:, o0:ml_cols],
                          w_in[:, s0 + HALF + SSD_CONV_DIM:]], axis=1).astype(BF16)
    wc = w_in[:, s0 + HALF:s0 + HALF + SSD_CONV_DIM].astype(BF16)
    cb = conv_b.reshape(1, SSD_CONV_DIM)
    psplit = (2 * ML_QK, HALF, HALF, HALF, SMALL)
    csplit = (HALF, SSD_GN, SSD_GN)
    pc = _inproj(xc, mod_c, wp, wc, conv_w, cb, psplit, csplit, True, "row", False, 512)
    pl_ = _inproj(xl, mod_l, wp, wc, conv_w, cb, psplit, csplit, True, "col", False, 512)
    m_c, m_l = _mlstm_mixer((pc[0], pc[1], pc[4]), (pl_[0], pl_[1], pl_[4]), gate_b)
    s_c, s_l = _ssd_mixer((pc[5], pc[6], pc[7], pc[4]), (pl_[5], pl_[6], pl_[7], pl_[4]), dt_bias, a_log)
    consts = [ml_norm.reshape(1, ML_DV), jnp.repeat(d_skip, SSD_HEADDIM).reshape(1, HALF),
              ssd_norm.reshape(1, HALF)]
    specs = [lambda tm: _seq_spec(tm, HALF)] * 7
    xl = _outproj(_outproj_odd_kernel, [m_l[0], m_l[1], pl_[2], s_l[0], s_l[1], pl_[5], pl_[3]], specs,
                  xl, mod_l, w_out, ln_g, ln_b, consts, "col", 512)
    if need_ctx:
        xc = _outproj(_outproj_odd_kernel, [m_c[0], m_c[1], pc[2], s_c[0], s_c[1], pc[5], pc[3]], specs,
                      xc, mod_c, w_out, ln_g, ln_b, consts, "row", 512)
    return xc, xl


def kernel(x, c, ctx, c_ctx, w_ada, b_ada, ln_g, ln_b, w_ffn_in, w_ffn_out, w_in_even, w_out_even, hy_conv_w, hy_conv_b, hy_w1, hy_b1, hy_w2, hy_b2, hy_w3, hy_freq, hy_bias, gla_a_w, gla_a_b, gla_norm_w, w_in_odd, w_out_odd, ml_gate_b, ml_norm_w, ssd_conv_w, ssd_conv_b, ssd_dt_bias, ssd_a_log, ssd_d, ssd_norm_w):
    bsz, n, d = x.shape
    depth = w_ada.shape[0]
    assert bsz == 2 and d == D_MODEL
    cond8 = jnp.zeros((8, d), F32).at[:bsz].set(c).at[bsz].set(c_ctx)
    mod_all = _ada_all(cond8, w_ada, b_ada)
    xl, xc = x, ctx
    for l in range(depth):
        last = l == depth - 1
        mod_l = mod_all[l, :bsz].reshape(bsz, N_MOD, d)
        mod_c = jnp.broadcast_to(mod_all[l, bsz].reshape(1, N_MOD, d), (bsz, N_MOD, d))
        w1a, w2a = w_ffn_in[l, 0].astype(BF16), w_ffn_out[l, 0].astype(BF16)
        w1b, w2b = w_ffn_in[l, 1].astype(BF16), w_ffn_out[l, 1].astype(BF16)
        odd = l % 2 == 1
        xl = _ffn_step(xl, mod_l, 0, w1a, w2a, ln_g[l, 0], ln_b[l, 0], out_cv=odd)
        xc = _ffn_step(xc, mod_c, 0, w1a, w2a, ln_g[l, 0], ln_b[l, 0])
        i = l // 2
        if not odd:
            hy_p = (hy_conv_w[i], hy_conv_b[i],
                    (hy_w1[i], hy_b1[i], hy_w2[i], hy_b2[i], hy_w3[i], hy_freq[i]), hy_bias[i])
            gla_p = (gla_a_w[i], gla_a_b[i], gla_norm_w[i])
            xc, xl = _even_layer_mix(xc, xl, mod_c, mod_l, w_in_even[i], w_out_even[i].astype(BF16),
                                     hy_p, gla_p, ln_g[l, 1], ln_b[l, 1], not last)
        else:
            ml_p = (ml_gate_b[i], ml_norm_w[i])
            ssd_p = (ssd_conv_w[i], ssd_conv_b[i], ssd_dt_bias[i], ssd_a_log[i], ssd_d[i], ssd_norm_w[i])
            xc, xl = _odd_layer_mix(xc, xl, mod_c, mod_l, w_in_odd[i], w_out_odd[i].astype(BF16),
                                    ml_p, ssd_p, ln_g[l, 1], ln_b[l, 1], not last)
        xl = _ffn_step(xl, mod_l, 6, w1b, w2b, ln_g[l, 2], ln_b[l, 2], in_cv=odd)
        if not last:
            xc = _ffn_step(xc, mod_c, 6, w1b, w2b, ln_g[l, 2], ln_b[l, 2])
    return xl
```

```python
import functools
import math

import numpy as np
import jax
import jax.numpy as jnp
from jax import lax
from jax.experimental import pallas as pl
from jax.experimental.pallas import tpu as pltpu

F32 = jnp.float32
BF16 = jnp.bfloat16

D_MODEL = 1024
DEPTH = 4
GRID_W = 64
D_FF = 2816
N_MOD = 9
HALF = D_MODEL // 2
CHUNK = 64
LN_EPS = 1e-5
NORM_EPS = 1e-6

HY_CH = HALF
HY_ORDER = 2
HY_EMB = 33
HY_BANDS = (HY_EMB - 1) // 2
HY_FAST_DECAY = 0.3
HY_SLOW_DECAY = 1.5
HY_TARGET = 1e-2
HY_COLS = 3 * HY_CH
HY_ROWS = HY_ORDER * HY_CH

GLA_HEADS = 4
GLA_DK = HALF // 2 // GLA_HEADS
GLA_DV = HALF // GLA_HEADS
GLA_RANK = 16
GLA_TAU = 16.0
GLA_QK = GLA_HEADS * GLA_DK

ML_HEADS = 4
ML_DQK = HALF // 2 // ML_HEADS
ML_DV = HALF // ML_HEADS
ML_QK = ML_HEADS * ML_DQK

SSD_HEADDIM = 64
SSD_HEADS = HALF // SSD_HEADDIM
SSD_GROUPS = 2
SSD_HPG = SSD_HEADS // SSD_GROUPS
SSD_STATE = 128
SSD_GN = SSD_GROUPS * SSD_STATE
SSD_CONV_DIM = HALF + 2 * SSD_GN

DN_ALPHA = (2.0 * DEPTH) ** 0.25

SMALL = 32
LANES = 128
VMEM_LIMIT = 56 * 1024 * 1024


def _cparams(*sem):
    return pltpu.CompilerParams(dimension_semantics=sem, vmem_limit_bytes=VMEM_LIMIT)


def _mm(a, b):
    return jnp.dot(a.astype(BF16), b.astype(BF16), preferred_element_type=F32)


def _mm_nt(a, b):
    return lax.dot_general(a.astype(BF16), b.astype(BF16), (((1,), (1,)), ((), ())),
                           preferred_element_type=F32)


def _mm_tn(a, b):
    return lax.dot_general(a.astype(BF16), b.astype(BF16), (((0,), (0,)), ((), ())),
                           preferred_element_type=F32)


def _mm_hi(a, b):
    return jnp.dot(a, b, precision=lax.Precision.HIGHEST, preferred_element_type=F32)


def _mm_nt_hi(a, b):
    return lax.dot_general(a, b, (((1,), (1,)), ((), ())), precision=lax.Precision.HIGHEST,
                           preferred_element_type=F32)


def _split3(x):
    x1 = x.astype(BF16)
    r1 = x - x1.astype(F32)
    x2 = r1.astype(BF16)
    return x1, x2, (r1 - x2.astype(F32)).astype(BF16)


def _sel_rows(sel3, x):
    return jnp.dot(sel3.astype(BF16), jnp.concatenate(_split3(x), axis=0), preferred_element_type=F32)


def _sel_cols(x, sel):
    sel = sel.astype(BF16)
    return sum(jnp.dot(p, sel, preferred_element_type=F32) for p in _split3(x))


def _sel_nt(a, b, split_a):
    dn = (((1,), (1,)), ((), ()))
    if split_a:
        return sum(lax.dot_general(p, b.astype(BF16), dn, preferred_element_type=F32) for p in _split3(a))
    return sum(lax.dot_general(a.astype(BF16), p, dn, preferred_element_type=F32) for p in _split3(b))


def _tri3(reverse):
    row = lax.broadcasted_iota(jnp.int32, (CHUNK, 3 * CHUNK), 0)
    col = lax.broadcasted_iota(jnp.int32, (CHUNK, 3 * CHUNK), 1) % CHUNK
    return ((row <= col) if reverse else (row >= col)).astype(F32)


def _silu(x):
    return x * jax.nn.sigmoid(x)


def _log_sigmoid(x):
    return jnp.minimum(x, 0.0) - jnp.log1p(jnp.exp(-jnp.abs(x)))


def _softplus(x):
    return jnp.maximum(x, 0.0) + jnp.log1p(jnp.exp(-jnp.abs(x)))


def _layer_norm(y, g, b):
    yc = y - jnp.mean(y, axis=-1, keepdims=True)
    return yc * lax.rsqrt(jnp.mean(yc * yc, axis=-1, keepdims=True) + LN_EPS) * g + b


def _rms(y):
    return y * lax.rsqrt(jnp.mean(y * y, axis=-1, keepdims=True) + NORM_EPS)


def _tri(reverse):
    row = lax.broadcasted_iota(jnp.int32, (CHUNK, CHUNK), 0)
    col = lax.broadcasted_iota(jnp.int32, (CHUNK, CHUNK), 1)
    return (row <= col) if reverse else (row >= col)


def _eye(n):
    return (lax.broadcasted_iota(jnp.int32, (n, n), 0) == lax.broadcasted_iota(jnp.int32, (n, n), 1)).astype(F32)


def _resident(shape):
    nd = len(shape)
    return pl.BlockSpec(shape, lambda *_: (0,) * nd, pipeline_mode=pl.Buffered(1))


def _cplx_mm(ar, ai, w):
    u, v = _mm(ar, w), _mm(ai, w)
    h = w.shape[1] // 2
    return u[:, :h] - v[:, h:], u[:, h:] + v[:, :h]


def _cplx_mm_left(fr, fi, xr, xi):
    rhs = jnp.concatenate([xr, xi], axis=1)
    p, q = _mm(fr, rhs), _mm(fi, rhs)
    h = xr.shape[1]
    return p[:, :h] - q[:, h:], p[:, h:] + q[:, :h]


def _ada_kernel(c_ref, w_ref, b_ref, o_ref):
    o_ref[0] = _mm(_silu(c_ref[...]), w_ref[0]) + b_ref[0]


def _ada_all(cond8, w_ada, b_ada):
    depth, d, nm = w_ada.shape
    tn = nm // 8
    return pl.pallas_call(
        _ada_kernel,
        grid=(depth, nm // tn),
        in_specs=[pl.BlockSpec((8, d), lambda l, j: (0, 0)),
                  pl.BlockSpec((1, d, tn), lambda l, j: (l, 0, j)),
                  pl.BlockSpec((1, 1, tn), lambda l, j: (l, 0, j))],
        out_specs=pl.BlockSpec((1, 8, tn), lambda l, j: (l, 0, j)),
        out_shape=jax.ShapeDtypeStruct((depth, 8, nm), F32),
        name="ada",
        compiler_params=_cparams("arbitrary", "arbitrary"),
    )(cond8, w_ada, b_ada.reshape(depth, 1, nm))


def _stream_dims(x, layout):
    if layout == "col":
        return x.shape[0], x.shape[1] * GRID_W, x.shape[2] // GRID_W
    return x.shape


def _tile_rows(n, layout, tm):
    return n // GRID_W if layout == "col" else min(tm, n)


def _main_spec(n, d, layout, tm):
    if layout == "col":
        return pl.BlockSpec((None, n // GRID_W, d), lambda b, i: (b, 0, i))
    return pl.BlockSpec((None, tm, d), lambda b, i: (b, i, 0))


def _halo_specs(n, d, layout, tm):
    if layout == "col":
        r8 = n // GRID_W // 8
        prev = pl.BlockSpec((None, 8, d), lambda b, i: (b, r8 - 1, jnp.maximum(i - 1, 0)))
        nxt = pl.BlockSpec((None, 8, d), lambda b, i: (b, 0, jnp.minimum(i + 1, GRID_W - 1)))
    else:
        t8, n8 = tm // 8, n // 8
        prev = pl.BlockSpec((None, 8, d), lambda b, i: (b, jnp.maximum(i * t8 - 1, 0), 0))
        nxt = pl.BlockSpec((None, 8, d), lambda b, i: (b, jnp.minimum((i + 1) * t8, n8 - 1), 0))
    return prev, nxt


def _seq_spec(tm, cols):
    return pl.BlockSpec((None, tm, cols), lambda b, i: (b, i, 0))


def _chan_spec(rows, tm):
    return pl.BlockSpec((None, rows, tm // LANES, LANES), lambda b, i: (b, 0, i, 0))


def _lanes_of(ref, lead, pieces):
    return jnp.concatenate([ref[lead + (slice(None), q, slice(None))] for q in range(pieces)], axis=1)


def _ffn_kernel(x_ref, mod_ref, w1_ref, w2_ref, g_ref, b_ref, o_ref, *, mo, nk, in_cv, out_cv):
    d = w2_ref.shape[1]
    pieces = in_cv or out_cv
    if in_cv:
        x = jnp.concatenate([x_ref[:, c * d:(c + 1) * d] for c in range(GRID_W)], axis=0)
    elif pieces:
        x = jnp.concatenate([x_ref[:, c, :] for c in range(GRID_W)], axis=0)
    else:
        x = x_ref[...]
    shift, scale, gate = mod_ref[mo:mo + 1, :], mod_ref[mo + 1:mo + 2, :], mod_ref[mo + 2:mo + 3, :]
    h = (x * (1.0 + scale) + shift).astype(BF16)
    tf = D_FF // nk
    acc = jnp.zeros(x.shape, F32)
    for k in range(nk):
        gt = jnp.dot(h, w1_ref[:, k * tf:(k + 1) * tf], preferred_element_type=F32)
        up = jnp.dot(h, w1_ref[:, D_FF + k * tf:D_FF + (k + 1) * tf], preferred_element_type=F32)
        acc = acc + _mm(_silu(gt) * up, w2_ref[k * tf:(k + 1) * tf, :])
    y = _layer_norm(DN_ALPHA * x + (0.5 * gate) * acc, g_ref[...], b_ref[...])
    if out_cv:
        for c in range(GRID_W):
            o_ref[:, c * d:(c + 1) * d] = y[8 * c:8 * c + 8]
    elif pieces:
        for c in range(GRID_W):
            o_ref[:, c, :] = y[8 * c:8 * c + 8]
    else:
        o_ref[...] = y


def _ffn_step(x, mod, mo, w1, w2, g, b, in_cv=False, out_cv=False, tm=512):
    bsz, n, d = _stream_dims(x, "col" if in_cv else "row")
    rows = n // GRID_W
    if in_cv or out_cv:
        tm = 8 * GRID_W
        cv_spec = pl.BlockSpec((None, 8, GRID_W * d), lambda bb, i: (bb, i, 0))
        tok_spec = pl.BlockSpec((None, 8, GRID_W, d), lambda bb, i: (bb, i, 0, 0))
        x = x if in_cv else x.reshape(bsz, rows, GRID_W, d)
        in_spec, out_spec = (cv_spec if in_cv else tok_spec), (cv_spec if out_cv else tok_spec)
        out_dims = (bsz, rows, GRID_W * d) if out_cv else (bsz, rows, GRID_W, d)
    else:
        tm = min(tm, n)
        in_spec = out_spec = _seq_spec(tm, d)
        out_dims = (bsz, n, d)
    out = pl.pallas_call(
        functools.partial(_ffn_kernel, mo=mo, nk=2, in_cv=in_cv, out_cv=out_cv),
        grid=(bsz, n // tm),
        in_specs=[in_spec,
                  pl.BlockSpec((None, N_MOD, d), lambda bb, i: (bb, 0, 0)),
                  _resident(w1.shape), _resident(w2.shape),
                  _resident((1, d)), _resident((1, d))],
        out_specs=out_spec,
        out_shape=jax.ShapeDtypeStruct(out_dims, F32),
        name="ffn",
        compiler_params=_cparams("arbitrary", "arbitrary"),
    )(x, mod, w1, w2, g.reshape(1, d), b.reshape(1, d))
    return out if out_cv else out.reshape(bsz, n, d)


def _inproj_kernel(x_ref, xp_ref, xn_ref, mod_ref, wp_ref, wc_ref, cw_ref, cb_ref, *out_refs,
                   psplit, csplit, act, chan_major):
    i, last = pl.program_id(1), pl.num_programs(1) - 1
    shift, scale = mod_ref[3:4, :], mod_ref[4:5, :]
    tm = x_ref.shape[0]
    h = x_ref[...] * (1.0 + scale) + shift
    hb = h.astype(BF16)
    o = 0
    for k, w in enumerate(psplit):
        out_refs[k][...] = jnp.dot(hb, wp_ref[:, o:o + w], preferred_element_type=F32).astype(out_refs[k].dtype)
        o += w
    hp = (xp_ref[...] * (1.0 + scale) + shift) * jnp.where(i > 0, 1.0, 0.0)
    hn = (xn_ref[...] * (1.0 + scale) + shift) * jnp.where(i < last, 1.0, 0.0)
    couts = out_refs[len(psplit):]
    if chan_major:
        lane = lax.broadcasted_iota(jnp.int32, (1, tm), 1)
        halo = jnp.concatenate([hp, hn, jnp.zeros((LANES - 16, hp.shape[1]), F32)], axis=0).astype(BF16)
        o = 0
        for k, w in enumerate(csplit):
            wt = wc_ref[o:o + w, :]
            p = _mm_nt(wt, hb)
            p_halo = _mm_nt(wt, halo)
            up = jnp.where(lane == 0, p_halo[:, 7:8], pltpu.roll(p, 1, 1))
            dn = jnp.where(lane == tm - 1, p_halo[:, 8:9], pltpu.roll(p, tm - 1, 1))
            cw = cw_ref[o:o + w, :]
            y = cw[:, 0:1] * up + cw[:, 1:2] * p + cw[:, 2:3] * dn + cb_ref[o:o + w, :]
            for q in range(tm // LANES):
                couts[k][:, q, :] = y[:, q * LANES:(q + 1) * LANES]
            o += w
    else:
        pc = _mm(jnp.concatenate([hp, h, hn], axis=0), wc_ref[...])
        up = pltpu.roll(pc, 1, 0)
        dn = pltpu.roll(pc, tm + 15, 0)
        y = (cw_ref[0:1, :] * up + cw_ref[1:2, :] * pc + cw_ref[2:3, :] * dn + cb_ref[...])[8:8 + tm]
        if act:
            y = _silu(y)
        o = 0
        for k, w in enumerate(csplit):
            couts[k][...] = y[:, o:o + w].astype(couts[k].dtype)
            o += w


def _inproj(xv, mod, wp, wc, cw, cb, psplit, csplit, act, layout, chan_major, tm):
    bsz, n, d = _stream_dims(xv, layout)
    tm = _tile_rows(n, layout, tm)
    prev, nxt = _halo_specs(n, d, layout, tm)
    pdt = [BF16 if w >= LANES else F32 for w in psplit]
    out_specs = [_seq_spec(tm, w) for w in psplit]
    out_shape = [jax.ShapeDtypeStruct((bsz, n, w), dt) for w, dt in zip(psplit, pdt)]
    if chan_major:
        out_specs += [_chan_spec(w, tm) for w in csplit]
        out_shape += [jax.ShapeDtypeStruct((bsz, w, n // LANES, LANES), F32) for w in csplit]
    else:
        out_specs += [_seq_spec(tm, w) for w in csplit]
        out_shape += [jax.ShapeDtypeStruct((bsz, n, w), BF16) for w in csplit]
    return pl.pallas_call(
        functools.partial(_inproj_kernel, psplit=tuple(psplit), csplit=tuple(csplit), act=act,
                          chan_major=chan_major),
        grid=(bsz, n // tm),
        in_specs=[_main_spec(n, d, layout, tm), prev, nxt,
                  pl.BlockSpec((None, N_MOD, d), lambda bb, i: (bb, 0, 0)),
                  _resident(wp.shape), _resident(wc.shape), _resident(cw.shape), _resident(cb.shape)],
        out_specs=out_specs,
        out_shape=out_shape,
        name="inproj_chan" if chan_major else "inproj",
        compiler_params=_cparams("arbitrary", "arbitrary"),
    )(xv, xv, xv, mod, wp, wc, cw, cb)


def _outproj_even_kernel(hy_ref, of_ref, ob_ref, g_ref, x_ref, mod_ref, w_ref, lg_ref, lb_ref, nw_ref, o_ref):
    tm = x_ref.shape[0]
    y = _mm_tn(_lanes_of(hy_ref, (), tm // LANES), w_ref[0:HALF, :])
    o = of_ref[...].astype(F32) + ob_ref[...].astype(F32)
    g = g_ref[...].astype(F32)
    heads = [_rms(o[:, h * GLA_DV:(h + 1) * GLA_DV]) * nw_ref[...] for h in range(GLA_HEADS)]
    y = y + _mm(_silu(g) * jnp.concatenate(heads, axis=1), w_ref[HALF:2 * HALF, :])
    o_ref[...] = _layer_norm(DN_ALPHA * x_ref[...] + mod_ref[5:6, :] * y, lg_ref[...], lb_ref[...])


def _outproj_odd_kernel(hf_ref, hb_ref, og_ref, yf_ref, yb_ref, xs_ref, z_ref, x_ref, mod_ref, w_ref,
                        lg_ref, lb_ref, mnw_ref, dsk_ref, snw_ref, o_ref):
    h = hf_ref[...].astype(F32) + hb_ref[...].astype(F32)
    heads = []
    for k in range(ML_HEADS):
        hk = h[:, k * ML_DV:(k + 1) * ML_DV]
        hc = hk - jnp.mean(hk, axis=-1, keepdims=True)
        heads.append(hc * lax.rsqrt(jnp.mean(hc * hc, axis=-1, keepdims=True) + NORM_EPS) * mnw_ref[...])
    m = jax.nn.sigmoid(og_ref[...].astype(F32)) * jnp.concatenate(heads, axis=1)
    s = yf_ref[...].astype(F32) + yb_ref[...].astype(F32) + dsk_ref[...] * xs_ref[...].astype(F32)
    s = _rms(s * _silu(z_ref[...].astype(F32))) * snw_ref[...]
    y = _mm(m, w_ref[0:HALF, :]) + _mm(s, w_ref[HALF:2 * HALF, :])
    o_ref[...] = _layer_norm(DN_ALPHA * x_ref[...] + mod_ref[5:6, :] * y, lg_ref[...], lb_ref[...])


def _outproj(kern, parts, part_specs, xv, mod, w, g, b, consts, layout, tm):
    bsz, n, d = _stream_dims(xv, layout)
    tm = _tile_rows(n, layout, tm)
    consts = [g.reshape(1, d), b.reshape(1, d)] + consts
    return pl.pallas_call(
        kern,
        grid=(bsz, n // tm),
        in_specs=[s(tm) for s in part_specs] + [_main_spec(n, d, layout, tm),
                  pl.BlockSpec((None, N_MOD, d), lambda bb, i: (bb, 0, 0)), _resident(w.shape)]
        + [_resident(c.shape) for c in consts],
        out_specs=_main_spec(n, d, layout, tm),
        out_shape=jax.ShapeDtypeStruct(xv.shape, F32),
        name=kern.__name__.strip("_"),
        compiler_params=_cparams("arbitrary", "arbitrary"),
    )(*parts, xv, mod, w, *consts)


def _scan_call(kernel, seqs, consts, states, out_cols, tb=512):
    bsz, n, _ = seqs[0].shape
    tb = min(tb, n)
    nblk = n // tb

    def fmap(i):
        return (0, i, 0)

    def bmap(i):
        return (0, nblk - 1 - i, 0)

    in_specs = [pl.BlockSpec((bsz, tb, s.shape[-1]), fmap) for s in seqs]
    in_specs += [pl.BlockSpec((bsz, tb, s.shape[-1]), bmap) for s in seqs]
    in_specs += [_resident(c.shape) for c in consts]
    st_specs = [pl.BlockSpec(s.shape, lambda i, nd=s.ndim: (0,) * nd) for s in states]
    outs = pl.pallas_call(
        functools.partial(kernel, nchunk=tb // CHUNK),
        grid=(nblk,),
        in_specs=in_specs + st_specs,
        out_specs=[pl.BlockSpec((bsz, tb, out_cols), fmap), pl.BlockSpec((bsz, tb, out_cols), bmap)] + st_specs,
        out_shape=[jax.ShapeDtypeStruct((bsz, n, out_cols), BF16)] * 2
        + [jax.ShapeDtypeStruct(s.shape, F32) for s in states],
        scratch_shapes=[pltpu.VMEM(s.shape, F32) for s in states],
        name=kernel.__name__.strip("_"),
        compiler_params=_cparams("arbitrary"),
    )(*seqs, *seqs, *consts, *states)
    return outs[0], outs[1], tuple(outs[2:])


def _chunk_rows(ci, nchunk, reverse):
    c = (nchunk - 1 - ci) if reverse else ci
    return pl.ds(pl.multiple_of(c * CHUNK, CHUNK), CHUNK)


def _chains(ci, nchunk, bsz, unroll):
    return [(d, b, _chunk_rows(ci * unroll + u, nchunk, d == 1))
            for u in range(unroll) for d in range(2) for b in range(bsz)]


def _interleave(gens):
    results = [None] * len(gens)
    live = list(enumerate(gens))
    while live:
        still = []
        for idx, g in live:
            try:
                next(g)
                still.append((idx, g))
            except StopIteration as stop:
                results[idx] = stop.value
        live = still
    return results


def _scan_states(i, s0_refs, sout_refs, scrs, body, trips):
    @pl.when(i == 0)
    def _():
        for s0, scr in zip(s0_refs, scrs):
            scr[...] = s0[...]

    lax.fori_loop(0, trips, body, 0)

    @pl.when(i == pl.num_programs(0) - 1)
    def _():
        for so, scr in zip(sout_refs, scrs):
            so[...] = scr[...]


def _gla_chunk(qk, v, a, aw, ab, state, tri, head):
    heads = range(GLA_HEADS)
    q = qk[:, 0:GLA_QK].astype(F32) * (GLA_DK ** -0.5)
    k = qk[:, GLA_QK:2 * GLA_QK].astype(F32)
    tri, tri3 = tri
    lr = _mm(a, aw)
    yield
    log_a = _log_sigmoid(lr + ab) * (1.0 / GLA_TAU)
    b = _sel_rows(tri3, log_a)
    yield
    b_last = jnp.sum(log_a, axis=0, keepdims=True)
    q_dec = q * jnp.exp(b)
    k_inv = k * jnp.exp(-b)
    k_dec = k * jnp.exp(b_last - b)
    vs = [v[:, h * GLA_DV:(h + 1) * GLA_DV] for h in heads]
    q4 = jnp.concatenate([jnp.where(head == h, q_dec, 0.0) for h in heads], axis=0).astype(BF16)
    att4 = _mm_nt(q4, k_inv)
    kv = _mm_tn(jnp.concatenate(vs, axis=0),
                jnp.concatenate([jnp.where(head == h, k_dec, 0.0) for h in heads], axis=0))
    yield
    s_old = state[0]
    inter4 = _mm_nt(q4, s_old)
    state[0] = s_old * jnp.exp(b_last) + kv
    outs = [_mm(jnp.where(tri, att4[h * CHUNK:(h + 1) * CHUNK], 0.0), vs[h]) for h in heads]
    yield
    return [outs[h] + inter4[h * CHUNK:(h + 1) * CHUNK] for h in heads]


def _gla_kernel(qkf_ref, vf_ref, af_ref, qkb_ref, vb_ref, ab_ref, aw_ref, abias_ref, s0_ref,
                of_ref, ob_ref, sout_ref, s_scr, *, nchunk):
    head = lax.broadcasted_iota(jnp.int32, (1, GLA_QK), 1) // GLA_DK
    tris = tuple((_tri(rev), _tri3(rev)) for rev in (False, True))
    dirs = ((qkf_ref, vf_ref, af_ref, of_ref), (qkb_ref, vb_ref, ab_ref, ob_ref))
    bsz, unroll = qkf_ref.shape[0], 2

    def body(ci, carry):
        chains = _chains(ci, nchunk, bsz, unroll)
        states = {(d, b): [s_scr[d, b]] for d in range(2) for b in range(bsz)}
        done = _interleave([_gla_chunk(dirs[d][0][b, rows, :], dirs[d][1][b, rows, :], dirs[d][2][b, rows, :],
                                       aw_ref[d], abias_ref[d], states[d, b], tris[d], head)
                            for d, b, rows in chains])
        for (d, b, rows), outs in zip(chains, done):
            for h, oh in enumerate(outs):
                dirs[d][3][b, rows, h * GLA_DV:(h + 1) * GLA_DV] = oh.astype(BF16)
        for (d, b), st in states.items():
            s_scr[d, b] = st[0]
        return carry

    _scan_states(pl.program_id(0), [s0_ref], [sout_ref], [s_scr], body, nchunk // unroll)


def _gla_mixer(ctx_p, lat_p, a_w, a_b):
    bsz = ctx_p[0].shape[0]
    aw = jnp.zeros((2, SMALL, GLA_QK), F32)
    for d in range(2):
        aw = aw.at[d, d * GLA_RANK:(d + 1) * GLA_RANK].set(a_w[d])
    ab = a_b.reshape(2, 1, GLA_QK)
    st = (jnp.zeros((2, bsz, GLA_DV, GLA_QK), F32),)
    cf, cb, st = _scan_call(_gla_kernel, list(ctx_p), [aw, ab], st, HALF)
    lf, lb, _ = _scan_call(_gla_kernel, list(lat_p), [aw, ab], st, HALF)
    return (cf, cb), (lf, lb)


def _mlstm_chunk(qk, v, gates, state, tri, head, eye, shift, spread, d):
    q = qk[:, 0:ML_QK].astype(F32)
    k = qk[:, ML_QK:2 * ML_QK].astype(F32) * (ML_DQK ** -0.5)
    tri, tri3 = tri
    heads = range(ML_HEADS)
    trif = tri.astype(F32)
    gates_t = _mm_nt_hi(eye, gates)
    lf = _log_sigmoid(gates)
    b_col = _mm_hi(trif, lf)
    yield
    b_row = _mm_nt_hi(_log_sigmoid(gates_t), trif)
    b_tot = jnp.sum(lf, axis=0, keepdims=True)
    vs = [v[:, h * ML_DV:(h + 1) * ML_DV] for h in heads]
    qms = [jnp.where(head == h, q, 0.0) for h in heads]
    s_raw = [_mm_nt(qm, k) for qm in qms]
    c_old, n_all, m_all = state
    n_old = n_all[0:1, :]
    inter = [_mm_nt(qm, c_old) for qm in qms]
    m_old = m_all[0:1, :]
    a = b_tot - b_col + _sel_cols(gates, shift)
    a_max = jnp.max(a, axis=0, keepdims=True)
    m_new = jnp.maximum(b_tot + m_old, a_max)
    w_old, w_new = jnp.exp(b_tot + m_old - m_new), jnp.exp(a_max - m_new)
    kw_all = k * _mm(jnp.exp(a - a_max) * w_new, spread)
    keep = _sel_cols(jnp.broadcast_to(w_old, (8, SMALL)), spread)[0:1, :]
    n_add = jnp.sum(kw_all, axis=0, keepdims=True)
    c_add = _mm_tn(jnp.concatenate(vs, axis=0),
                   jnp.concatenate([jnp.where(head == h, kw_all, 0.0) for h in heads], axis=0))
    state[:] = [c_old * keep + c_add, jnp.broadcast_to(n_old * keep + n_add, n_all.shape),
                jnp.broadcast_to(m_new, m_all.shape)]
    yield
    ms, es, ss = [], [], []
    for h in heads:
        ic, fc = d * 8 + h, d * 8 + 4 + h
        bc = b_col[:, fc:fc + 1]
        r_vis = jnp.where(tri, gates_t[ic:ic + 1, :] - b_row[fc:fc + 1, :], -jnp.inf)
        g = bc + m_all[0:1, fc:fc + 1]
        m = jnp.maximum(g, bc + jnp.max(r_vis, axis=1, keepdims=True))
        ms.append(m)
        es.append(jnp.exp(g - m))
        ss.append(s_raw[h] * jnp.exp(r_vis + (bc - m)))
    nums = [_mm(ss[h], vs[h]) for h in heads]
    yield
    outs = []
    for h in heads:
        num = nums[h] + es[h] * inter[h]
        den = jnp.sum(ss[h], axis=1, keepdims=True) + es[h] * jnp.sum(qms[h] * n_old, axis=1, keepdims=True)
        outs.append(num / jnp.maximum(jnp.abs(den), jnp.exp(-ms[h])))
    return outs


def _mlstm_kernel(qkf_ref, vf_ref, gf_ref, qkb_ref, vb_ref, gb_ref, bias_ref, c0_ref, n0_ref, m0_ref,
                  hf_ref, hb_ref, cout_ref, nout_ref, mout_ref, c_scr, n_scr, m_scr, *, nchunk):
    head = lax.broadcasted_iota(jnp.int32, (1, ML_QK), 1) // ML_DQK
    col_r = lax.broadcasted_iota(jnp.int32, (SMALL, SMALL), 0)
    col_c = lax.broadcasted_iota(jnp.int32, (SMALL, SMALL), 1)
    shift = ((col_c == col_r + ML_HEADS) & (col_r % 8 < ML_HEADS)).astype(F32)
    row = lax.broadcasted_iota(jnp.int32, (SMALL, ML_QK), 0)
    lane_head = lax.broadcasted_iota(jnp.int32, (SMALL, ML_QK), 1) // ML_DQK
    spreads = [(row == d * 8 + ML_HEADS + lane_head).astype(F32) for d in range(2)]
    eye = _eye(SMALL)
    tris = tuple((_tri(rev), _tri3(rev)) for rev in (False, True))
    dirs = ((qkf_ref, vf_ref, gf_ref, hf_ref), (qkb_ref, vb_ref, gb_ref, hb_ref))
    bsz, unroll = qkf_ref.shape[0], 1

    def body(ci, carry):
        chains = _chains(ci, nchunk, bsz, unroll)
        states = {(d, b): [c_scr[d, b], n_scr[d, b], m_scr[d, b]] for d in range(2) for b in range(bsz)}
        done = _interleave([_mlstm_chunk(
            dirs[d][0][b, rows, :], dirs[d][1][b, rows, :], dirs[d][2][b, rows, :] + bias_ref[...],
            states[d, b], tris[d], head, eye, shift, spreads[d], d) for d, b, rows in chains])
        for (d, b, rows), outs in zip(chains, done):
            for h, oh in enumerate(outs):
                dirs[d][3][b, rows, h * ML_DV:(h + 1) * ML_DV] = oh.astype(BF16)
        for (d, b), st in states.items():
            c_scr[d, b], n_scr[d, b], m_scr[d, b] = st
        return carry

    _scan_states(pl.program_id(0), [c0_ref, n0_ref, m0_ref], [cout_ref, nout_ref, mout_ref],
                 [c_scr, n_scr, m_scr], body, nchunk // unroll)


def _mlstm_mixer(ctx_p, lat_p, gate_b):
    bsz = ctx_p[0].shape[0]
    st = (jnp.zeros((2, bsz, ML_DV, ML_QK), F32), jnp.zeros((2, bsz, 8, ML_QK), F32),
          jnp.zeros((2, bsz, 8, SMALL), F32))
    gb = jnp.zeros((1, SMALL), F32).at[0, :16].set(gate_b.reshape(16))
    cf, cb, st = _scan_call(_mlstm_kernel, list(ctx_p), [gb], st, HALF)
    lf, lb, _ = _scan_call(_mlstm_kernel, list(lat_p), [gb], st, HALF)
    return (cf, cb), (lf, lb)


def _ssd_chunk(x, bm, cm, raw, alr, alc, expand, state, tri, sub, eye, d):
    x, bm, cm = x.astype(F32), bm.astype(F32), cm.astype(F32)
    gw = SSD_HPG * SSD_HEADDIM
    tri, tri3 = tri
    groups = range(SSD_GROUPS)
    lns = [slice(g * gw, (g + 1) * gw) for g in groups]
    cms = [cm[:, g * SSD_STATE:(g + 1) * SSD_STATE] for g in groups]
    bms = [bm[:, g * SSD_STATE:(g + 1) * SSD_STATE] for g in groups]
    raw_t = _sel_nt(eye, raw, False)
    dt = _softplus(raw)
    da = dt * (-jnp.exp(alr))
    cum = _sel_rows(tri3, da)
    cbs = [_mm_nt(cms[g], bms[g]) for g in groups]
    yield
    dt_t = _softplus(raw_t)
    cum_t = _sel_nt(dt_t * (-jnp.exp(alc)), tri, True)
    cl = jnp.sum(da, axis=0, keepdims=True)
    e_cum = _sel_cols(jnp.exp(cum), expand)
    wgt = _sel_cols(jnp.exp(cl - cum) * dt, expand)
    dec = _sel_cols(jnp.broadcast_to(jnp.exp(cl), (8, SMALL)), expand)[0:1, :]
    yield
    xw = x * wgt
    kvs = [_mm_tn(bms[g], xw[:, lns[g]]) for g in groups]
    diags = []
    for g in groups:
        xg = x[:, lns[g]]
        mixes = []
        for j in range(SSD_HPG):
            col = 16 + d * 8 + g * SSD_HPG + j
            seg = jnp.exp(jnp.where(tri, cum[:, col:col + 1] - cum_t[col:col + 1, :], -jnp.inf))
            mixes.append(cbs[g] * seg * dt_t[col:col + 1, :])
        diags.append(_mm(jnp.concatenate(mixes, axis=1),
                         jnp.concatenate([jnp.where(sub == j, xg, 0.0) for j in range(SSD_HPG)], axis=0)))
    yield
    offs = [_mm(cms[g], state[g]) for g in groups]
    state[:] = [state[g] * dec[:, lns[g]] + kvs[g] for g in groups]
    yield
    return [offs[g] * e_cum[:, lns[g]] + diags[g] for g in groups]


def _ssd_kernel(xf_ref, bmf_ref, cmf_ref, rf_ref, xb_ref, bmb_ref, cmb_ref, rb_ref,
                dtb_ref, alr_ref, alc_ref, ex_ref, s0_ref, yf_ref, yb_ref, sout_ref, s_scr, *, nchunk):
    gw = SSD_HPG * SSD_HEADDIM
    sub = lax.broadcasted_iota(jnp.int32, (1, gw), 1) // SSD_HEADDIM
    eye = _eye(SMALL)
    tris = tuple((_tri(rev), _tri3(rev)) for rev in (False, True))
    dirs = ((xf_ref, bmf_ref, cmf_ref, rf_ref, yf_ref), (xb_ref, bmb_ref, cmb_ref, rb_ref, yb_ref))

    bsz, unroll = xf_ref.shape[0], 2

    def body(ci, carry):
        chains = _chains(ci, nchunk, bsz, unroll)
        states = {(d, b): [s_scr[d, b, :, g * gw:(g + 1) * gw] for g in range(SSD_GROUPS)]
                  for d in range(2) for b in range(bsz)}
        done = _interleave([_ssd_chunk(
            dirs[d][0][b, rows, :], dirs[d][1][b, rows, :], dirs[d][2][b, rows, :],
            dirs[d][3][b, rows, :] + dtb_ref[...], alr_ref[...], alc_ref[...],
            ex_ref[d], states[d, b], tris[d], sub, eye, d) for d, b, rows in chains])
        for (d, b, rows), ys in zip(chains, done):
            for g in range(SSD_GROUPS):
                dirs[d][4][b, rows, g * gw:(g + 1) * gw] = ys[g].astype(BF16)
        for (d, b), st in states.items():
            for g in range(SSD_GROUPS):
                s_scr[d, b, :, g * gw:(g + 1) * gw] = st[g]
        return carry

    _scan_states(pl.program_id(0), [s0_ref], [sout_ref], [s_scr], body, nchunk // unroll)


def _ssd_mixer(ctx_p, lat_p, dt_bias, a_log):
    bsz = ctx_p[0].shape[0]
    st = (jnp.zeros((2, bsz, SSD_STATE, HALF), F32),)
    dtb = jnp.zeros((1, SMALL), F32).at[0, 16:].set(dt_bias.reshape(16))
    al = jnp.zeros((1, SMALL), F32).at[0, 16:].set(a_log.reshape(16).astype(F32))
    lane_head = np.arange(HALF) // SSD_HEADDIM
    ex = jnp.asarray(np.stack([(np.arange(SMALL)[:, None] == (16 + d * 8 + lane_head)[None, :])
                               for d in range(2)]).astype(np.float32))
    consts = [dtb, al, al.reshape(SMALL, 1), ex]
    cf, cb, st = _scan_call(_ssd_kernel, list(ctx_p), consts, st, HALF)
    lf, lb, _ = _scan_call(_ssd_kernel, list(lat_p), consts, st, HALF)
    return (cf, cb), (lf, lb)


def _hy_filter_kernel(f_ref, w1t_ref, w1c_ref, w1s_ref, b1_ref, w2_ref, b2_ref, fq1_ref, fq2_ref, w3s_ref, w3b_ref,
                      dl_ref, o_ref, s_ref, *, n, npiece):
    side, jt = pl.program_id(0), pl.program_id(1)
    tnorm = 1.0 / max(n - 1, 1)
    first = (side == 0) & (jt == 0)
    width = npiece * LANES

    @pl.when(first)
    def _():
        s_ref[...] = jnp.zeros(s_ref.shape, F32)

    def mlp(pos):
        ang = pos * f_ref[...] * (2.0 * math.pi / n)
        pre = (pos * tnorm) * w1t_ref[...] + _mm_hi(w1c_ref[...], jnp.cos(ang)) \
            + _mm_hi(w1s_ref[...], -jnp.sin(ang)) + b1_ref[...]
        hid = jnp.sin(fq1_ref[...] * pre)
        return jnp.sin(fq2_ref[...] * (_mm_hi(w2_ref[...], hid) + b2_ref[...]))

    h00 = _mm(w3b_ref[...], mlp(jnp.zeros((1, LANES), F32)))[:, 0:1]
    tau = jt * width + lax.broadcasted_iota(jnp.int32, (1, width), 1)
    pos = (side * n + (1 - 2 * side) * tau).astype(F32)
    ht = _mm(w3s_ref[...], mlp(pos))
    ht = ht * jnp.exp(-(pos * tnorm) * dl_ref[...])
    ht = jnp.where((tau == 0) & (side == 1), 0.0, ht)
    s_ref[...] += jnp.sum(jnp.abs(ht), axis=1, keepdims=True) + jnp.where(first, jnp.abs(h00), 0.0)
    ht = ht + jnp.where((tau == 0) & (side == 0), h00, 0.0)
    for q in range(npiece):
        o_ref[q] = ht[:, q * LANES:(q + 1) * LANES]


def _hyena_filter(n, w1, b1, w2, b2, w3, freq):
    kk = n // LANES
    npiece = min(8, kk)
    ff = w1.shape[1]
    pad = LANES - ff
    bands = jnp.linspace(1e-4, HY_BANDS - 1, HY_BANDS, dtype=F32).reshape(HY_BANDS, 1)
    deltas = jnp.abs(jnp.linspace(math.log(HY_TARGET) / HY_SLOW_DECAY, math.log(HY_TARGET) / HY_FAST_DECAY,
                                  HY_CH, dtype=F32))
    dl = jnp.tile(deltas, HY_ORDER).reshape(HY_ROWS, 1)
    w3t = jnp.transpose(w3.reshape(ff, HY_ORDER, 2, HY_CH), (2, 1, 3, 0)).reshape(2, HY_ROWS, ff)
    w3t = jnp.pad(w3t, ((0, 0), (0, 0), (0, pad)))
    args = (bands, w1[0].reshape(ff, 1), jnp.transpose(w1[1:1 + HY_BANDS]), jnp.transpose(w1[1 + HY_BANDS:]),
            b1.reshape(ff, 1), jnp.pad(jnp.transpose(w2), ((0, pad), (0, 0))),
            jnp.pad(b2.reshape(ff, 1), ((0, pad), (0, 0))),
            freq[0].reshape(ff, 1), jnp.pad(freq[1].reshape(ff, 1), ((0, pad), (0, 0))))
    nt = kk // npiece
    return pl.pallas_call(
        functools.partial(_hy_filter_kernel, n=n, npiece=npiece),
        grid=(2, nt),
        in_specs=[_resident(a.shape) for a in args]
        + [pl.BlockSpec((None, HY_ROWS, LANES), lambda sd, j: (sd, 0, 0)),
           pl.BlockSpec((None, HY_ROWS, LANES), lambda sd, j: (1, 0, 0)), _resident(dl.shape)],
        out_specs=[pl.BlockSpec((npiece, HY_ROWS, LANES), lambda sd, j: (sd * nt + j, 0, 0)),
                   pl.BlockSpec((HY_ROWS, 1), lambda sd, j: (0, 0))],
        out_shape=[jax.ShapeDtypeStruct((2 * kk, HY_ROWS, LANES), F32),
                   jax.ShapeDtypeStruct((HY_ROWS, 1), F32)],
        name="hy_filter",
        compiler_params=_cparams("arbitrary", "arbitrary"),
    )(*args, w3t, w3t, dl)


def _dft_tables(n):
    big = 2 * n
    n1 = big // LANES
    k = np.arange(n1)

    def cis(num, den):
        ang = 2.0 * np.pi * (num % den) / den
        return np.stack([np.cos(ang), -np.sin(ang)]).astype(np.float32)

    f1 = cis(k[:, None] * k[None, :], n1)
    k2 = np.arange(LANES)
    f2 = cis(k2[:, None] * k2[None, :], LANES)
    tw = cis(k[:, None] * k2[None, :], big)
    w2 = np.concatenate([f2[0], f2[1]], axis=1)
    w2c = np.concatenate([f2[0], -f2[1]], axis=1)
    ginv = np.stack([f1[0, :n1 // 2, :], -f1[1, :n1 // 2, :]]) / big
    return tuple(jnp.asarray(a) for a in (f1, tw, w2, w2c, ginv))


HY_CB = 8
HY_GRP = 4


def _hy_spec_kernel(f_ref, s_ref, f1_ref, tw_ref, w2_ref, o_ref):
    f1r, f1i = f1_ref[0], f1_ref[1]
    twr, twi = tw_ref[0], tw_ref[1]
    n1 = f1r.shape[0]
    for c0 in range(0, HY_CB, HY_GRP):
        trs, tis = [], []
        for ca in range(c0, c0 + HY_GRP, 2):
            rhs = jnp.concatenate([f_ref[:, ca, :], f_ref[:, ca + 1, :]], axis=1)
            pr, pi = _mm(f1r, rhs), _mm(f1i, rhs)
            for e in range(2):
                ar, ai = pr[:, e * LANES:(e + 1) * LANES], pi[:, e * LANES:(e + 1) * LANES]
                trs.append(ar * twr - ai * twi)
                tis.append(ar * twi + ai * twr)
        xr, xi = _cplx_mm(jnp.concatenate(trs, axis=0), jnp.concatenate(tis, axis=0), w2_ref[...])
        for e in range(HY_GRP):
            inv = 1.0 / s_ref[c0 + e]
            o_ref[0, c0 + e] = xr[e * n1:(e + 1) * n1] * inv
            o_ref[1, c0 + e] = xi[e * n1:(e + 1) * n1] * inv


def _hy_spectrum(filt, norm, f1, tw, w2):
    n1, rows, _ = filt.shape
    return pl.pallas_call(
        _hy_spec_kernel,
        grid=(rows // HY_CB,),
        in_specs=[pl.BlockSpec((n1, HY_CB, LANES), lambda j: (0, j, 0)),
                  pl.BlockSpec((HY_CB, 1, 1), lambda j: (j, 0, 0)),
                  _resident(f1.shape), _resident(tw.shape), _resident(w2.shape)],
        out_specs=pl.BlockSpec((2, HY_CB, n1, LANES), lambda j: (0, j, 0, 0)),
        out_shape=jax.ShapeDtypeStruct((2, rows, n1, LANES), F32),
        name="hy_spectrum",
        compiler_params=_cparams("arbitrary"),
    )(filt, norm.reshape(rows, 1, 1), f1, tw, w2)


def _hy_conv_kernel(v_ref, x_ref, hf_ref, bias_ref, f1_ref, g_ref, tw_ref, w2_ref, w2c_ref, o_ref):
    f1r, f1i = f1_ref[0], f1_ref[1]
    gr, gi = g_ref[0], g_ref[1]
    twr, twi = tw_ref[0], tw_ref[1]
    n1 = f1r.shape[0]
    for c0 in range(0, HY_CB, HY_GRP):
        vs = [(v_ref[0, c], v_ref[1, c]) for c in range(c0, c0 + HY_GRP)]
        trs, tis = [], []
        for v0, v1 in vs:
            ar, ai = _cplx_mm_left(f1r, f1i, v0, v1)
            trs.append(ar * twr - ai * twi)
            tis.append(ar * twi + ai * twr)
        xr, xi = _cplx_mm(jnp.concatenate(trs, axis=0), jnp.concatenate(tis, axis=0), w2_ref[...])
        hr = hf_ref[0, c0:c0 + HY_GRP].reshape(HY_GRP * n1, LANES)
        hi = hf_ref[1, c0:c0 + HY_GRP].reshape(HY_GRP * n1, LANES)
        zr, zi = _cplx_mm(xr * hr - xi * hi, xr * hi + xi * hr, w2c_ref[...])
        for e, (v0, v1) in enumerate(vs):
            zre, zie = zr[e * n1:(e + 1) * n1], zi[e * n1:(e + 1) * n1]
            cr, ci = _cplx_mm_left(gr, gi, zre * twr + zie * twi, zie * twr - zre * twi)
            bz = bias_ref[c0 + e]
            o_ref[0, c0 + e] = x_ref[0, c0 + e] * (cr + bz * v0)
            o_ref[1, c0 + e] = x_ref[1, c0 + e] * (ci + bz * v1)


def _hy_conv(v, xg, hf, bias3, order, tabs):
    f1, tw, w2, w2c, ginv = tabs
    bsz, ch, k1, _ = v.shape
    n1 = 2 * k1
    nb = ch // HY_CB
    f1h = f1[:, :, :k1]
    dat = pl.BlockSpec((2, HY_CB, k1, LANES), lambda j: (0, j, 0, 0))
    return pl.pallas_call(
        _hy_conv_kernel,
        grid=(nb,),
        in_specs=[dat, dat,
                  pl.BlockSpec((2, HY_CB, n1, LANES), lambda j: (0, order * nb + j, 0, 0)),
                  pl.BlockSpec((HY_CB, 1, 1), lambda j: (order * nb + j, 0, 0)),
                  _resident(f1h.shape), _resident(ginv.shape), _resident(tw.shape),
                  _resident(w2.shape), _resident(w2c.shape)],
        out_specs=dat,
        out_shape=jax.ShapeDtypeStruct(v.shape, F32),
        name="hy_conv",
        compiler_params=_cparams("arbitrary"),
    )(v, xg, hf, bias3, f1h, ginv, tw, w2, w2c)


def _hy_ctx_kernel(v_ref, x1_ref, x2_ref, filt_ref, s_ref, bias_ref, f_ref, o_ref, *, n):
    k, big = n // LANES, 2 * n
    fr, fi = f_ref[0], f_ref[1]
    c0, c1 = _lanes_of(v_ref, (0,), k), _lanes_of(v_ref, (1,), k)
    for o, x_ref in enumerate((x1_ref, x2_ref)):
        rows = slice(o * HY_CH, (o + 1) * HY_CH)
        filt = jnp.concatenate([filt_ref[q, rows, :] for q in range(2 * k)], axis=1)
        inv = 1.0 / s_ref[rows, :]
        hr, hi = _mm(filt, fr) * inv, _mm(filt, fi) * inv
        xr = _mm(c0, fr[:n]) - _mm(c1, fi[:n])
        xi = _mm(c0, fi[:n]) + _mm(c1, fr[:n])
        yr, yi = xr * hr - xi * hi, xr * hi + xi * hr
        gr, gi = fr[:, :n] * (1.0 / big), fi[:, :n] * (-1.0 / big)
        cr = _mm(yr, gr) - _mm(yi, gi)
        ci = _mm(yr, gi) + _mm(yi, gr)
        b = bias_ref[rows, :]
        c0 = _lanes_of(x_ref, (0,), k) * (cr + b * c0)
        c1 = _lanes_of(x_ref, (1,), k) * (ci + b * c1)
    for q in range(k):
        o_ref[0, :, q, :] = c0[:, q * LANES:(q + 1) * LANES]
        o_ref[1, :, q, :] = c1[:, q * LANES:(q + 1) * LANES]


def _hyena_ctx(v, x1, x2, filt, norm, bias):
    bsz, ch, k, _ = v.shape
    n = k * LANES
    big = 2 * n
    idx = np.arange(big)
    ang = 2.0 * np.pi * ((idx[:, None] * idx[None, :]) % big) / big
    f = jnp.asarray(np.stack([np.cos(ang), -np.sin(ang)]).astype(np.float32))
    args = (v, x1, x2, filt, norm, bias.reshape(HY_ROWS, 1), f)
    return pl.pallas_call(
        functools.partial(_hy_ctx_kernel, n=n),
        grid=(1,),
        in_specs=[_resident(a.shape) for a in args],
        out_specs=pl.BlockSpec(v.shape, lambda j: (0, 0, 0, 0)),
        out_shape=jax.ShapeDtypeStruct(v.shape, F32),
        name="hy_ctx",
        compiler_params=_cparams("arbitrary"),
    )(*args)


def _hyena_lat(v, x1, x2, filt, norm, bias):
    assert v.shape[0] == 2
    n = v.shape[2] * LANES
    tabs = _dft_tables(n)
    hf = _hy_spectrum(filt, norm, tabs[0], tabs[1], tabs[2])
    bias3 = bias.reshape(HY_ROWS, 1, 1)
    z = _hy_conv(v, x1, hf, bias3, 0, tabs)
    return _hy_conv(z, x2, hf, bias3, 1, tabs)


def _even_layer_mix(xc, xl, mod_c, mod_l, w_in, w_out, hy_p, gla_p, ln_g, ln_b, need_ctx):
    conv_w, conv_b, filt_params, bias = hy_p
    a_w, a_b, gla_norm = gla_p
    wp = w_in[:, HY_COLS:].astype(BF16)
    wc = jnp.transpose(w_in[:, :HY_COLS]).astype(BF16)
    cw, cb = jnp.transpose(conv_w), conv_b.reshape(HY_COLS, 1)
    psplit = (2 * GLA_QK, HALF, HALF, SMALL)
    csplit = (HY_CH, HY_CH, HY_CH)
    pc = _inproj(xc, mod_c, wp, wc, cw, cb, psplit, csplit, False, "row", True, 1024)
    pl_ = _inproj(xl, mod_l, wp, wc, cw, cb, psplit, csplit, False, "row", True, 1024)
    g_c, g_l = _gla_mixer((pc[0], pc[1], pc[3]), (pl_[0], pl_[1], pl_[3]), a_w, a_b)
    filt, norm = _hyena_filter(xl.shape[1], *filt_params)
    h_l = _hyena_lat(pl_[4], pl_[5], pl_[6], filt, norm, bias)
    nw = [gla_norm.reshape(1, GLA_DV)]
    specs = [lambda tm: _chan_spec(HY_CH, tm)] + [lambda tm: _seq_spec(tm, HALF)] * 3
    xl = _outproj(_outproj_even_kernel, [h_l, g_l[0], g_l[1], pl_[2]], specs, xl, mod_l, w_out,
                  ln_g, ln_b, nw, "row", 1024)
    if need_ctx:
        filt_c, norm_c = _hyena_filter(xc.shape[1], *filt_params)
        h_c = _hyena_ctx(pc[4], pc[5], pc[6], filt_c, norm_c, bias)
        xc = _outproj(_outproj_even_kernel, [h_c, g_c[0], g_c[1], pc[2]], specs, xc, mod_c, w_out,
                      ln_g, ln_b, nw, "row", 1024)
    return xc, xl


def _odd_layer_mix(xc, xl, mod_c, mod_l, w_in, w_out, ml_p, ssd_p, ln_g, ln_b, need_ctx):
    gate_b, ml_norm = ml_p
    conv_w, conv_b, dt_bias, a_log, d_skip, ssd_norm = ssd_p
    ml_cols = 2 * ML_QK + 2 * HALF + 4 * ML_HEADS
    o0 = 2 * ML_QK + 2 * HALF
    s0 = ml_cols
    wp = jnp.concatenate([w_in[:, :o0], w_in[:, s0:s0 + HALF], w_in[:, o0:ml_cols],
                          w_in[:, s0 + HALF + SSD_CONV_DIM:]], axis=1).astype(BF16)
    wc = w_in[:, s0 + HALF:s0 + HALF + SSD_CONV_DIM].astype(BF16)
    cb = conv_b.reshape(1, SSD_CONV_DIM)
    psplit = (2 * ML_QK, HALF, HALF, HALF, SMALL)
    csplit = (HALF, SSD_GN, SSD_GN)
    pc = _inproj(xc, mod_c, wp, wc, conv_w, cb, psplit, csplit, True, "row", False, 512)
    pl_ = _inproj(xl, mod_l, wp, wc, conv_w, cb, psplit, csplit, True, "col", False, 512)
    m_c, m_l = _mlstm_mixer((pc[0], pc[1], pc[4]), (pl_[0], pl_[1], pl_[4]), gate_b)
    s_c, s_l = _ssd_mixer((pc[5], pc[6], pc[7], pc[4]), (pl_[5], pl_[6], pl_[7], pl_[4]), dt_bias, a_log)
    consts = [ml_norm.reshape(1, ML_DV), jnp.repeat(d_skip, SSD_HEADDIM).reshape(1, HALF),
              ssd_norm.reshape(1, HALF)]
    specs = [lambda tm: _seq_spec(tm, HALF)] * 7
    xl = _outproj(_outproj_odd_kernel, [m_l[0], m_l[1], pl_[2], s_l[0], s_l[1], pl_[5], pl_[3]], specs,
                  xl, mod_l, w_out, ln_g, ln_b, consts, "col", 512)
    if need_ctx:
        xc = _outproj(_outproj_odd_kernel, [m_c[0], m_c[1], pc[2], s_c[0], s_c[1], pc[5], pc[3]], specs,
                      xc, mod_c, w_out, ln_g, ln_b, consts, "row", 512)
    return xc, xl


def kernel(x, c, ctx, c_ctx, w_ada, b_ada, ln_g, ln_b, w_ffn_in, w_ffn_out, w_in_even, w_out_even, hy_conv_w, hy_conv_b, hy_w1, hy_b1, hy_w2, hy_b2, hy_w3, hy_freq, hy_bias, gla_a_w, gla_a_b, gla_norm_w, w_in_odd, w_out_odd, ml_gate_b, ml_norm_w, ssd_conv_w, ssd_conv_b, ssd_dt_bias, ssd_a_log, ssd_d, ssd_norm_w):
    bsz, n, d = x.shape
    depth = w_ada.shape[0]
    assert bsz == 2 and d == D_MODEL
    cond8 = jnp.zeros((8, d), F32).at[:bsz].set(c).at[bsz].set(c_ctx)
    mod_all = _ada_all(cond8, w_ada, b_ada)
    xl, xc = x, ctx
    for l in range(depth):
        last = l == depth - 1
        mod_l = mod_all[l, :bsz].reshape(bsz, N_MOD, d)
        mod_c = jnp.broadcast_to(mod_all[l, bsz].reshape(1, N_MOD, d), (bsz, N_MOD, d))
        w1a, w2a = w_ffn_in[l, 0].astype(BF16), w_ffn_out[l, 0].astype(BF16)
        w1b, w2b = w_ffn_in[l, 1].astype(BF16), w_ffn_out[l, 1].astype(BF16)
        odd = l % 2 == 1
        xl = _ffn_step(xl, mod_l, 0, w1a, w2a, ln_g[l, 0], ln_b[l, 0], out_cv=odd)
        xc = _ffn_step(xc, mod_c, 0, w1a, w2a, ln_g[l, 0], ln_b[l, 0])
        i = l // 2
        if not odd:
            hy_p = (hy_conv_w[i], hy_conv_b[i],
                    (hy_w1[i], hy_b1[i], hy_w2[i], hy_b2[i], hy_w3[i], hy_freq[i]), hy_bias[i])
            gla_p = (gla_a_w[i], gla_a_b[i], gla_norm_w[i])
            xc, xl = _even_layer_mix(xc, xl, mod_c, mod_l, w_in_even[i], w_out_even[i].astype(BF16),
                                     hy_p, gla_p, ln_g[l, 1], ln_b[l, 1], not last)
        else:
            ml_p = (ml_gate_b[i], ml_norm_w[i])
            ssd_p = (ssd_conv_w[i], ssd_conv_b[i], ssd_dt_bias[i], ssd_a_log[i], ssd_d[i], ssd_norm_w[i])
            xc, xl = _odd_layer_mix(xc, xl, mod_c, mod_l, w_in_odd[i], w_out_odd[i].astype(BF16),
                                    ml_p, ssd_p, ln_g[l, 1], ln_b[l, 1], not last)
        xl = _ffn_step(xl, mod_l, 6, w1b, w2b, ln_g[l, 2], ln_b[l, 2], in_cv=odd)
        if not last:
            xc = _ffn_step(xc, mod_c, 6, w1b, w2b, ln_g[l, 2], ln_b[l, 2])
    return xl
```

```python
import functools
import math

import numpy as np
import jax
import jax.numpy as jnp
from jax import lax
from jax.experimental import pallas as pl
from jax.experimental.pallas import tpu as pltpu

F32 = jnp.float32
BF16 = jnp.bfloat16

D_MODEL = 1024
DEPTH = 4
GRID_W = 64
D_FF = 2816
N_MOD = 9
HALF = D_MODEL // 2
CHUNK = 64
LN_EPS = 1e-5
NORM_EPS = 1e-6

HY_CH = HALF
HY_ORDER = 2
HY_EMB = 33
HY_BANDS = (HY_EMB - 1) // 2
HY_FAST_DECAY = 0.3
HY_SLOW_DECAY = 1.5
HY_TARGET = 1e-2
HY_COLS = 3 * HY_CH
HY_ROWS = HY_ORDER * HY_CH

GLA_HEADS = 4
GLA_DK = HALF // 2 // GLA_HEADS
GLA_DV = HALF // GLA_HEADS
GLA_RANK = 16
GLA_TAU = 16.0
GLA_QK = GLA_HEADS * GLA_DK

ML_HEADS = 4
ML_DQK = HALF // 2 // ML_HEADS
ML_DV = HALF // ML_HEADS
ML_QK = ML_HEADS * ML_DQK

SSD_HEADDIM = 64
SSD_HEADS = HALF // SSD_HEADDIM
SSD_GROUPS = 2
SSD_HPG = SSD_HEADS // SSD_GROUPS
SSD_STATE = 128
SSD_GN = SSD_GROUPS * SSD_STATE
SSD_CONV_DIM = HALF + 2 * SSD_GN

DN_ALPHA = (2.0 * DEPTH) ** 0.25

SMALL = 32
LANES = 128
VMEM_LIMIT = 56 * 1024 * 1024


def _cparams(*sem):
    return pltpu.CompilerParams(dimension_semantics=sem, vmem_limit_bytes=VMEM_LIMIT)


def _mm(a, b):
    return jnp.dot(a.astype(BF16), b.astype(BF16), preferred_element_type=F32)


def _mm_nt(a, b):
    return lax.dot_general(a.astype(BF16), b.astype(BF16), (((1,), (1,)), ((), ())),
                           preferred_element_type=F32)


def _mm_tn(a, b):
    return lax.dot_general(a.astype(BF16), b.astype(BF16), (((0,), (0,)), ((), ())),
                           preferred_element_type=F32)


def _mm_hi(a, b):
    return jnp.dot(a, b, precision=lax.Precision.HIGHEST, preferred_element_type=F32)


def _mm_nt_hi(a, b):
    return lax.dot_general(a, b, (((1,), (1,)), ((), ())), precision=lax.Precision.HIGHEST,
                           preferred_element_type=F32)


def _split3(x):
    x1 = x.astype(BF16)
    r1 = x - x1.astype(F32)
    x2 = r1.astype(BF16)
    return x1, x2, (r1 - x2.astype(F32)).astype(BF16)


def _sel_rows(sel3, x):
    return jnp.dot(sel3.astype(BF16), jnp.concatenate(_split3(x), axis=0), preferred_element_type=F32)


def _sel_cols(x, sel):
    sel = sel.astype(BF16)
    return sum(jnp.dot(p, sel, preferred_element_type=F32) for p in _split3(x))


def _sel_nt(a, b, split_a):
    dn = (((1,), (1,)), ((), ()))
    if split_a:
        return sum(lax.dot_general(p, b.astype(BF16), dn, preferred_element_type=F32) for p in _split3(a))
    return sum(lax.dot_general(a.astype(BF16), p, dn, preferred_element_type=F32) for p in _split3(b))


def _tri3(reverse):
    row = lax.broadcasted_iota(jnp.int32, (CHUNK, 3 * CHUNK), 0)
    col = lax.broadcasted_iota(jnp.int32, (CHUNK, 3 * CHUNK), 1) % CHUNK
    return ((row <= col) if reverse else (row >= col)).astype(F32)


def _silu(x):
    return x * jax.nn.sigmoid(x)


def _log_sigmoid(x):
    return jnp.minimum(x, 0.0) - jnp.log1p(jnp.exp(-jnp.abs(x)))


def _softplus(x):
    return jnp.maximum(x, 0.0) + jnp.log1p(jnp.exp(-jnp.abs(x)))


def _layer_norm(y, g, b):
    yc = y - jnp.mean(y, axis=-1, keepdims=True)
    return yc * lax.rsqrt(jnp.mean(yc * yc, axis=-1, keepdims=True) + LN_EPS) * g + b


def _rms(y):
    return y * lax.rsqrt(jnp.mean(y * y, axis=-1, keepdims=True) + NORM_EPS)


def _tri(reverse):
    row = lax.broadcasted_iota(jnp.int32, (CHUNK, CHUNK), 0)
    col = lax.broadcasted_iota(jnp.int32, (CHUNK, CHUNK), 1)
    return (row <= col) if reverse else (row >= col)


def _eye(n):
    return (lax.broadcasted_iota(jnp.int32, (n, n), 0) == lax.broadcasted_iota(jnp.int32, (n, n), 1)).astype(F32)


def _resident(shape):
    nd = len(shape)
    return pl.BlockSpec(shape, lambda *_: (0,) * nd, pipeline_mode=pl.Buffered(1))


def _cplx_mm(ar, ai, w):
    u, v = _mm(ar, w), _mm(ai, w)
    h = w.shape[1] // 2
    return u[:, :h] - v[:, h:], u[:, h:] + v[:, :h]


def _cplx_mm_left(fr, fi, xr, xi):
    rhs = jnp.concatenate([xr, xi], axis=1)
    p, q = _mm(fr, rhs), _mm(fi, rhs)
    h = xr.shape[1]
    return p[:, :h] - q[:, h:], p[:, h:] + q[:, :h]


def _ada_kernel(c_ref, w_ref, b_ref, o_ref):
    o_ref[0] = _mm(_silu(c_ref[...]), w_ref[0]) + b_ref[0]


def _ada_all(cond8, w_ada, b_ada):
    depth, d, nm = w_ada.shape
    tn = nm // 8
    return pl.pallas_call(
        _ada_kernel,
        grid=(depth, nm // tn),
        in_specs=[pl.BlockSpec((8, d), lambda l, j: (0, 0)),
                  pl.BlockSpec((1, d, tn), lambda l, j: (l, 0, j)),
                  pl.BlockSpec((1, 1, tn), lambda l, j: (l, 0, j))],
        out_specs=pl.BlockSpec((1, 8, tn), lambda l, j: (l, 0, j)),
        out_shape=jax.ShapeDtypeStruct((depth, 8, nm), F32),
        name="ada",
        compiler_params=_cparams("arbitrary", "arbitrary"),
    )(cond8, w_ada, b_ada.reshape(depth, 1, nm))


def _stream_dims(x, layout):
    if layout == "col":
        return x.shape[0], x.shape[1] * GRID_W, x.shape[2] // GRID_W
    return x.shape


COL_TILE = 2


def _tile_rows(n, layout, tm):
    return COL_TILE * (n // GRID_W) if layout == "col" else min(tm, n)


def _main_spec(n, d, layout, tm):
    if layout == "col":
        return pl.BlockSpec((None, n // GRID_W, COL_TILE * d), lambda b, i: (b, 0, i))
    return pl.BlockSpec((None, tm, d), lambda b, i: (b, i, 0))


def _load_tile(x_ref, d):
    cols = x_ref.shape[1] // d
    if cols == 1:
        return x_ref[...]
    return jnp.concatenate([x_ref[:, c * d:(c + 1) * d] for c in range(cols)], axis=0)


def _store_tile(o_ref, y, d):
    cols = o_ref.shape[1] // d
    rows = o_ref.shape[0]
    if cols == 1:
        o_ref[...] = y
    else:
        for c in range(cols):
            o_ref[:, c * d:(c + 1) * d] = y[c * rows:(c + 1) * rows]


def _halo_specs(n, d, layout, tm):
    if layout == "col":
        r8 = n // GRID_W // 8
        prev = pl.BlockSpec((None, 8, d), lambda b, i: (b, r8 - 1, jnp.maximum(COL_TILE * i - 1, 0)))
        nxt = pl.BlockSpec((None, 8, d), lambda b, i: (b, 0, jnp.minimum(COL_TILE * (i + 1), GRID_W - 1)))
    else:
        t8, n8 = tm // 8, n // 8
        prev = pl.BlockSpec((None, 8, d), lambda b, i: (b, jnp.maximum(i * t8 - 1, 0), 0))
        nxt = pl.BlockSpec((None, 8, d), lambda b, i: (b, jnp.minimum((i + 1) * t8, n8 - 1), 0))
    return prev, nxt


def _seq_spec(tm, cols):
    return pl.BlockSpec((None, tm, cols), lambda b, i: (b, i, 0))


def _chan_spec(rows, tm):
    return pl.BlockSpec((None, rows, tm // LANES, LANES), lambda b, i: (b, 0, i, 0))


def _lanes_of(ref, lead, pieces):
    return jnp.concatenate([ref[lead + (slice(None), q, slice(None))] for q in range(pieces)], axis=1)


def _ffn_kernel(x_ref, mod_ref, w1_ref, w2_ref, g_ref, b_ref, o_ref, *, mo, nk, in_cv, out_cv):
    d = w2_ref.shape[1]
    pieces = in_cv or out_cv
    if in_cv:
        x = jnp.concatenate([x_ref[:, c * d:(c + 1) * d] for c in range(GRID_W)], axis=0)
    elif pieces:
        x = jnp.concatenate([x_ref[:, c, :] for c in range(GRID_W)], axis=0)
    else:
        x = x_ref[...]
    shift, scale, gate = mod_ref[mo:mo + 1, :], mod_ref[mo + 1:mo + 2, :], mod_ref[mo + 2:mo + 3, :]
    h = (x * (1.0 + scale) + shift).astype(BF16)
    tf = D_FF // nk
    acc = jnp.zeros(x.shape, F32)
    for k in range(nk):
        gt = jnp.dot(h, w1_ref[:, k * tf:(k + 1) * tf], preferred_element_type=F32)
        up = jnp.dot(h, w1_ref[:, D_FF + k * tf:D_FF + (k + 1) * tf], preferred_element_type=F32)
        acc = acc + _mm(_silu(gt) * up, w2_ref[k * tf:(k + 1) * tf, :])
    y = _layer_norm(DN_ALPHA * x + (0.5 * gate) * acc, g_ref[...], b_ref[...])
    if out_cv:
        for c in range(GRID_W):
            o_ref[:, c * d:(c + 1) * d] = y[8 * c:8 * c + 8]
    elif pieces:
        for c in range(GRID_W):
            o_ref[:, c, :] = y[8 * c:8 * c + 8]
    else:
        o_ref[...] = y


def _ffn_step(x, mod, mo, w1, w2, g, b, in_cv=False, out_cv=False, tm=512):
    bsz, n, d = _stream_dims(x, "col" if in_cv else "row")
    rows = n // GRID_W
    if in_cv or out_cv:
        tm = 8 * GRID_W
        cv_spec = pl.BlockSpec((None, 8, GRID_W * d), lambda bb, i: (bb, i, 0))
        tok_spec = pl.BlockSpec((None, 8, GRID_W, d), lambda bb, i: (bb, i, 0, 0))
        x = x if in_cv else x.reshape(bsz, rows, GRID_W, d)
        in_spec, out_spec = (cv_spec if in_cv else tok_spec), (cv_spec if out_cv else tok_spec)
        out_dims = (bsz, rows, GRID_W * d) if out_cv else (bsz, rows, GRID_W, d)
    else:
        tm = min(tm, n)
        in_spec = out_spec = _seq_spec(tm, d)
        out_dims = (bsz, n, d)
    out = pl.pallas_call(
        functools.partial(_ffn_kernel, mo=mo, nk=2, in_cv=in_cv, out_cv=out_cv),
        grid=(bsz, n // tm),
        in_specs=[in_spec,
                  pl.BlockSpec((None, N_MOD, d), lambda bb, i: (bb, 0, 0)),
                  _resident(w1.shape), _resident(w2.shape),
                  _resident((1, d)), _resident((1, d))],
        out_specs=out_spec,
        out_shape=jax.ShapeDtypeStruct(out_dims, F32),
        name="ffn",
        compiler_params=_cparams("arbitrary", "arbitrary"),
    )(x, mod, w1, w2, g.reshape(1, d), b.reshape(1, d))
    return out if out_cv else out.reshape(bsz, n, d)


def _inproj_kernel(x_ref, xp_ref, xn_ref, mod_ref, wp_ref, wc_ref, cw_ref, cb_ref, *out_refs,
                   psplit, csplit, act, chan_major):
    i, last = pl.program_id(1), pl.num_programs(1) - 1
    shift, scale = mod_ref[3:4, :], mod_ref[4:5, :]
    x = _load_tile(x_ref, mod_ref.shape[1])
    tm = x.shape[0]
    h = x * (1.0 + scale) + shift
    hb = h.astype(BF16)
    o = 0
    for k, w in enumerate(psplit):
        out_refs[k][...] = jnp.dot(hb, wp_ref[:, o:o + w], preferred_element_type=F32).astype(out_refs[k].dtype)
        o += w
    hp = (xp_ref[...] * (1.0 + scale) + shift) * jnp.where(i > 0, 1.0, 0.0)
    hn = (xn_ref[...] * (1.0 + scale) + shift) * jnp.where(i < last, 1.0, 0.0)
    couts = out_refs[len(psplit):]
    if chan_major:
        lane = lax.broadcasted_iota(jnp.int32, (1, tm), 1)
        halo = jnp.concatenate([hp, hn, jnp.zeros((LANES - 16, hp.shape[1]), F32)], axis=0).astype(BF16)
        o = 0
        for k, w in enumerate(csplit):
            wt = wc_ref[o:o + w, :]
            p = _mm_nt(wt, hb)
            p_halo = _mm_nt(wt, halo)
            up = jnp.where(lane == 0, p_halo[:, 7:8], pltpu.roll(p, 1, 1))
            dn = jnp.where(lane == tm - 1, p_halo[:, 8:9], pltpu.roll(p, tm - 1, 1))
            cw = cw_ref[o:o + w, :]
            y = cw[:, 0:1] * up + cw[:, 1:2] * p + cw[:, 2:3] * dn + cb_ref[o:o + w, :]
            for q in range(tm // LANES):
                couts[k][:, q, :] = y[:, q * LANES:(q + 1) * LANES]
            o += w
    else:
        pc = _mm(jnp.concatenate([hp, h, hn], axis=0), wc_ref[...])
        up = pltpu.roll(pc, 1, 0)
        dn = pltpu.roll(pc, tm + 15, 0)
        y = (cw_ref[0:1, :] * up + cw_ref[1:2, :] * pc + cw_ref[2:3, :] * dn + cb_ref[...])[8:8 + tm]
        if act:
            y = _silu(y)
        o = 0
        for k, w in enumerate(csplit):
            couts[k][...] = y[:, o:o + w].astype(couts[k].dtype)
            o += w


def _inproj(xv, mod, wp, wc, cw, cb, psplit, csplit, act, layout, chan_major, tm):
    bsz, n, d = _stream_dims(xv, layout)
    tm = _tile_rows(n, layout, tm)
    prev, nxt = _halo_specs(n, d, layout, tm)
    pdt = [BF16 if w >= LANES else F32 for w in psplit]
    out_specs = [_seq_spec(tm, w) for w in psplit]
    out_shape = [jax.ShapeDtypeStruct((bsz, n, w), dt) for w, dt in zip(psplit, pdt)]
    if chan_major:
        out_specs += [_chan_spec(w, tm) for w in csplit]
        out_shape += [jax.ShapeDtypeStruct((bsz, w, n // LANES, LANES), F32) for w in csplit]
    else:
        out_specs += [_seq_spec(tm, w) for w in csplit]
        out_shape += [jax.ShapeDtypeStruct((bsz, n, w), BF16) for w in csplit]
    return pl.pallas_call(
        functools.partial(_inproj_kernel, psplit=tuple(psplit), csplit=tuple(csplit), act=act,
                          chan_major=chan_major),
        grid=(bsz, n // tm),
        in_specs=[_main_spec(n, d, layout, tm), prev, nxt,
                  pl.BlockSpec((None, N_MOD, d), lambda bb, i: (bb, 0, 0)),
                  _resident(wp.shape), _resident(wc.shape), _resident(cw.shape), _resident(cb.shape)],
        out_specs=out_specs,
        out_shape=out_shape,
        name="inproj_chan" if chan_major else "inproj",
        compiler_params=_cparams("arbitrary", "arbitrary"),
    )(xv, xv, xv, mod, wp, wc, cw, cb)


def _outproj_even_kernel(hy_ref, of_ref, ob_ref, g_ref, x_ref, mod_ref, w_ref, lg_ref, lb_ref, nw_ref, o_ref):
    tm = x_ref.shape[0]
    y = _mm_tn(_lanes_of(hy_ref, (), tm // LANES), w_ref[0:HALF, :])
    o = of_ref[...].astype(F32) + ob_ref[...].astype(F32)
    g = g_ref[...].astype(F32)
    heads = [_rms(o[:, h * GLA_DV:(h + 1) * GLA_DV]) * nw_ref[...] for h in range(GLA_HEADS)]
    y = y + _mm(_silu(g) * jnp.concatenate(heads, axis=1), w_ref[HALF:2 * HALF, :])
    o_ref[...] = _layer_norm(DN_ALPHA * x_ref[...] + mod_ref[5:6, :] * y, lg_ref[...], lb_ref[...])


def _outproj_odd_kernel(hf_ref, hb_ref, og_ref, yf_ref, yb_ref, xs_ref, z_ref, x_ref, mod_ref, w_ref,
                        lg_ref, lb_ref, mnw_ref, dsk_ref, snw_ref, o_ref):
    h = hf_ref[...].astype(F32) + hb_ref[...].astype(F32)
    heads = []
    for k in range(ML_HEADS):
        hk = h[:, k * ML_DV:(k + 1) * ML_DV]
        hc = hk - jnp.mean(hk, axis=-1, keepdims=True)
        heads.append(hc * lax.rsqrt(jnp.mean(hc * hc, axis=-1, keepdims=True) + NORM_EPS) * mnw_ref[...])
    m = jax.nn.sigmoid(og_ref[...].astype(F32)) * jnp.concatenate(heads, axis=1)
    s = yf_ref[...].astype(F32) + yb_ref[...].astype(F32) + dsk_ref[...] * xs_ref[...].astype(F32)
    s = _rms(s * _silu(z_ref[...].astype(F32))) * snw_ref[...]
    y = _mm(m, w_ref[0:HALF, :]) + _mm(s, w_ref[HALF:2 * HALF, :])
    d = mod_ref.shape[1]
    x = _load_tile(x_ref, d)
    _store_tile(o_ref, _layer_norm(DN_ALPHA * x + mod_ref[5:6, :] * y, lg_ref[...], lb_ref[...]), d)


def _outproj(kern, parts, part_specs, xv, mod, w, g, b, consts, layout, tm):
    bsz, n, d = _stream_dims(xv, layout)
    tm = _tile_rows(n, layout, tm)
    consts = [g.reshape(1, d), b.reshape(1, d)] + consts
    return pl.pallas_call(
        kern,
        grid=(bsz, n // tm),
        in_specs=[s(tm) for s in part_specs] + [_main_spec(n, d, layout, tm),
                  pl.BlockSpec((None, N_MOD, d), lambda bb, i: (bb, 0, 0)), _resident(w.shape)]
        + [_resident(c.shape) for c in consts],
        out_specs=_main_spec(n, d, layout, tm),
        out_shape=jax.ShapeDtypeStruct(xv.shape, F32),
        name=kern.__name__.strip("_"),
        compiler_params=_cparams("arbitrary", "arbitrary"),
    )(*parts, xv, mod, w, *consts)


def _scan_call(kernel, seqs, consts, states, out_cols, tb=512):
    bsz, n, _ = seqs[0].shape
    tb = min(tb, n)
    nblk = n // tb

    def fmap(i):
        return (0, i, 0)

    def bmap(i):
        return (0, nblk - 1 - i, 0)

    in_specs = [pl.BlockSpec((bsz, tb, s.shape[-1]), fmap) for s in seqs]
    in_specs += [pl.BlockSpec((bsz, tb, s.shape[-1]), bmap) for s in seqs]
    in_specs += [_resident(c.shape) for c in consts]
    st_specs = [pl.BlockSpec(s.shape, lambda i, nd=s.ndim: (0,) * nd) for s in states]
    outs = pl.pallas_call(
        functools.partial(kernel, nchunk=tb // CHUNK),
        grid=(nblk,),
        in_specs=in_specs + st_specs,
        out_specs=[pl.BlockSpec((bsz, tb, out_cols), fmap), pl.BlockSpec((bsz, tb, out_cols), bmap)] + st_specs,
        out_shape=[jax.ShapeDtypeStruct((bsz, n, out_cols), BF16)] * 2
        + [jax.ShapeDtypeStruct(s.shape, F32) for s in states],
        scratch_shapes=[pltpu.VMEM(s.shape, F32) for s in states],
        name=kernel.__name__.strip("_"),
        compiler_params=_cparams("arbitrary"),
    )(*seqs, *seqs, *consts, *states)
    return outs[0], outs[1], tuple(outs[2:])


def _chunk_rows(ci, nchunk, reverse):
    c = (nchunk - 1 - ci) if reverse else ci
    return pl.ds(pl.multiple_of(c * CHUNK, CHUNK), CHUNK)


def _chains(ci, nchunk, bsz, unroll):
    return [(d, b, _chunk_rows(ci * unroll + u, nchunk, d == 1))
            for u in range(unroll) for d in range(2) for b in range(bsz)]


def _interleave(gens):
    results = [None] * len(gens)
    live = list(enumerate(gens))
    while live:
        still = []
        for idx, g in live:
            try:
                next(g)
                still.append((idx, g))
            except StopIteration as stop:
                results[idx] = stop.value
        live = still
    return results


def _scan_states(i, s0_refs, sout_refs, scrs, body, trips):
    @pl.when(i == 0)
    def _():
        for s0, scr in zip(s0_refs, scrs):
            scr[...] = s0[...]

    lax.fori_loop(0, trips, body, 0)

    @pl.when(i == pl.num_programs(0) - 1)
    def _():
        for so, scr in zip(sout_refs, scrs):
            so[...] = scr[...]


def _gla_chunk(qk, v, a, aw, ab, state, tri, head):
    heads = range(GLA_HEADS)
    q = qk[:, 0:GLA_QK].astype(F32) * (GLA_DK ** -0.5)
    k = qk[:, GLA_QK:2 * GLA_QK].astype(F32)
    tri, tri3 = tri
    lr = _mm(a, aw)
    yield
    log_a = _log_sigmoid(lr + ab) * (1.0 / GLA_TAU)
    b = _sel_rows(tri3, log_a)
    yield
    b_last = jnp.sum(log_a, axis=0, keepdims=True)
    q_dec = q * jnp.exp(b)
    k_inv = k * jnp.exp(-b)
    k_dec = k * jnp.exp(b_last - b)
    vs = [v[:, h * GLA_DV:(h + 1) * GLA_DV] for h in heads]
    q4 = jnp.concatenate([jnp.where(head == h, q_dec, 0.0) for h in heads], axis=0).astype(BF16)
    att4 = _mm_nt(q4, k_inv)
    kv = _mm_tn(jnp.concatenate(vs, axis=0),
                jnp.concatenate([jnp.where(head == h, k_dec, 0.0) for h in heads], axis=0))
    yield
    s_old = state[0]
    inter4 = _mm_nt(q4, s_old)
    state[0] = s_old * jnp.exp(b_last) + kv
    outs = [_mm(jnp.where(tri, att4[h * CHUNK:(h + 1) * CHUNK], 0.0), vs[h]) for h in heads]
    yield
    return [outs[h] + inter4[h * CHUNK:(h + 1) * CHUNK] for h in heads]


def _gla_kernel(qkf_ref, vf_ref, af_ref, qkb_ref, vb_ref, ab_ref, aw_ref, abias_ref, s0_ref,
                of_ref, ob_ref, sout_ref, s_scr, *, nchunk):
    head = lax.broadcasted_iota(jnp.int32, (1, GLA_QK), 1) // GLA_DK
    tris = tuple((_tri(rev), _tri3(rev)) for rev in (False, True))
    dirs = ((qkf_ref, vf_ref, af_ref, of_ref), (qkb_ref, vb_ref, ab_ref, ob_ref))
    bsz, unroll = qkf_ref.shape[0], 4

    def body(ci, carry):
        chains = _chains(ci, nchunk, bsz, unroll)
        states = {(d, b): [s_scr[d, b]] for d in range(2) for b in range(bsz)}
        done = _interleave([_gla_chunk(dirs[d][0][b, rows, :], dirs[d][1][b, rows, :], dirs[d][2][b, rows, :],
                                       aw_ref[d], abias_ref[d], states[d, b], tris[d], head)
                            for d, b, rows in chains])
        for (d, b, rows), outs in zip(chains, done):
            for h, oh in enumerate(outs):
                dirs[d][3][b, rows, h * GLA_DV:(h + 1) * GLA_DV] = oh.astype(BF16)
        for (d, b), st in states.items():
            s_scr[d, b] = st[0]
        return carry

    _scan_states(pl.program_id(0), [s0_ref], [sout_ref], [s_scr], body, nchunk // unroll)


def _gla_mixer(ctx_p, lat_p, a_w, a_b):
    bsz = ctx_p[0].shape[0]
    aw = jnp.zeros((2, SMALL, GLA_QK), F32)
    for d in range(2):
        aw = aw.at[d, d * GLA_RANK:(d + 1) * GLA_RANK].set(a_w[d])
    ab = a_b.reshape(2, 1, GLA_QK)
    st = (jnp.zeros((2, bsz, GLA_DV, GLA_QK), F32),)
    cf, cb, st = _scan_call(_gla_kernel, list(ctx_p), [aw, ab], st, HALF)
    lf, lb, _ = _scan_call(_gla_kernel, list(lat_p), [aw, ab], st, HALF)
    return (cf, cb), (lf, lb)


def _mlstm_chunk(qk, v, gates, state, tri, head, eye, shift, spread, d):
    q = qk[:, 0:ML_QK].astype(F32)
    k = qk[:, ML_QK:2 * ML_QK].astype(F32) * (ML_DQK ** -0.5)
    tri, tri3 = tri
    heads = range(ML_HEADS)
    trif = tri.astype(F32)
    gates_t = _mm_nt_hi(eye, gates)
    lf = _log_sigmoid(gates)
    b_col = _mm_hi(trif, lf)
    yield
    b_row = _mm_nt_hi(_log_sigmoid(gates_t), trif)
    b_tot = jnp.sum(lf, axis=0, keepdims=True)
    vs = [v[:, h * ML_DV:(h + 1) * ML_DV] for h in heads]
    qms = [jnp.where(head == h, q, 0.0) for h in heads]
    s_raw = [_mm_nt(qm, k) for qm in qms]
    c_old, n_all, m_all = state
    n_old = n_all[0:1, :]
    inter = [_mm_nt(qm, c_old) for qm in qms]
    m_old = m_all[0:1, :]
    a = b_tot - b_col + _sel_cols(gates, shift)
    a_max = jnp.max(a, axis=0, keepdims=True)
    m_new = jnp.maximum(b_tot + m_old, a_max)
    w_old, w_new = jnp.exp(b_tot + m_old - m_new), jnp.exp(a_max - m_new)
    kw_all = k * _mm(jnp.exp(a - a_max) * w_new, spread)
    keep = _sel_cols(jnp.broadcast_to(w_old, (8, SMALL)), spread)[0:1, :]
    n_add = jnp.sum(kw_all, axis=0, keepdims=True)
    c_add = _mm_tn(jnp.concatenate(vs, axis=0),
                   jnp.concatenate([jnp.where(head == h, kw_all, 0.0) for h in heads], axis=0))
    state[:] = [c_old * keep + c_add, jnp.broadcast_to(n_old * keep + n_add, n_all.shape),
                jnp.broadcast_to(m_new, m_all.shape)]
    yield
    ms, es, ss = [], [], []
    for h in heads:
        ic, fc = d * 8 + h, d * 8 + 4 + h
        bc = b_col[:, fc:fc + 1]
        r_vis = jnp.where(tri, gates_t[ic:ic + 1, :] - b_row[fc:fc + 1, :], -jnp.inf)
        g = bc + m_all[0:1, fc:fc + 1]
        m = jnp.maximum(g, bc + jnp.max(r_vis, axis=1, keepdims=True))
        ms.append(m)
        es.append(jnp.exp(g - m))
        ss.append(s_raw[h] * jnp.exp(r_vis + (bc - m)))
    nums = [_mm(ss[h], vs[h]) for h in heads]
    yield
    outs = []
    for h in heads:
        num = nums[h] + es[h] * inter[h]
        den = jnp.sum(ss[h], axis=1, keepdims=True) + es[h] * jnp.sum(qms[h] * n_old, axis=1, keepdims=True)
        outs.append(num / jnp.maximum(jnp.abs(den), jnp.exp(-ms[h])))
    return outs


def _mlstm_kernel(qkf_ref, vf_ref, gf_ref, qkb_ref, vb_ref, gb_ref, bias_ref, c0_ref, n0_ref, m0_ref,
                  hf_ref, hb_ref, cout_ref, nout_ref, mout_ref, c_scr, n_scr, m_scr, *, nchunk):
    head = lax.broadcasted_iota(jnp.int32, (1, ML_QK), 1) // ML_DQK
    col_r = lax.broadcasted_iota(jnp.int32, (SMALL, SMALL), 0)
    col_c = lax.broadcasted_iota(jnp.int32, (SMALL, SMALL), 1)
    shift = ((col_c == col_r + ML_HEADS) & (col_r % 8 < ML_HEADS)).astype(F32)
    row = lax.broadcasted_iota(jnp.int32, (SMALL, ML_QK), 0)
    lane_head = lax.broadcasted_iota(jnp.int32, (SMALL, ML_QK), 1) // ML_DQK
    spreads = [(row == d * 8 + ML_HEADS + lane_head).astype(F32) for d in range(2)]
    eye = _eye(SMALL)
    tris = tuple((_tri(rev), _tri3(rev)) for rev in (False, True))
    dirs = ((qkf_ref, vf_ref, gf_ref, hf_ref), (qkb_ref, vb_ref, gb_ref, hb_ref))
    bsz, unroll = qkf_ref.shape[0], 2

    def body(ci, carry):
        chains = _chains(ci, nchunk, bsz, unroll)
        states = {(d, b): [c_scr[d, b], n_scr[d, b], m_scr[d, b]] for d in range(2) for b in range(bsz)}
        done = _interleave([_mlstm_chunk(
            dirs[d][0][b, rows, :], dirs[d][1][b, rows, :], dirs[d][2][b, rows, :] + bias_ref[...],
            states[d, b], tris[d], head, eye, shift, spreads[d], d) for d, b, rows in chains])
        for (d, b, rows), outs in zip(chains, done):
            for h, oh in enumerate(outs):
                dirs[d][3][b, rows, h * ML_DV:(h + 1) * ML_DV] = oh.astype(BF16)
        for (d, b), st in states.items():
            c_scr[d, b], n_scr[d, b], m_scr[d, b] = st
        return carry

    _scan_states(pl.program_id(0), [c0_ref, n0_ref, m0_ref], [cout_ref, nout_ref, mout_ref],
                 [c_scr, n_scr, m_scr], body, nchunk // unroll)


def _mlstm_mixer(ctx_p, lat_p, gate_b):
    bsz = ctx_p[0].shape[0]
    st = (jnp.zeros((2, bsz, ML_DV, ML_QK), F32), jnp.zeros((2, bsz, 8, ML_QK), F32),
          jnp.zeros((2, bsz, 8, SMALL), F32))
    gb = jnp.zeros((1, SMALL), F32).at[0, :16].set(gate_b.reshape(16))
    cf, cb, st = _scan_call(_mlstm_kernel, list(ctx_p), [gb], st, HALF)
    lf, lb, _ = _scan_call(_mlstm_kernel, list(lat_p), [gb], st, HALF)
    return (cf, cb), (lf, lb)


def _ssd_chunk(x, bm, cm, raw, alr, alc, expand, state, tri, sub, eye, d):
    x, bm, cm = x.astype(F32), bm.astype(F32), cm.astype(F32)
    gw = SSD_HPG * SSD_HEADDIM
    tri, tri3 = tri
    groups = range(SSD_GROUPS)
    lns = [slice(g * gw, (g + 1) * gw) for g in groups]
    cms = [cm[:, g * SSD_STATE:(g + 1) * SSD_STATE] for g in groups]
    bms = [bm[:, g * SSD_STATE:(g + 1) * SSD_STATE] for g in groups]
    raw_t = _sel_nt(eye, raw, False)
    dt = _softplus(raw)
    da = dt * (-jnp.exp(alr))
    cum = _sel_rows(tri3, da)
    cbs = [_mm_nt(cms[g], bms[g]) for g in groups]
    yield
    dt_t = _softplus(raw_t)
    cum_t = _sel_nt(dt_t * (-jnp.exp(alc)), tri, True)
    cl = jnp.sum(da, axis=0, keepdims=True)
    e_cum = _sel_cols(jnp.exp(cum), expand)
    wgt = _sel_cols(jnp.exp(cl - cum) * dt, expand)
    dec = _sel_cols(jnp.broadcast_to(jnp.exp(cl), (8, SMALL)), expand)[0:1, :]
    yield
    xw = x * wgt
    kvs = [_mm_tn(bms[g], xw[:, lns[g]]) for g in groups]
    diags = []
    for g in groups:
        xg = x[:, lns[g]]
        mixes = []
        for j in range(SSD_HPG):
            col = 16 + d * 8 + g * SSD_HPG + j
            seg = jnp.exp(jnp.where(tri, cum[:, col:col + 1] - cum_t[col:col + 1, :], -jnp.inf))
            mixes.append(cbs[g] * seg * dt_t[col:col + 1, :])
        diags.append(_mm(jnp.concatenate(mixes, axis=1),
                         jnp.concatenate([jnp.where(sub == j, xg, 0.0) for j in range(SSD_HPG)], axis=0)))
    yield
    offs = [_mm(cms[g], state[g]) for g in groups]
    state[:] = [state[g] * dec[:, lns[g]] + kvs[g] for g in groups]
    yield
    return [offs[g] * e_cum[:, lns[g]] + diags[g] for g in groups]


def _ssd_kernel(xf_ref, bmf_ref, cmf_ref, rf_ref, xb_ref, bmb_ref, cmb_ref, rb_ref,
                dtb_ref, alr_ref, alc_ref, ex_ref, s0_ref, yf_ref, yb_ref, sout_ref, s_scr, *, nchunk):
    gw = SSD_HPG * SSD_HEADDIM
    sub = lax.broadcasted_iota(jnp.int32, (1, gw), 1) // SSD_HEADDIM
    eye = _eye(SMALL)
    tris = tuple((_tri(rev), _tri3(rev)) for rev in (False, True))
    dirs = ((xf_ref, bmf_ref, cmf_ref, rf_ref, yf_ref), (xb_ref, bmb_ref, cmb_ref, rb_ref, yb_ref))

    bsz, unroll = xf_ref.shape[0], 4

    def body(ci, carry):
        chains = _chains(ci, nchunk, bsz, unroll)
        states = {(d, b): [s_scr[d, b, :, g * gw:(g + 1) * gw] for g in range(SSD_GROUPS)]
                  for d in range(2) for b in range(bsz)}
        done = _interleave([_ssd_chunk(
            dirs[d][0][b, rows, :], dirs[d][1][b, rows, :], dirs[d][2][b, rows, :],
            dirs[d][3][b, rows, :] + dtb_ref[...], alr_ref[...], alc_ref[...],
            ex_ref[d], states[d, b], tris[d], sub, eye, d) for d, b, rows in chains])
        for (d, b, rows), ys in zip(chains, done):
            for g in range(SSD_GROUPS):
                dirs[d][4][b, rows, g * gw:(g + 1) * gw] = ys[g].astype(BF16)
        for (d, b), st in states.items():
            for g in range(SSD_GROUPS):
                s_scr[d, b, :, g * gw:(g + 1) * gw] = st[g]
        return carry

    _scan_states(pl.program_id(0), [s0_ref], [sout_ref], [s_scr], body, nchunk // unroll)


def _ssd_mixer(ctx_p, lat_p, dt_bias, a_log):
    bsz = ctx_p[0].shape[0]
    st = (jnp.zeros((2, bsz, SSD_STATE, HALF), F32),)
    dtb = jnp.zeros((1, SMALL), F32).at[0, 16:].set(dt_bias.reshape(16))
    al = jnp.zeros((1, SMALL), F32).at[0, 16:].set(a_log.reshape(16).astype(F32))
    lane_head = np.arange(HALF) // SSD_HEADDIM
    ex = jnp.asarray(np.stack([(np.arange(SMALL)[:, None] == (16 + d * 8 + lane_head)[None, :])
                               for d in range(2)]).astype(np.float32))
    consts = [dtb, al, al.reshape(SMALL, 1), ex]
    cf, cb, st = _scan_call(_ssd_kernel, list(ctx_p), consts, st, HALF)
    lf, lb, _ = _scan_call(_ssd_kernel, list(lat_p), consts, st, HALF)
    return (cf, cb), (lf, lb)


def _hy_filter_kernel(f_ref, w1t_ref, w1c_ref, w1s_ref, b1_ref, w2_ref, b2_ref, fq1_ref, fq2_ref, w3s_ref, w3b_ref,
                      dl_ref, o_ref, s_ref, *, n, npiece):
    side, jt = pl.program_id(0), pl.program_id(1)
    tnorm = 1.0 / max(n - 1, 1)
    first = (side == 0) & (jt == 0)
    width = npiece * LANES

    @pl.when(first)
    def _():
        s_ref[...] = jnp.zeros(s_ref.shape, F32)

    def mlp(pos):
        ang = pos * f_ref[...] * (2.0 * math.pi / n)
        pre = (pos * tnorm) * w1t_ref[...] + _mm_hi(w1c_ref[...], jnp.cos(ang)) \
            + _mm_hi(w1s_ref[...], -jnp.sin(ang)) + b1_ref[...]
        hid = jnp.sin(fq1_ref[...] * pre)
        return jnp.sin(fq2_ref[...] * (_mm_hi(w2_ref[...], hid) + b2_ref[...]))

    h00 = _mm(w3b_ref[...], mlp(jnp.zeros((1, LANES), F32)))[:, 0:1]
    tau = jt * width + lax.broadcasted_iota(jnp.int32, (1, width), 1)
    pos = (side * n + (1 - 2 * side) * tau).astype(F32)
    ht = _mm(w3s_ref[...], mlp(pos))
    ht = ht * jnp.exp(-(pos * tnorm) * dl_ref[...])
    ht = jnp.where((tau == 0) & (side == 1), 0.0, ht)
    s_ref[...] += jnp.sum(jnp.abs(ht), axis=1, keepdims=True) + jnp.where(first, jnp.abs(h00), 0.0)
    ht = ht + jnp.where((tau == 0) & (side == 0), h00, 0.0)
    for q in range(npiece):
        o_ref[q] = ht[:, q * LANES:(q + 1) * LANES]


def _hyena_filter(n, w1, b1, w2, b2, w3, freq):
    kk = n // LANES
    npiece = min(8, kk)
    ff = w1.shape[1]
    pad = LANES - ff
    bands = jnp.linspace(1e-4, HY_BANDS - 1, HY_BANDS, dtype=F32).reshape(HY_BANDS, 1)
    deltas = jnp.abs(jnp.linspace(math.log(HY_TARGET) / HY_SLOW_DECAY, math.log(HY_TARGET) / HY_FAST_DECAY,
                                  HY_CH, dtype=F32))
    dl = jnp.tile(deltas, HY_ORDER).reshape(HY_ROWS, 1)
    w3t = jnp.transpose(w3.reshape(ff, HY_ORDER, 2, HY_CH), (2, 1, 3, 0)).reshape(2, HY_ROWS, ff)
    w3t = jnp.pad(w3t, ((0, 0), (0, 0), (0, pad)))
    args = (bands, w1[0].reshape(ff, 1), jnp.transpose(w1[1:1 + HY_BANDS]), jnp.transpose(w1[1 + HY_BANDS:]),
            b1.reshape(ff, 1), jnp.pad(jnp.transpose(w2), ((0, pad), (0, 0))),
            jnp.pad(b2.reshape(ff, 1), ((0, pad), (0, 0))),
            freq[0].reshape(ff, 1), jnp.pad(freq[1].reshape(ff, 1), ((0, pad), (0, 0))))
    nt = kk // npiece
    return pl.pallas_call(
        functools.partial(_hy_filter_kernel, n=n, npiece=npiece),
        grid=(2, nt),
        in_specs=[_resident(a.shape) for a in args]
        + [pl.BlockSpec((None, HY_ROWS, LANES), lambda sd, j: (sd, 0, 0)),
           pl.BlockSpec((None, HY_ROWS, LANES), lambda sd, j: (1, 0, 0)), _resident(dl.shape)],
        out_specs=[pl.BlockSpec((npiece, HY_ROWS, LANES), lambda sd, j: (sd * nt + j, 0, 0)),
                   pl.BlockSpec((HY_ROWS, 1), lambda sd, j: (0, 0))],
        out_shape=[jax.ShapeDtypeStruct((2 * kk, HY_ROWS, LANES), F32),
                   jax.ShapeDtypeStruct((HY_ROWS, 1), F32)],
        name="hy_filter",
        compiler_params=_cparams("arbitrary", "arbitrary"),
    )(*args, w3t, w3t, dl)


def _dft_tables(n):
    big = 2 * n
    n1 = big // LANES
    k = np.arange(n1)

    def cis(num, den):
        ang = 2.0 * np.pi * (num % den) / den
        return np.stack([np.cos(ang), -np.sin(ang)]).astype(np.float32)

    f1 = cis(k[:, None] * k[None, :], n1)
    k2 = np.arange(LANES)
    f2 = cis(k2[:, None] * k2[None, :], LANES)
    tw = cis(k[:, None] * k2[None, :], big)
    w2 = np.concatenate([f2[0], f2[1]], axis=1)
    w2c = np.concatenate([f2[0], -f2[1]], axis=1)
    ginv = np.stack([f1[0, :n1 // 2, :], -f1[1, :n1 // 2, :]]) / big
    return tuple(jnp.asarray(a) for a in (f1, tw, w2, w2c, ginv))


HY_CB = 8
HY_GRP = 4


def _hy_spec_kernel(f_ref, s_ref, f1_ref, tw_ref, w2_ref, o_ref):
    f1r, f1i = f1_ref[0], f1_ref[1]
    twr, twi = tw_ref[0], tw_ref[1]
    n1 = f1r.shape[0]
    for c0 in range(0, HY_CB, HY_GRP):
        trs, tis = [], []
        for ca in range(c0, c0 + HY_GRP, 2):
            rhs = jnp.concatenate([f_ref[:, ca, :], f_ref[:, ca + 1, :]], axis=1)
            pr, pi = _mm(f1r, rhs), _mm(f1i, rhs)
            for e in range(2):
                ar, ai = pr[:, e * LANES:(e + 1) * LANES], pi[:, e * LANES:(e + 1) * LANES]
                trs.append(ar * twr - ai * twi)
                tis.append(ar * twi + ai * twr)
        xr, xi = _cplx_mm(jnp.concatenate(trs, axis=0), jnp.concatenate(tis, axis=0), w2_ref[...])
        for e in range(HY_GRP):
            inv = 1.0 / s_ref[c0 + e]
            o_ref[0, c0 + e] = xr[e * n1:(e + 1) * n1] * inv
            o_ref[1, c0 + e] = xi[e * n1:(e + 1) * n1] * inv


def _hy_spectrum(filt, norm, f1, tw, w2):
    n1, rows, _ = filt.shape
    return pl.pallas_call(
        _hy_spec_kernel,
        grid=(rows // HY_CB,),
        in_specs=[pl.BlockSpec((n1, HY_CB, LANES), lambda j: (0, j, 0)),
                  pl.BlockSpec((HY_CB, 1, 1), lambda j: (j, 0, 0)),
                  _resident(f1.shape), _resident(tw.shape), _resident(w2.shape)],
        out_specs=pl.BlockSpec((2, HY_CB, n1, LANES), lambda j: (0, j, 0, 0)),
        out_shape=jax.ShapeDtypeStruct((2, rows, n1, LANES), F32),
        name="hy_spectrum",
        compiler_params=_cparams("arbitrary"),
    )(filt, norm.reshape(rows, 1, 1), f1, tw, w2)


def _hy_conv_kernel(v_ref, x_ref, hf_ref, bias_ref, f1_ref, g_ref, tw_ref, w2_ref, w2c_ref, o_ref):
    f1r, f1i = f1_ref[0], f1_ref[1]
    gr, gi = g_ref[0], g_ref[1]
    twr, twi = tw_ref[0], tw_ref[1]
    n1 = f1r.shape[0]
    for c0 in range(0, HY_CB, HY_GRP):
        vs = [(v_ref[0, c], v_ref[1, c]) for c in range(c0, c0 + HY_GRP)]
        trs, tis = [], []
        for v0, v1 in vs:
            ar, ai = _cplx_mm_left(f1r, f1i, v0, v1)
            trs.append(ar * twr - ai * twi)
            tis.append(ar * twi + ai * twr)
        xr, xi = _cplx_mm(jnp.concatenate(trs, axis=0), jnp.concatenate(tis, axis=0), w2_ref[...])
        hr = hf_ref[0, c0:c0 + HY_GRP].reshape(HY_GRP * n1, LANES)
        hi = hf_ref[1, c0:c0 + HY_GRP].reshape(HY_GRP * n1, LANES)
        zr, zi = _cplx_mm(xr * hr - xi * hi, xr * hi + xi * hr, w2c_ref[...])
        for e, (v0, v1) in enumerate(vs):
            zre, zie = zr[e * n1:(e + 1) * n1], zi[e * n1:(e + 1) * n1]
            cr, ci = _cplx_mm_left(gr, gi, zre * twr + zie * twi, zie * twr - zre * twi)
            bz = bias_ref[c0 + e]
            o_ref[0, c0 + e] = x_ref[0, c0 + e] * (cr + bz * v0)
            o_ref[1, c0 + e] = x_ref[1, c0 + e] * (ci + bz * v1)


def _hy_conv(v, xg, hf, bias3, order, tabs):
    f1, tw, w2, w2c, ginv = tabs
    bsz, ch, k1, _ = v.shape
    n1 = 2 * k1
    nb = ch // HY_CB
    f1h = f1[:, :, :k1]
    dat = pl.BlockSpec((2, HY_CB, k1, LANES), lambda j: (0, j, 0, 0))
    return pl.pallas_call(
        _hy_conv_kernel,
        grid=(nb,),
        in_specs=[dat, dat,
                  pl.BlockSpec((2, HY_CB, n1, LANES), lambda j: (0, order * nb + j, 0, 0)),
                  pl.BlockSpec((HY_CB, 1, 1), lambda j: (order * nb + j, 0, 0)),
                  _resident(f1h.shape), _resident(ginv.shape), _resident(tw.shape),
                  _resident(w2.shape), _resident(w2c.shape)],
        out_specs=dat,
        out_shape=jax.ShapeDtypeStruct(v.shape, F32),
        name="hy_conv",
        compiler_params=_cparams("arbitrary"),
    )(v, xg, hf, bias3, f1h, ginv, tw, w2, w2c)


def _hy_ctx_kernel(v_ref, x1_ref, x2_ref, filt_ref, s_ref, bias_ref, f_ref, o_ref, *, n):
    k, big = n // LANES, 2 * n
    fr, fi = f_ref[0], f_ref[1]
    c0, c1 = _lanes_of(v_ref, (0,), k), _lanes_of(v_ref, (1,), k)
    for o, x_ref in enumerate((x1_ref, x2_ref)):
        rows = slice(o * HY_CH, (o + 1) * HY_CH)
        filt = jnp.concatenate([filt_ref[q, rows, :] for q in range(2 * k)], axis=1)
        inv = 1.0 / s_ref[rows, :]
        hr, hi = _mm(filt, fr) * inv, _mm(filt, fi) * inv
        xr = _mm(c0, fr[:n]) - _mm(c1, fi[:n])
        xi = _mm(c0, fi[:n]) + _mm(c1, fr[:n])
        yr, yi = xr * hr - xi * hi, xr * hi + xi * hr
        gr, gi = fr[:, :n] * (1.0 / big), fi[:, :n] * (-1.0 / big)
        cr = _mm(yr, gr) - _mm(yi, gi)
        ci = _mm(yr, gi) + _mm(yi, gr)
        b = bias_ref[rows, :]
        c0 = _lanes_of(x_ref, (0,), k) * (cr + b * c0)
        c1 = _lanes_of(x_ref, (1,), k) * (ci + b * c1)
    for q in range(k):
        o_ref[0, :, q, :] = c0[:, q * LANES:(q + 1) * LANES]
        o_ref[1, :, q, :] = c1[:, q * LANES:(q + 1) * LANES]


def _hyena_ctx(v, x1, x2, filt, norm, bias):
    bsz, ch, k, _ = v.shape
    n = k * LANES
    big = 2 * n
    idx = np.arange(big)
    ang = 2.0 * np.pi * ((idx[:, None] * idx[None, :]) % big) / big
    f = jnp.asarray(np.stack([np.cos(ang), -np.sin(ang)]).astype(np.float32))
    args = (v, x1, x2, filt, norm, bias.reshape(HY_ROWS, 1), f)
    return pl.pallas_call(
        functools.partial(_hy_ctx_kernel, n=n),
        grid=(1,),
        in_specs=[_resident(a.shape) for a in args],
        out_specs=pl.BlockSpec(v.shape, lambda j: (0, 0, 0, 0)),
        out_shape=jax.ShapeDtypeStruct(v.shape, F32),
        name="hy_ctx",
        compiler_params=_cparams("arbitrary"),
    )(*args)


def _hyena_lat(v, x1, x2, filt, norm, bias):
    assert v.shape[0] == 2
    n = v.shape[2] * LANES
    tabs = _dft_tables(n)
    hf = _hy_spectrum(filt, norm, tabs[0], tabs[1], tabs[2])
    bias3 = bias.reshape(HY_ROWS, 1, 1)
    z = _hy_conv(v, x1, hf, bias3, 0, tabs)
    return _hy_conv(z, x2, hf, bias3, 1, tabs)


def _even_layer_mix(xc, xl, mod_c, mod_l, w_in, w_out, hy_p, gla_p, ln_g, ln_b, need_ctx):
    conv_w, conv_b, filt_params, bias = hy_p
    a_w, a_b, gla_norm = gla_p
    wp = w_in[:, HY_COLS:].astype(BF16)
    wc = jnp.transpose(w_in[:, :HY_COLS]).astype(BF16)
    cw, cb = jnp.transpose(conv_w), conv_b.reshape(HY_COLS, 1)
    psplit = (2 * GLA_QK, HALF, HALF, SMALL)
    csplit = (HY_CH, HY_CH, HY_CH)
    pc = _inproj(xc, mod_c, wp, wc, cw, cb, psplit, csplit, False, "row", True, 1024)
    pl_ = _inproj(xl, mod_l, wp, wc, cw, cb, psplit, csplit, False, "row", True, 1024)
    g_c, g_l = _gla_mixer((pc[0], pc[1], pc[3]), (pl_[0], pl_[1], pl_[3]), a_w, a_b)
    filt, norm = _hyena_filter(xl.shape[1], *filt_params)
    h_l = _hyena_lat(pl_[4], pl_[5], pl_[6], filt, norm, bias)
    nw = [gla_norm.reshape(1, GLA_DV)]
    specs = [lambda tm: _chan_spec(HY_CH, tm)] + [lambda tm: _seq_spec(tm, HALF)] * 3
    xl = _outproj(_outproj_even_kernel, [h_l, g_l[0], g_l[1], pl_[2]], specs, xl, mod_l, w_out,
                  ln_g, ln_b, nw, "row", 1024)
    if need_ctx:
        filt_c, norm_c = _hyena_filter(xc.shape[1], *filt_params)
        h_c = _hyena_ctx(pc[4], pc[5], pc[6], filt_c, norm_c, bias)
        xc = _outproj(_outproj_even_kernel, [h_c, g_c[0], g_c[1], pc[2]], specs, xc, mod_c, w_out,
                      ln_g, ln_b, nw, "row", 1024)
    return xc, xl


def _odd_layer_mix(xc, xl, mod_c, mod_l, w_in, w_out, ml_p, ssd_p, ln_g, ln_b, need_ctx):
    gate_b, ml_norm = ml_p
    conv_w, conv_b, dt_bias, a_log, d_skip, ssd_norm = ssd_p
    ml_cols = 2 * ML_QK + 2 * HALF + 4 * ML_HEADS
    o0 = 2 * ML_QK + 2 * HALF
    s0 = ml_cols
    wp = jnp.concatenate([w_in[:, :o0], w_in[:, s0:s0 + HALF], w_in[:, o0:ml_cols],
                          w_in[:, s0 + HALF + SSD_CONV_DIM:]], axis=1).astype(BF16)
    wc = w_in[:, s0 + HALF:s0 + HALF + SSD_CONV_DIM].astype(BF16)
    cb = conv_b.reshape(1, SSD_CONV_DIM)
    psplit = (2 * ML_QK, HALF, HALF, HALF, SMALL)
    csplit = (HALF, SSD_GN, SSD_GN)
    pc = _inproj(xc, mod_c, wp, wc, conv_w, cb, psplit, csplit, True, "row", False, 512)
    pl_ = _inproj(xl, mod_l, wp, wc, conv_w, cb, psplit, csplit, True, "col", False, 512)
    m_c, m_l = _mlstm_mixer((pc[0], pc[1], pc[4]), (pl_[0], pl_[1], pl_[4]), gate_b)
    s_c, s_l = _ssd_mixer((pc[5], pc[6], pc[7], pc[4]), (pl_[5], pl_[6], pl_[7], pl_[4]), dt_bias, a_log)
    consts = [ml_norm.reshape(1, ML_DV), jnp.repeat(d_skip, SSD_HEADDIM).reshape(1, HALF),
              ssd_norm.reshape(1, HALF)]
    specs = [lambda tm: _seq_spec(tm, HALF)] * 7
    xl = _outproj(_outproj_odd_kernel, [m_l[0], m_l[1], pl_[2], s_l[0], s_l[1], pl_[5], pl_[3]], specs,
                  xl, mod_l, w_out, ln_g, ln_b, consts, "col", 512)
    if need_ctx:
        xc = _outproj(_outproj_odd_kernel, [m_c[0], m_c[1], pc[2], s_c[0], s_c[1], pc[5], pc[3]], specs,
                      xc, mod_c, w_out, ln_g, ln_b, consts, "row", 512)
    return xc, xl


def kernel(x, c, ctx, c_ctx, w_ada, b_ada, ln_g, ln_b, w_ffn_in, w_ffn_out, w_in_even, w_out_even, hy_conv_w, hy_conv_b, hy_w1, hy_b1, hy_w2, hy_b2, hy_w3, hy_freq, hy_bias, gla_a_w, gla_a_b, gla_norm_w, w_in_odd, w_out_odd, ml_gate_b, ml_norm_w, ssd_conv_w, ssd_conv_b, ssd_dt_bias, ssd_a_log, ssd_d, ssd_norm_w):
    bsz, n, d = x.shape
    depth = w_ada.shape[0]
    assert bsz == 2 and d == D_MODEL
    cond8 = jnp.zeros((8, d), F32).at[:bsz].set(c).at[bsz].set(c_ctx)
    mod_all = _ada_all(cond8, w_ada, b_ada)
    xl, xc = x, ctx
    for l in range(depth):
        last = l == depth - 1
        mod_l = mod_all[l, :bsz].reshape(bsz, N_MOD, d)
        mod_c = jnp.broadcast_to(mod_all[l, bsz].reshape(1, N_MOD, d), (bsz, N_MOD, d))
        w1a, w2a = w_ffn_in[l, 0].astype(BF16), w_ffn_out[l, 0].astype(BF16)
        w1b, w2b = w_ffn_in[l, 1].astype(BF16), w_ffn_out[l, 1].astype(BF16)
        odd = l % 2 == 1
        xl = _ffn_step(xl, mod_l, 0, w1a, w2a, ln_g[l, 0], ln_b[l, 0], out_cv=odd)
        xc = _ffn_step(xc, mod_c, 0, w1a, w2a, ln_g[l, 0], ln_b[l, 0])
        i = l // 2
        if not odd:
            hy_p = (hy_conv_w[i], hy_conv_b[i],
                    (hy_w1[i], hy_b1[i], hy_w2[i], hy_b2[i], hy_w3[i], hy_freq[i]), hy_bias[i])
            gla_p = (gla_a_w[i], gla_a_b[i], gla_norm_w[i])
            xc, xl = _even_layer_mix(xc, xl, mod_c, mod_l, w_in_even[i], w_out_even[i].astype(BF16),
                                     hy_p, gla_p, ln_g[l, 1], ln_b[l, 1], not last)
        else:
            ml_p = (ml_gate_b[i], ml_norm_w[i])
            ssd_p = (ssd_conv_w[i], ssd_conv_b[i], ssd_dt_bias[i], ssd_a_log[i], ssd_d[i], ssd_norm_w[i])
            xc, xl = _odd_layer_mix(xc, xl, mod_c, mod_l, w_in_odd[i], w_out_odd[i].astype(BF16),
                                    ml_p, ssd_p, ln_g[l, 1], ln_b[l, 1], not last)
        xl = _ffn_step(xl, mod_l, 6, w1b, w2b, ln_g[l, 2], ln_b[l, 2], in_cv=odd)
        if not last:
            xc = _ffn_step(xc, mod_c, 6, w1b, w2b, ln_g[l, 2], ln_b[l, 2])
    return xl
```

```python
import functools
import math

import numpy as np
import jax
import jax.numpy as jnp
from jax import lax
from jax.experimental import pallas as pl
from jax.experimental.pallas import tpu as pltpu

F32 = jnp.float32
BF16 = jnp.bfloat16

D_MODEL = 1024
DEPTH = 4
GRID_W = 64
D_FF = 2816
N_MOD = 9
HALF = D_MODEL // 2
CHUNK = 64
LN_EPS = 1e-5
NORM_EPS = 1e-6

HY_CH = HALF
HY_ORDER = 2
HY_EMB = 33
HY_BANDS = (HY_EMB - 1) // 2
HY_FAST_DECAY = 0.3
HY_SLOW_DECAY = 1.5
HY_TARGET = 1e-2
HY_COLS = 3 * HY_CH
HY_ROWS = HY_ORDER * HY_CH

GLA_HEADS = 4
GLA_DK = HALF // 2 // GLA_HEADS
GLA_DV = HALF // GLA_HEADS
GLA_RANK = 16
GLA_TAU = 16.0
GLA_QK = GLA_HEADS * GLA_DK

ML_HEADS = 4
ML_DQK = HALF // 2 // ML_HEADS
ML_DV = HALF // ML_HEADS
ML_QK = ML_HEADS * ML_DQK

SSD_HEADDIM = 64
SSD_HEADS = HALF // SSD_HEADDIM
SSD_GROUPS = 2
SSD_HPG = SSD_HEADS // SSD_GROUPS
SSD_STATE = 128
SSD_GN = SSD_GROUPS * SSD_STATE
SSD_CONV_DIM = HALF + 2 * SSD_GN

DN_ALPHA = (2.0 * DEPTH) ** 0.25

SMALL = 32
LANES = 128
VMEM_LIMIT = 56 * 1024 * 1024


def _cparams(*sem):
    return pltpu.CompilerParams(dimension_semantics=sem, vmem_limit_bytes=VMEM_LIMIT)


def _mm(a, b):
    return jnp.dot(a.astype(BF16), b.astype(BF16), preferred_element_type=F32)


def _mm_nt(a, b):
    return lax.dot_general(a.astype(BF16), b.astype(BF16), (((1,), (1,)), ((), ())),
                           preferred_element_type=F32)


def _mm_tn(a, b):
    return lax.dot_general(a.astype(BF16), b.astype(BF16), (((0,), (0,)), ((), ())),
                           preferred_element_type=F32)


def _mm_hi(a, b):
    return jnp.dot(a, b, precision=lax.Precision.HIGHEST, preferred_element_type=F32)


def _mm_nt_hi(a, b):
    return lax.dot_general(a, b, (((1,), (1,)), ((), ())), precision=lax.Precision.HIGHEST,
                           preferred_element_type=F32)


def _split3(x):
    x1 = x.astype(BF16)
    r1 = x - x1.astype(F32)
    x2 = r1.astype(BF16)
    return x1, x2, (r1 - x2.astype(F32)).astype(BF16)


def _sel_rows(sel3, x):
    return jnp.dot(sel3.astype(BF16), jnp.concatenate(_split3(x), axis=0), preferred_element_type=F32)


def _sel_cols(x, sel):
    sel = sel.astype(BF16)
    return sum(jnp.dot(p, sel, preferred_element_type=F32) for p in _split3(x))


def _sel_nt(a, b, split_a):
    dn = (((1,), (1,)), ((), ()))
    if split_a:
        return sum(lax.dot_general(p, b.astype(BF16), dn, preferred_element_type=F32) for p in _split3(a))
    return sum(lax.dot_general(a.astype(BF16), p, dn, preferred_element_type=F32) for p in _split3(b))


def _tri3(reverse):
    row = lax.broadcasted_iota(jnp.int32, (CHUNK, 3 * CHUNK), 0)
    col = lax.broadcasted_iota(jnp.int32, (CHUNK, 3 * CHUNK), 1) % CHUNK
    return ((row <= col) if reverse else (row >= col)).astype(F32)


def _silu(x):
    return x * jax.nn.sigmoid(x)


def _log_sigmoid(x):
    return jnp.minimum(x, 0.0) - jnp.log1p(jnp.exp(-jnp.abs(x)))


def _softplus(x):
    return jnp.maximum(x, 0.0) + jnp.log1p(jnp.exp(-jnp.abs(x)))


def _layer_norm(y, g, b):
    yc = y - jnp.mean(y, axis=-1, keepdims=True)
    return yc * lax.rsqrt(jnp.mean(yc * yc, axis=-1, keepdims=True) + LN_EPS) * g + b


def _rms(y):
    return y * lax.rsqrt(jnp.mean(y * y, axis=-1, keepdims=True) + NORM_EPS)


def _tri(reverse):
    row = lax.broadcasted_iota(jnp.int32, (CHUNK, CHUNK), 0)
    col = lax.broadcasted_iota(jnp.int32, (CHUNK, CHUNK), 1)
    return (row <= col) if reverse else (row >= col)


def _eye(n):
    return (lax.broadcasted_iota(jnp.int32, (n, n), 0) == lax.broadcasted_iota(jnp.int32, (n, n), 1)).astype(F32)


def _resident(shape):
    nd = len(shape)
    return pl.BlockSpec(shape, lambda *_: (0,) * nd, pipeline_mode=pl.Buffered(1))


def _cplx_mm(ar, ai, w):
    u, v = _mm(ar, w), _mm(ai, w)
    h = w.shape[1] // 2
    return u[:, :h] - v[:, h:], u[:, h:] + v[:, :h]


def _cplx_mm_left(fr, fi, xr, xi):
    rhs = jnp.concatenate([xr, xi], axis=1)
    p, q = _mm(fr, rhs), _mm(fi, rhs)
    h = xr.shape[1]
    return p[:, :h] - q[:, h:], p[:, h:] + q[:, :h]


def _ada_kernel(c_ref, w_ref, b_ref, o_ref):
    o_ref[0] = _mm(_silu(c_ref[...]), w_ref[0]) + b_ref[0]


def _ada_all(cond8, w_ada, b_ada):
    depth, d, nm = w_ada.shape
    tn = nm // 8
    return pl.pallas_call(
        _ada_kernel,
        grid=(depth, nm // tn),
        in_specs=[pl.BlockSpec((8, d), lambda l, j: (0, 0)),
                  pl.BlockSpec((1, d, tn), lambda l, j: (l, 0, j)),
                  pl.BlockSpec((1, 1, tn), lambda l, j: (l, 0, j))],
        out_specs=pl.BlockSpec((1, 8, tn), lambda l, j: (l, 0, j)),
        out_shape=jax.ShapeDtypeStruct((depth, 8, nm), F32),
        name="ada",
        compiler_params=_cparams("arbitrary", "arbitrary"),
    )(cond8, w_ada, b_ada.reshape(depth, 1, nm))


def _stream_dims(x, layout):
    if layout == "col":
        return x.shape[0], x.shape[1] * GRID_W, x.shape[2] // GRID_W
    return x.shape


COL_TILE = 2


def _tile_rows(n, layout, tm):
    return COL_TILE * (n // GRID_W) if layout == "col" else min(tm, n)


def _main_spec(n, d, layout, tm):
    if layout == "col":
        return pl.BlockSpec((None, n // GRID_W, COL_TILE * d), lambda b, i: (b, 0, i))
    return pl.BlockSpec((None, tm, d), lambda b, i: (b, i, 0))


def _load_tile(x_ref, d):
    cols = x_ref.shape[1] // d
    if cols == 1:
        return x_ref[...]
    return jnp.concatenate([x_ref[:, c * d:(c + 1) * d] for c in range(cols)], axis=0)


def _store_tile(o_ref, y, d):
    cols = o_ref.shape[1] // d
    rows = o_ref.shape[0]
    if cols == 1:
        o_ref[...] = y
    else:
        for c in range(cols):
            o_ref[:, c * d:(c + 1) * d] = y[c * rows:(c + 1) * rows]


def _halo_specs(n, d, layout, tm):
    if layout == "col":
        r8 = n // GRID_W // 8
        prev = pl.BlockSpec((None, 8, d), lambda b, i: (b, r8 - 1, jnp.maximum(COL_TILE * i - 1, 0)))
        nxt = pl.BlockSpec((None, 8, d), lambda b, i: (b, 0, jnp.minimum(COL_TILE * (i + 1), GRID_W - 1)))
    else:
        t8, n8 = tm // 8, n // 8
        prev = pl.BlockSpec((None, 8, d), lambda b, i: (b, jnp.maximum(i * t8 - 1, 0), 0))
        nxt = pl.BlockSpec((None, 8, d), lambda b, i: (b, jnp.minimum((i + 1) * t8, n8 - 1), 0))
    return prev, nxt


def _seq_spec(tm, cols):
    return pl.BlockSpec((None, tm, cols), lambda b, i: (b, i, 0))


def _chan_spec(rows, tm):
    return pl.BlockSpec((None, rows, tm // LANES, LANES), lambda b, i: (b, 0, i, 0))


def _lanes_of(ref, lead, pieces):
    return jnp.concatenate([ref[lead + (slice(None), q, slice(None))] for q in range(pieces)], axis=1)


def _ffn_kernel(x_ref, mod_ref, w1_ref, w2_ref, g_ref, b_ref, o_ref, *, mo, nk, in_cv, out_cv):
    d = w2_ref.shape[1]
    pieces = in_cv or out_cv
    if in_cv:
        x = jnp.concatenate([x_ref[:, c * d:(c + 1) * d] for c in range(GRID_W)], axis=0)
    elif pieces:
        x = jnp.concatenate([x_ref[:, c, :] for c in range(GRID_W)], axis=0)
    else:
        x = x_ref[...]
    shift, scale, gate = mod_ref[mo:mo + 1, :], mod_ref[mo + 1:mo + 2, :], mod_ref[mo + 2:mo + 3, :]
    h = (x * (1.0 + scale) + shift).astype(BF16)
    tf = D_FF // nk
    acc = jnp.zeros(x.shape, F32)
    for k in range(nk):
        gt = jnp.dot(h, w1_ref[:, k * tf:(k + 1) * tf], preferred_element_type=F32)
        up = jnp.dot(h, w1_ref[:, D_FF + k * tf:D_FF + (k + 1) * tf], preferred_element_type=F32)
        acc = acc + _mm(_silu(gt) * up, w2_ref[k * tf:(k + 1) * tf, :])
    y = _layer_norm(DN_ALPHA * x + (0.5 * gate) * acc, g_ref[...], b_ref[...])
    if out_cv:
        for c in range(GRID_W):
            o_ref[:, c * d:(c + 1) * d] = y[8 * c:8 * c + 8]
    elif pieces:
        for c in range(GRID_W):
            o_ref[:, c, :] = y[8 * c:8 * c + 8]
    else:
        o_ref[...] = y


def _ffn_step(x, mod, mo, w1, w2, g, b, in_cv=False, out_cv=False, tm=512):
    bsz, n, d = _stream_dims(x, "col" if in_cv else "row")
    rows = n // GRID_W
    if in_cv or out_cv:
        tm = 8 * GRID_W
        cv_spec = pl.BlockSpec((None, 8, GRID_W * d), lambda bb, i: (bb, i, 0))
        tok_spec = pl.BlockSpec((None, 8, GRID_W, d), lambda bb, i: (bb, i, 0, 0))
        x = x if in_cv else x.reshape(bsz, rows, GRID_W, d)
        in_spec, out_spec = (cv_spec if in_cv else tok_spec), (cv_spec if out_cv else tok_spec)
        out_dims = (bsz, rows, GRID_W * d) if out_cv else (bsz, rows, GRID_W, d)
    else:
        tm = min(tm, n)
        in_spec = out_spec = _seq_spec(tm, d)
        out_dims = (bsz, n, d)
    out = pl.pallas_call(
        functools.partial(_ffn_kernel, mo=mo, nk=1, in_cv=in_cv, out_cv=out_cv),
        grid=(bsz, n // tm),
        in_specs=[in_spec,
                  pl.BlockSpec((None, N_MOD, d), lambda bb, i: (bb, 0, 0)),
                  _resident(w1.shape), _resident(w2.shape),
                  _resident((1, d)), _resident((1, d))],
        out_specs=out_spec,
        out_shape=jax.ShapeDtypeStruct(out_dims, F32),
        name="ffn",
        compiler_params=_cparams("arbitrary", "arbitrary"),
    )(x, mod, w1, w2, g.reshape(1, d), b.reshape(1, d))
    return out if out_cv else out.reshape(bsz, n, d)


def _inproj_kernel(x_ref, xp_ref, xn_ref, mod_ref, wp_ref, wc_ref, cw_ref, cb_ref, *out_refs,
                   psplit, csplit, act, chan_major):
    i, last = pl.program_id(1), pl.num_programs(1) - 1
    shift, scale = mod_ref[3:4, :], mod_ref[4:5, :]
    x = _load_tile(x_ref, mod_ref.shape[1])
    tm = x.shape[0]
    h = x * (1.0 + scale) + shift
    hb = h.astype(BF16)
    o = 0
    for k, w in enumerate(psplit):
        out_refs[k][...] = jnp.dot(hb, wp_ref[:, o:o + w], preferred_element_type=F32).astype(out_refs[k].dtype)
        o += w
    hp = (xp_ref[...] * (1.0 + scale) + shift) * jnp.where(i > 0, 1.0, 0.0)
    hn = (xn_ref[...] * (1.0 + scale) + shift) * jnp.where(i < last, 1.0, 0.0)
    couts = out_refs[len(psplit):]
    if chan_major:
        lane = lax.broadcasted_iota(jnp.int32, (1, tm), 1)
        halo = jnp.concatenate([hp, hn, jnp.zeros((LANES - 16, hp.shape[1]), F32)], axis=0).astype(BF16)
        o = 0
        for k, w in enumerate(csplit):
            wt = wc_ref[o:o + w, :]
            p = _mm_nt(wt, hb)
            p_halo = _mm_nt(wt, halo)
            up = jnp.where(lane == 0, p_halo[:, 7:8], pltpu.roll(p, 1, 1))
            dn = jnp.where(lane == tm - 1, p_halo[:, 8:9], pltpu.roll(p, tm - 1, 1))
            cw = cw_ref[o:o + w, :]
            y = cw[:, 0:1] * up + cw[:, 1:2] * p + cw[:, 2:3] * dn + cb_ref[o:o + w, :]
            for q in range(tm // LANES):
                couts[k][:, q, :] = y[:, q * LANES:(q + 1) * LANES]
            o += w
    else:
        pc = _mm(jnp.concatenate([hp, h, hn], axis=0), wc_ref[...])
        up = pltpu.roll(pc, 1, 0)
        dn = pltpu.roll(pc, tm + 15, 0)
        y = (cw_ref[0:1, :] * up + cw_ref[1:2, :] * pc + cw_ref[2:3, :] * dn + cb_ref[...])[8:8 + tm]
        if act:
            y = _silu(y)
        o = 0
        for k, w in enumerate(csplit):
            couts[k][...] = y[:, o:o + w].astype(couts[k].dtype)
            o += w


def _inproj(xv, mod, wp, wc, cw, cb, psplit, csplit, act, layout, chan_major, tm):
    bsz, n, d = _stream_dims(xv, layout)
    tm = _tile_rows(n, layout, tm)
    prev, nxt = _halo_specs(n, d, layout, tm)
    pdt = [BF16 if w >= LANES else F32 for w in psplit]
    out_specs = [_seq_spec(tm, w) for w in psplit]
    out_shape = [jax.ShapeDtypeStruct((bsz, n, w), dt) for w, dt in zip(psplit, pdt)]
    if chan_major:
        out_specs += [_chan_spec(w, tm) for w in csplit]
        out_shape += [jax.ShapeDtypeStruct((bsz, w, n // LANES, LANES), F32) for w in csplit]
    else:
        out_specs += [_seq_spec(tm, w) for w in csplit]
        out_shape += [jax.ShapeDtypeStruct((bsz, n, w), BF16) for w in csplit]
    return pl.pallas_call(
        functools.partial(_inproj_kernel, psplit=tuple(psplit), csplit=tuple(csplit), act=act,
                          chan_major=chan_major),
        grid=(bsz, n // tm),
        in_specs=[_main_spec(n, d, layout, tm), prev, nxt,
                  pl.BlockSpec((None, N_MOD, d), lambda bb, i: (bb, 0, 0)),
                  _resident(wp.shape), _resident(wc.shape), _resident(cw.shape), _resident(cb.shape)],
        out_specs=out_specs,
        out_shape=out_shape,
        name="inproj_chan" if chan_major else "inproj",
        compiler_params=_cparams("arbitrary", "arbitrary"),
    )(xv, xv, xv, mod, wp, wc, cw, cb)


def _outproj_even_kernel(hy_ref, of_ref, ob_ref, g_ref, x_ref, mod_ref, w_ref, lg_ref, lb_ref, nw_ref, o_ref):
    tm = x_ref.shape[0]
    y = _mm_tn(_lanes_of(hy_ref, (), tm // LANES), w_ref[0:HALF, :])
    o = of_ref[...].astype(F32) + ob_ref[...].astype(F32)
    g = g_ref[...].astype(F32)
    heads = [_rms(o[:, h * GLA_DV:(h + 1) * GLA_DV]) * nw_ref[...] for h in range(GLA_HEADS)]
    y = y + _mm(_silu(g) * jnp.concatenate(heads, axis=1), w_ref[HALF:2 * HALF, :])
    o_ref[...] = _layer_norm(DN_ALPHA * x_ref[...] + mod_ref[5:6, :] * y, lg_ref[...], lb_ref[...])


def _outproj_odd_kernel(hf_ref, hb_ref, og_ref, yf_ref, yb_ref, xs_ref, z_ref, x_ref, mod_ref, w_ref,
                        lg_ref, lb_ref, mnw_ref, dsk_ref, snw_ref, o_ref):
    h = hf_ref[...].astype(F32) + hb_ref[...].astype(F32)
    heads = []
    for k in range(ML_HEADS):
        hk = h[:, k * ML_DV:(k + 1) * ML_DV]
        hc = hk - jnp.mean(hk, axis=-1, keepdims=True)
        heads.append(hc * lax.rsqrt(jnp.mean(hc * hc, axis=-1, keepdims=True) + NORM_EPS) * mnw_ref[...])
    m = jax.nn.sigmoid(og_ref[...].astype(F32)) * jnp.concatenate(heads, axis=1)
    s = yf_ref[...].astype(F32) + yb_ref[...].astype(F32) + dsk_ref[...] * xs_ref[...].astype(F32)
    s = _rms(s * _silu(z_ref[...].astype(F32))) * snw_ref[...]
    y = _mm(m, w_ref[0:HALF, :]) + _mm(s, w_ref[HALF:2 * HALF, :])
    d = mod_ref.shape[1]
    x = _load_tile(x_ref, d)
    _store_tile(o_ref, _layer_norm(DN_ALPHA * x + mod_ref[5:6, :] * y, lg_ref[...], lb_ref[...]), d)


def _outproj(kern, parts, part_specs, xv, mod, w, g, b, consts, layout, tm):
    bsz, n, d = _stream_dims(xv, layout)
    tm = _tile_rows(n, layout, tm)
    consts = [g.reshape(1, d), b.reshape(1, d)] + consts
    return pl.pallas_call(
        kern,
        grid=(bsz, n // tm),
        in_specs=[s(tm) for s in part_specs] + [_main_spec(n, d, layout, tm),
                  pl.BlockSpec((None, N_MOD, d), lambda bb, i: (bb, 0, 0)), _resident(w.shape)]
        + [_resident(c.shape) for c in consts],
        out_specs=_main_spec(n, d, layout, tm),
        out_shape=jax.ShapeDtypeStruct(xv.shape, F32),
        name=kern.__name__.strip("_"),
        compiler_params=_cparams("arbitrary", "arbitrary"),
    )(*parts, xv, mod, w, *consts)


def _scan_call(kernel, seqs, consts, states, out_cols, tb=512):
    bsz, n, _ = seqs[0].shape
    tb = min(tb, n)
    nblk = n // tb

    def fmap(i):
        return (0, i, 0)

    def bmap(i):
        return (0, nblk - 1 - i, 0)

    in_specs = [pl.BlockSpec((bsz, tb, s.shape[-1]), fmap) for s in seqs]
    in_specs += [pl.BlockSpec((bsz, tb, s.shape[-1]), bmap) for s in seqs]
    in_specs += [_resident(c.shape) for c in consts]
    st_specs = [pl.BlockSpec(s.shape, lambda i, nd=s.ndim: (0,) * nd) for s in states]
    outs = pl.pallas_call(
        functools.partial(kernel, nchunk=tb // CHUNK),
        grid=(nblk,),
        in_specs=in_specs + st_specs,
        out_specs=[pl.BlockSpec((bsz, tb, out_cols), fmap), pl.BlockSpec((bsz, tb, out_cols), bmap)] + st_specs,
        out_shape=[jax.ShapeDtypeStruct((bsz, n, out_cols), BF16)] * 2
        + [jax.ShapeDtypeStruct(s.shape, F32) for s in states],
        scratch_shapes=[pltpu.VMEM(s.shape, F32) for s in states],
        name=kernel.__name__.strip("_"),
        compiler_params=_cparams("arbitrary"),
    )(*seqs, *seqs, *consts, *states)
    return outs[0], outs[1], tuple(outs[2:])


def _chunk_rows(ci, nchunk, reverse):
    c = (nchunk - 1 - ci) if reverse else ci
    return pl.ds(pl.multiple_of(c * CHUNK, CHUNK), CHUNK)


def _chains(ci, nchunk, bsz, unroll):
    return [(d, b, _chunk_rows(ci * unroll + u, nchunk, d == 1))
            for u in range(unroll) for d in range(2) for b in range(bsz)]


def _interleave(gens):
    results = [None] * len(gens)
    live = list(enumerate(gens))
    while live:
        still = []
        for idx, g in live:
            try:
                next(g)
                still.append((idx, g))
            except StopIteration as stop:
                results[idx] = stop.value
        live = still
    return results


def _scan_states(i, s0_refs, sout_refs, scrs, body, trips):
    @pl.when(i == 0)
    def _():
        for s0, scr in zip(s0_refs, scrs):
            scr[...] = s0[...]

    lax.fori_loop(0, trips, body, 0)

    @pl.when(i == pl.num_programs(0) - 1)
    def _():
        for so, scr in zip(sout_refs, scrs):
            so[...] = scr[...]


def _gla_chunk(qk, v, a, aw, ab, state, tri, head):
    heads = range(GLA_HEADS)
    q = qk[:, 0:GLA_QK].astype(F32) * (GLA_DK ** -0.5)
    k = qk[:, GLA_QK:2 * GLA_QK].astype(F32)
    tri, tri3 = tri
    lr = _mm(a, aw)
    yield
    log_a = _log_sigmoid(lr + ab) * (1.0 / GLA_TAU)
    b = _sel_rows(tri3, log_a)
    yield
    b_last = jnp.sum(log_a, axis=0, keepdims=True)
    q_dec = q * jnp.exp(b)
    k_inv = k * jnp.exp(-b)
    k_dec = k * jnp.exp(b_last - b)
    vs = [v[:, h * GLA_DV:(h + 1) * GLA_DV] for h in heads]
    q4 = jnp.concatenate([jnp.where(head == h, q_dec, 0.0) for h in heads], axis=0).astype(BF16)
    att4 = _mm_nt(q4, k_inv)
    kv = _mm_tn(jnp.concatenate(vs, axis=0),
                jnp.concatenate([jnp.where(head == h, k_dec, 0.0) for h in heads], axis=0))
    yield
    s_old = state[0]
    inter4 = _mm_nt(q4, s_old)
    state[0] = s_old * jnp.exp(b_last) + kv
    outs = [_mm(jnp.where(tri, att4[h * CHUNK:(h + 1) * CHUNK], 0.0), vs[h]) for h in heads]
    yield
    return [outs[h] + inter4[h * CHUNK:(h + 1) * CHUNK] for h in heads]


def _gla_kernel(qkf_ref, vf_ref, af_ref, qkb_ref, vb_ref, ab_ref, aw_ref, abias_ref, s0_ref,
                of_ref, ob_ref, sout_ref, s_scr, *, nchunk):
    head = lax.broadcasted_iota(jnp.int32, (1, GLA_QK), 1) // GLA_DK
    tris = tuple((_tri(rev), _tri3(rev)) for rev in (False, True))
    dirs = ((qkf_ref, vf_ref, af_ref, of_ref), (qkb_ref, vb_ref, ab_ref, ob_ref))
    bsz, unroll = qkf_ref.shape[0], 4

    def body(ci, carry):
        chains = _chains(ci, nchunk, bsz, unroll)
        states = {(d, b): [s_scr[d, b]] for d in range(2) for b in range(bsz)}
        done = _interleave([_gla_chunk(dirs[d][0][b, rows, :], dirs[d][1][b, rows, :], dirs[d][2][b, rows, :],
                                       aw_ref[d], abias_ref[d], states[d, b], tris[d], head)
                            for d, b, rows in chains])
        for (d, b, rows), outs in zip(chains, done):
            for h, oh in enumerate(outs):
                dirs[d][3][b, rows, h * GLA_DV:(h + 1) * GLA_DV] = oh.astype(BF16)
        for (d, b), st in states.items():
            s_scr[d, b] = st[0]
        return carry

    _scan_states(pl.program_id(0), [s0_ref], [sout_ref], [s_scr], body, nchunk // unroll)


def _gla_mixer(ctx_p, lat_p, a_w, a_b):
    bsz = ctx_p[0].shape[0]
    aw = jnp.zeros((2, SMALL, GLA_QK), F32)
    for d in range(2):
        aw = aw.at[d, d * GLA_RANK:(d + 1) * GLA_RANK].set(a_w[d])
    ab = a_b.reshape(2, 1, GLA_QK)
    st = (jnp.zeros((2, bsz, GLA_DV, GLA_QK), F32),)
    cf, cb, st = _scan_call(_gla_kernel, list(ctx_p), [aw, ab], st, HALF)
    lf, lb, _ = _scan_call(_gla_kernel, list(lat_p), [aw, ab], st, HALF)
    return (cf, cb), (lf, lb)


def _mlstm_chunk(qk, v, gates, state, tri, head, eye, shift, spread, d):
    q = qk[:, 0:ML_QK].astype(F32)
    k = qk[:, ML_QK:2 * ML_QK].astype(F32) * (ML_DQK ** -0.5)
    tri, tri3 = tri
    heads = range(ML_HEADS)
    trif = tri.astype(F32)
    gates_t = _mm_nt_hi(eye, gates)
    lf = _log_sigmoid(gates)
    b_col = _mm_hi(trif, lf)
    yield
    b_row = _mm_nt_hi(_log_sigmoid(gates_t), trif)
    b_tot = jnp.sum(lf, axis=0, keepdims=True)
    vs = [v[:, h * ML_DV:(h + 1) * ML_DV] for h in heads]
    qms = [jnp.where(head == h, q, 0.0) for h in heads]
    s_raw = [_mm_nt(qm, k) for qm in qms]
    c_old, n_all, m_all = state
    n_old = n_all[0:1, :]
    inter = [_mm_nt(qm, c_old) for qm in qms]
    m_old = m_all[0:1, :]
    a = b_tot - b_col + _sel_cols(gates, shift)
    a_max = jnp.max(a, axis=0, keepdims=True)
    m_new = jnp.maximum(b_tot + m_old, a_max)
    w_old, w_new = jnp.exp(b_tot + m_old - m_new), jnp.exp(a_max - m_new)
    kw_all = k * _mm(jnp.exp(a - a_max) * w_new, spread)
    keep = _sel_cols(jnp.broadcast_to(w_old, (8, SMALL)), spread)[0:1, :]
    n_add = jnp.sum(kw_all, axis=0, keepdims=True)
    c_add = _mm_tn(jnp.concatenate(vs, axis=0),
                   jnp.concatenate([jnp.where(head == h, kw_all, 0.0) for h in heads], axis=0))
    state[:] = [c_old * keep + c_add, jnp.broadcast_to(n_old * keep + n_add, n_all.shape),
                jnp.broadcast_to(m_new, m_all.shape)]
    yield
    ms, es, ss = [], [], []
    for h in heads:
        ic, fc = d * 8 + h, d * 8 + 4 + h
        bc = b_col[:, fc:fc + 1]
        r_vis = jnp.where(tri, gates_t[ic:ic + 1, :] - b_row[fc:fc + 1, :], -jnp.inf)
        g = bc + m_all[0:1, fc:fc + 1]
        m = jnp.maximum(g, bc + jnp.max(r_vis, axis=1, keepdims=True))
        ms.append(m)
        es.append(jnp.exp(g - m))
        ss.append(s_raw[h] * jnp.exp(r_vis + (bc - m)))
    nums = [_mm(ss[h], vs[h]) for h in heads]
    yield
    outs = []
    for h in heads:
        num = nums[h] + es[h] * inter[h]
        den = jnp.sum(ss[h], axis=1, keepdims=True) + es[h] * jnp.sum(qms[h] * n_old, axis=1, keepdims=True)
        outs.append(num / jnp.maximum(jnp.abs(den), jnp.exp(-ms[h])))
    return outs


def _mlstm_kernel(qkf_ref, vf_ref, gf_ref, qkb_ref, vb_ref, gb_ref, bias_ref, c0_ref, n0_ref, m0_ref,
                  hf_ref, hb_ref, cout_ref, nout_ref, mout_ref, c_scr, n_scr, m_scr, *, nchunk):
    head = lax.broadcasted_iota(jnp.int32, (1, ML_QK), 1) // ML_DQK
    col_r = lax.broadcasted_iota(jnp.int32, (SMALL, SMALL), 0)
    col_c = lax.broadcasted_iota(jnp.int32, (SMALL, SMALL), 1)
    shift = ((col_c == col_r + ML_HEADS) & (col_r % 8 < ML_HEADS)).astype(F32)
    row = lax.broadcasted_iota(jnp.int32, (SMALL, ML_QK), 0)
    lane_head = lax.broadcasted_iota(jnp.int32, (SMALL, ML_QK), 1) // ML_DQK
    spreads = [(row == d * 8 + ML_HEADS + lane_head).astype(F32) for d in range(2)]
    eye = _eye(SMALL)
    tris = tuple((_tri(rev), _tri3(rev)) for rev in (False, True))
    dirs = ((qkf_ref, vf_ref, gf_ref, hf_ref), (qkb_ref, vb_ref, gb_ref, hb_ref))
    bsz, unroll = qkf_ref.shape[0], 2

    def body(ci, carry):
        chains = _chains(ci, nchunk, bsz, unroll)
        states = {(d, b): [c_scr[d, b], n_scr[d, b], m_scr[d, b]] for d in range(2) for b in range(bsz)}
        done = _interleave([_mlstm_chunk(
            dirs[d][0][b, rows, :], dirs[d][1][b, rows, :], dirs[d][2][b, rows, :] + bias_ref[...],
            states[d, b], tris[d], head, eye, shift, spreads[d], d) for d, b, rows in chains])
        for (d, b, rows), outs in zip(chains, done):
            for h, oh in enumerate(outs):
                dirs[d][3][b, rows, h * ML_DV:(h + 1) * ML_DV] = oh.astype(BF16)
        for (d, b), st in states.items():
            c_scr[d, b], n_scr[d, b], m_scr[d, b] = st
        return carry

    _scan_states(pl.program_id(0), [c0_ref, n0_ref, m0_ref], [cout_ref, nout_ref, mout_ref],
                 [c_scr, n_scr, m_scr], body, nchunk // unroll)


def _mlstm_mixer(ctx_p, lat_p, gate_b):
    bsz = ctx_p[0].shape[0]
    st = (jnp.zeros((2, bsz, ML_DV, ML_QK), F32), jnp.zeros((2, bsz, 8, ML_QK), F32),
          jnp.zeros((2, bsz, 8, SMALL), F32))
    gb = jnp.zeros((1, SMALL), F32).at[0, :16].set(gate_b.reshape(16))
    cf, cb, st = _scan_call(_mlstm_kernel, list(ctx_p), [gb], st, HALF)
    lf, lb, _ = _scan_call(_mlstm_kernel, list(lat_p), [gb], st, HALF)
    return (cf, cb), (lf, lb)


def _ssd_chunk(x, bm, cm, raw, alr, alc, expand, state, tri, sub, eye, d):
    x, bm, cm = x.astype(F32), bm.astype(F32), cm.astype(F32)
    gw = SSD_HPG * SSD_HEADDIM
    tri, tri3 = tri
    groups = range(SSD_GROUPS)
    lns = [slice(g * gw, (g + 1) * gw) for g in groups]
    cms = [cm[:, g * SSD_STATE:(g + 1) * SSD_STATE] for g in groups]
    bms = [bm[:, g * SSD_STATE:(g + 1) * SSD_STATE] for g in groups]
    raw_t = _sel_nt(eye, raw, False)
    dt = _softplus(raw)
    da = dt * (-jnp.exp(alr))
    cum = _sel_rows(tri3, da)
    cbs = [_mm_nt(cms[g], bms[g]) for g in groups]
    yield
    dt_t = _softplus(raw_t)
    cum_t = _sel_nt(dt_t * (-jnp.exp(alc)), tri, True)
    cl = jnp.sum(da, axis=0, keepdims=True)
    e_cum = _sel_cols(jnp.exp(cum), expand)
    wgt = _sel_cols(jnp.exp(cl - cum) * dt, expand)
    dec = _sel_cols(jnp.broadcast_to(jnp.exp(cl), (8, SMALL)), expand)[0:1, :]
    yield
    xw = x * wgt
    kvs = [_mm_tn(bms[g], xw[:, lns[g]]) for g in groups]
    diags = []
    for g in groups:
        xg = x[:, lns[g]]
        mixes = []
        for j in range(SSD_HPG):
            col = 16 + d * 8 + g * SSD_HPG + j
            seg = jnp.exp(jnp.where(tri, cum[:, col:col + 1] - cum_t[col:col + 1, :], -jnp.inf))
            mixes.append(cbs[g] * seg * dt_t[col:col + 1, :])
        diags.append(_mm(jnp.concatenate(mixes, axis=1),
                         jnp.concatenate([jnp.where(sub == j, xg, 0.0) for j in range(SSD_HPG)], axis=0)))
    yield
    offs = [_mm(cms[g], state[g]) for g in groups]
    state[:] = [state[g] * dec[:, lns[g]] + kvs[g] for g in groups]
    yield
    return [offs[g] * e_cum[:, lns[g]] + diags[g] for g in groups]


def _ssd_kernel(xf_ref, bmf_ref, cmf_ref, rf_ref, xb_ref, bmb_ref, cmb_ref, rb_ref,
                dtb_ref, alr_ref, alc_ref, ex_ref, s0_ref, yf_ref, yb_ref, sout_ref, s_scr, *, nchunk):
    gw = SSD_HPG * SSD_HEADDIM
    sub = lax.broadcasted_iota(jnp.int32, (1, gw), 1) // SSD_HEADDIM
    eye = _eye(SMALL)
    tris = tuple((_tri(rev), _tri3(rev)) for rev in (False, True))
    dirs = ((xf_ref, bmf_ref, cmf_ref, rf_ref, yf_ref), (xb_ref, bmb_ref, cmb_ref, rb_ref, yb_ref))

    bsz, unroll = xf_ref.shape[0], 4

    def body(ci, carry):
        chains = _chains(ci, nchunk, bsz, unroll)
        states = {(d, b): [s_scr[d, b, :, g * gw:(g + 1) * gw] for g in range(SSD_GROUPS)]
                  for d in range(2) for b in range(bsz)}
        done = _interleave([_ssd_chunk(
            dirs[d][0][b, rows, :], dirs[d][1][b, rows, :], dirs[d][2][b, rows, :],
            dirs[d][3][b, rows, :] + dtb_ref[...], alr_ref[...], alc_ref[...],
            ex_ref[d], states[d, b], tris[d], sub, eye, d) for d, b, rows in chains])
        for (d, b, rows), ys in zip(chains, done):
            for g in range(SSD_GROUPS):
                dirs[d][4][b, rows, g * gw:(g + 1) * gw] = ys[g].astype(BF16)
        for (d, b), st in states.items():
            for g in range(SSD_GROUPS):
                s_scr[d, b, :, g * gw:(g + 1) * gw] = st[g]
        return carry

    _scan_states(pl.program_id(0), [s0_ref], [sout_ref], [s_scr], body, nchunk // unroll)


def _ssd_mixer(ctx_p, lat_p, dt_bias, a_log):
    bsz = ctx_p[0].shape[0]
    st = (jnp.zeros((2, bsz, SSD_STATE, HALF), F32),)
    dtb = jnp.zeros((1, SMALL), F32).at[0, 16:].set(dt_bias.reshape(16))
    al = jnp.zeros((1, SMALL), F32).at[0, 16:].set(a_log.reshape(16).astype(F32))
    lane_head = np.arange(HALF) // SSD_HEADDIM
    ex = jnp.asarray(np.stack([(np.arange(SMALL)[:, None] == (16 + d * 8 + lane_head)[None, :])
                               for d in range(2)]).astype(np.float32))
    consts = [dtb, al, al.reshape(SMALL, 1), ex]
    cf, cb, st = _scan_call(_ssd_kernel, list(ctx_p), consts, st, HALF)
    lf, lb, _ = _scan_call(_ssd_kernel, list(lat_p), consts, st, HALF)
    return (cf, cb), (lf, lb)


def _hy_filter_kernel(f_ref, w1t_ref, w1c_ref, w1s_ref, b1_ref, w2_ref, b2_ref, fq1_ref, fq2_ref, w3s_ref, w3b_ref,
                      dl_ref, o_ref, s_ref, *, n, npiece):
    side, jt = pl.program_id(0), pl.program_id(1)
    tnorm = 1.0 / max(n - 1, 1)
    first = (side == 0) & (jt == 0)
    width = npiece * LANES

    @pl.when(first)
    def _():
        s_ref[...] = jnp.zeros(s_ref.shape, F32)

    def mlp(pos):
        ang = pos * f_ref[...] * (2.0 * math.pi / n)
        pre = (pos * tnorm) * w1t_ref[...] + _mm_hi(w1c_ref[...], jnp.cos(ang)) \
            + _mm_hi(w1s_ref[...], -jnp.sin(ang)) + b1_ref[...]
        hid = jnp.sin(fq1_ref[...] * pre)
        return jnp.sin(fq2_ref[...] * (_mm_hi(w2_ref[...], hid) + b2_ref[...]))

    h00 = _mm(w3b_ref[...], mlp(jnp.zeros((1, LANES), F32)))[:, 0:1]
    tau = jt * width + lax.broadcasted_iota(jnp.int32, (1, width), 1)
    pos = (side * n + (1 - 2 * side) * tau).astype(F32)
    ht = _mm(w3s_ref[...], mlp(pos))
    ht = ht * jnp.exp(-(pos * tnorm) * dl_ref[...])
    ht = jnp.where((tau == 0) & (side == 1), 0.0, ht)
    s_ref[...] += jnp.sum(jnp.abs(ht), axis=1, keepdims=True) + jnp.where(first, jnp.abs(h00), 0.0)
    ht = ht + jnp.where((tau == 0) & (side == 0), h00, 0.0)
    for q in range(npiece):
        o_ref[q] = ht[:, q * LANES:(q + 1) * LANES]


def _hyena_filter(n, w1, b1, w2, b2, w3, freq):
    kk = n // LANES
    npiece = min(8, kk)
    ff = w1.shape[1]
    pad = LANES - ff
    bands = jnp.linspace(1e-4, HY_BANDS - 1, HY_BANDS, dtype=F32).reshape(HY_BANDS, 1)
    deltas = jnp.abs(jnp.linspace(math.log(HY_TARGET) / HY_SLOW_DECAY, math.log(HY_TARGET) / HY_FAST_DECAY,
                                  HY_CH, dtype=F32))
    dl = jnp.tile(deltas, HY_ORDER).reshape(HY_ROWS, 1)
    w3t = jnp.transpose(w3.reshape(ff, HY_ORDER, 2, HY_CH), (2, 1, 3, 0)).reshape(2, HY_ROWS, ff)
    w3t = jnp.pad(w3t, ((0, 0), (0, 0), (0, pad)))
    args = (bands, w1[0].reshape(ff, 1), jnp.transpose(w1[1:1 + HY_BANDS]), jnp.transpose(w1[1 + HY_BANDS:]),
            b1.reshape(ff, 1), jnp.pad(jnp.transpose(w2), ((0, pad), (0, 0))),
            jnp.pad(b2.reshape(ff, 1), ((0, pad), (0, 0))),
            freq[0].reshape(ff, 1), jnp.pad(freq[1].reshape(ff, 1), ((0, pad), (0, 0))))
    nt = kk // npiece
    return pl.pallas_call(
        functools.partial(_hy_filter_kernel, n=n, npiece=npiece),
        grid=(2, nt),
        in_specs=[_resident(a.shape) for a in args]
        + [pl.BlockSpec((None, HY_ROWS, LANES), lambda sd, j: (sd, 0, 0)),
           pl.BlockSpec((None, HY_ROWS, LANES), lambda sd, j: (1, 0, 0)), _resident(dl.shape)],
        out_specs=[pl.BlockSpec((npiece, HY_ROWS, LANES), lambda sd, j: (sd * nt + j, 0, 0)),
                   pl.BlockSpec((HY_ROWS, 1), lambda sd, j: (0, 0))],
        out_shape=[jax.ShapeDtypeStruct((2 * kk, HY_ROWS, LANES), F32),
                   jax.ShapeDtypeStruct((HY_ROWS, 1), F32)],
        name="hy_filter",
        compiler_params=_cparams("arbitrary", "arbitrary"),
    )(*args, w3t, w3t, dl)


def _dft_tables(n):
    big = 2 * n
    n1 = big // LANES
    k = np.arange(n1)

    def cis(num, den):
        ang = 2.0 * np.pi * (num % den) / den
        return np.stack([np.cos(ang), -np.sin(ang)]).astype(np.float32)

    f1 = cis(k[:, None] * k[None, :], n1)
    k2 = np.arange(LANES)
    f2 = cis(k2[:, None] * k2[None, :], LANES)
    tw = cis(k[:, None] * k2[None, :], big)
    w2 = np.concatenate([f2[0], f2[1]], axis=1)
    w2c = np.concatenate([f2[0], -f2[1]], axis=1)
    ginv = np.stack([f1[0, :n1 // 2, :], -f1[1, :n1 // 2, :]]) / big
    return tuple(jnp.asarray(a) for a in (f1, tw, w2, w2c, ginv))


HY_CB = 8
HY_GRP = 4


def _hy_spec_kernel(f_ref, s_ref, f1_ref, tw_ref, w2_ref, o_ref):
    f1r, f1i = f1_ref[0], f1_ref[1]
    twr, twi = tw_ref[0], tw_ref[1]
    n1 = f1r.shape[0]
    for c0 in range(0, HY_CB, HY_GRP):
        trs, tis = [], []
        for ca in range(c0, c0 + HY_GRP, 2):
            rhs = jnp.concatenate([f_ref[:, ca, :], f_ref[:, ca + 1, :]], axis=1)
            pr, pi = _mm(f1r, rhs), _mm(f1i, rhs)
            for e in range(2):
                ar, ai = pr[:, e * LANES:(e + 1) * LANES], pi[:, e * LANES:(e + 1) * LANES]
                trs.append(ar * twr - ai * twi)
                tis.append(ar * twi + ai * twr)
        xr, xi = _cplx_mm(jnp.concatenate(trs, axis=0), jnp.concatenate(tis, axis=0), w2_ref[...])
        for e in range(HY_GRP):
            inv = 1.0 / s_ref[c0 + e]
            o_ref[0, c0 + e] = xr[e * n1:(e + 1) * n1] * inv
            o_ref[1, c0 + e] = xi[e * n1:(e + 1) * n1] * inv


def _hy_spectrum(filt, norm, f1, tw, w2):
    n1, rows, _ = filt.shape
    return pl.pallas_call(
        _hy_spec_kernel,
        grid=(rows // HY_CB,),
        in_specs=[pl.BlockSpec((n1, HY_CB, LANES), lambda j: (0, j, 0)),
                  pl.BlockSpec((HY_CB, 1, 1), lambda j: (j, 0, 0)),
                  _resident(f1.shape), _resident(tw.shape), _resident(w2.shape)],
        out_specs=pl.BlockSpec((2, HY_CB, n1, LANES), lambda j: (0, j, 0, 0)),
        out_shape=jax.ShapeDtypeStruct((2, rows, n1, LANES), F32),
        name="hy_spectrum",
        compiler_params=_cparams("arbitrary"),
    )(filt, norm.reshape(rows, 1, 1), f1, tw, w2)


def _hy_conv_kernel(v_ref, x_ref, hf_ref, bias_ref, f1_ref, g_ref, tw_ref, w2_ref, w2c_ref, o_ref):
    f1r, f1i = f1_ref[0], f1_ref[1]
    gr, gi = g_ref[0], g_ref[1]
    twr, twi = tw_ref[0], tw_ref[1]
    n1 = f1r.shape[0]
    for c0 in range(0, HY_CB, HY_GRP):
        vs = [(v_ref[0, c], v_ref[1, c]) for c in range(c0, c0 + HY_GRP)]
        trs, tis = [], []
        for v0, v1 in vs:
            ar, ai = _cplx_mm_left(f1r, f1i, v0, v1)
            trs.append(ar * twr - ai * twi)
            tis.append(ar * twi + ai * twr)
        xr, xi = _cplx_mm(jnp.concatenate(trs, axis=0), jnp.concatenate(tis, axis=0), w2_ref[...])
        hr = hf_ref[0, c0:c0 + HY_GRP].reshape(HY_GRP * n1, LANES)
        hi = hf_ref[1, c0:c0 + HY_GRP].reshape(HY_GRP * n1, LANES)
        zr, zi = _cplx_mm(xr * hr - xi * hi, xr * hi + xi * hr, w2c_ref[...])
        for e, (v0, v1) in enumerate(vs):
            zre, zie = zr[e * n1:(e + 1) * n1], zi[e * n1:(e + 1) * n1]
            cr, ci = _cplx_mm_left(gr, gi, zre * twr + zie * twi, zie * twr - zre * twi)
            bz = bias_ref[c0 + e]
            o_ref[0, c0 + e] = x_ref[0, c0 + e] * (cr + bz * v0)
            o_ref[1, c0 + e] = x_ref[1, c0 + e] * (ci + bz * v1)


def _hy_conv(v, xg, hf, bias3, order, tabs):
    f1, tw, w2, w2c, ginv = tabs
    bsz, ch, k1, _ = v.shape
    n1 = 2 * k1
    nb = ch // HY_CB
    f1h = f1[:, :, :k1]
    dat = pl.BlockSpec((2, HY_CB, k1, LANES), lambda j: (0, j, 0, 0))
    return pl.pallas_call(
        _hy_conv_kernel,
        grid=(nb,),
        in_specs=[dat, dat,
                  pl.BlockSpec((2, HY_CB, n1, LANES), lambda j: (0, order * nb + j, 0, 0)),
                  pl.BlockSpec((HY_CB, 1, 1), lambda j: (order * nb + j, 0, 0)),
                  _resident(f1h.shape), _resident(ginv.shape), _resident(tw.shape),
                  _resident(w2.shape), _resident(w2c.shape)],
        out_specs=dat,
        out_shape=jax.ShapeDtypeStruct(v.shape, F32),
        name="hy_conv",
        compiler_params=_cparams("arbitrary"),
    )(v, xg, hf, bias3, f1h, ginv, tw, w2, w2c)


def _hy_ctx_kernel(v_ref, x1_ref, x2_ref, filt_ref, s_ref, bias_ref, f_ref, o_ref, *, n):
    k, big = n // LANES, 2 * n
    fr, fi = f_ref[0], f_ref[1]
    c0, c1 = _lanes_of(v_ref, (0,), k), _lanes_of(v_ref, (1,), k)
    for o, x_ref in enumerate((x1_ref, x2_ref)):
        rows = slice(o * HY_CH, (o + 1) * HY_CH)
        filt = jnp.concatenate([filt_ref[q, rows, :] for q in range(2 * k)], axis=1)
        inv = 1.0 / s_ref[rows, :]
        hr, hi = _mm(filt, fr) * inv, _mm(filt, fi) * inv
        xr = _mm(c0, fr[:n]) - _mm(c1, fi[:n])
        xi = _mm(c0, fi[:n]) + _mm(c1, fr[:n])
        yr, yi = xr * hr - xi * hi, xr * hi + xi * hr
        gr, gi = fr[:, :n] * (1.0 / big), fi[:, :n] * (-1.0 / big)
        cr = _mm(yr, gr) - _mm(yi, gi)
        ci = _mm(yr, gi) + _mm(yi, gr)
        b = bias_ref[rows, :]
        c0 = _lanes_of(x_ref, (0,), k) * (cr + b * c0)
        c1 = _lanes_of(x_ref, (1,), k) * (ci + b * c1)
    for q in range(k):
        o_ref[0, :, q, :] = c0[:, q * LANES:(q + 1) * LANES]
        o_ref[1, :, q, :] = c1[:, q * LANES:(q + 1) * LANES]


def _hyena_ctx(v, x1, x2, filt, norm, bias):
    bsz, ch, k, _ = v.shape
    n = k * LANES
    big = 2 * n
    idx = np.arange(big)
    ang = 2.0 * np.pi * ((idx[:, None] * idx[None, :]) % big) / big
    f = jnp.asarray(np.stack([np.cos(ang), -np.sin(ang)]).astype(np.float32))
    args = (v, x1, x2, filt, norm, bias.reshape(HY_ROWS, 1), f)
    return pl.pallas_call(
        functools.partial(_hy_ctx_kernel, n=n),
        grid=(1,),
        in_specs=[_resident(a.shape) for a in args],
        out_specs=pl.BlockSpec(v.shape, lambda j: (0, 0, 0, 0)),
        out_shape=jax.ShapeDtypeStruct(v.shape, F32),
        name="hy_ctx",
        compiler_params=_cparams("arbitrary"),
    )(*args)


def _hyena_lat(v, x1, x2, filt, norm, bias):
    assert v.shape[0] == 2
    n = v.shape[2] * LANES
    tabs = _dft_tables(n)
    hf = _hy_spectrum(filt, norm, tabs[0], tabs[1], tabs[2])
    bias3 = bias.reshape(HY_ROWS, 1, 1)
    z = _hy_conv(v, x1, hf, bias3, 0, tabs)
    return _hy_conv(z, x2, hf, bias3, 1, tabs)


def _even_layer_mix(xc, xl, mod_c, mod_l, w_in, w_out, hy_p, gla_p, ln_g, ln_b, need_ctx):
    conv_w, conv_b, filt_params, bias = hy_p
    a_w, a_b, gla_norm = gla_p
    wp = w_in[:, HY_COLS:].astype(BF16)
    wc = jnp.transpose(w_in[:, :HY_COLS]).astype(BF16)
    cw, cb = jnp.transpose(conv_w), conv_b.reshape(HY_COLS, 1)
    psplit = (2 * GLA_QK, HALF, HALF, SMALL)
    csplit = (HY_CH, HY_CH, HY_CH)
    pc = _inproj(xc, mod_c, wp, wc, cw, cb, psplit, csplit, False, "row", True, 1024)
    pl_ = _inproj(xl, mod_l, wp, wc, cw, cb, psplit, csplit, False, "row", True, 1024)
    g_c, g_l = _gla_mixer((pc[0], pc[1], pc[3]), (pl_[0], pl_[1], pl_[3]), a_w, a_b)
    filt, norm = _hyena_filter(xl.shape[1], *filt_params)
    h_l = _hyena_lat(pl_[4], pl_[5], pl_[6], filt, norm, bias)
    nw = [gla_norm.reshape(1, GLA_DV)]
    specs = [lambda tm: _chan_spec(HY_CH, tm)] + [lambda tm: _seq_spec(tm, HALF)] * 3
    xl = _outproj(_outproj_even_kernel, [h_l, g_l[0], g_l[1], pl_[2]], specs, xl, mod_l, w_out,
                  ln_g, ln_b, nw, "row", 1024)
    if need_ctx:
        filt_c, norm_c = _hyena_filter(xc.shape[1], *filt_params)
        h_c = _hyena_ctx(pc[4], pc[5], pc[6], filt_c, norm_c, bias)
        xc = _outproj(_outproj_even_kernel, [h_c, g_c[0], g_c[1], pc[2]], specs, xc, mod_c, w_out,
                      ln_g, ln_b, nw, "row", 1024)
    return xc, xl


def _odd_layer_mix(xc, xl, mod_c, mod_l, w_in, w_out, ml_p, ssd_p, ln_g, ln_b, need_ctx):
    gate_b, ml_norm = ml_p
    conv_w, conv_b, dt_bias, a_log, d_skip, ssd_norm = ssd_p
    ml_cols = 2 * ML_QK + 2 * HALF + 4 * ML_HEADS
    o0 = 2 * ML_QK + 2 * HALF
    s0 = ml_cols
    wp = jnp.concatenate([w_in[:, :o0], w_in[:, s0:s0 + HALF], w_in[:, o0:ml_cols],
                          w_in[:, s0 + HALF + SSD_CONV_DIM:]], axis=1).astype(BF16)
    wc = w_in[:, s0 + HALF:s0 + HALF + SSD_CONV_DIM].astype(BF16)
    cb = conv_b.reshape(1, SSD_CONV_DIM)
    psplit = (2 * ML_QK, HALF, HALF, HALF, SMALL)
    csplit = (HALF, SSD_GN, SSD_GN)
    pc = _inproj(xc, mod_c, wp, wc, conv_w, cb, psplit, csplit, True, "row", False, 512)
    pl_ = _inproj(xl, mod_l, wp, wc, conv_w, cb, psplit, csplit, True, "col", False, 512)
    m_c, m_l = _mlstm_mixer((pc[0], pc[1], pc[4]), (pl_[0], pl_[1], pl_[4]), gate_b)
    s_c, s_l = _ssd_mixer((pc[5], pc[6], pc[7], pc[4]), (pl_[5], pl_[6], pl_[7], pl_[4]), dt_bias, a_log)
    consts = [ml_norm.reshape(1, ML_DV), jnp.repeat(d_skip, SSD_HEADDIM).reshape(1, HALF),
              ssd_norm.reshape(1, HALF)]
    specs = [lambda tm: _seq_spec(tm, HALF)] * 7
    xl = _outproj(_outproj_odd_kernel, [m_l[0], m_l[1], pl_[2], s_l[0], s_l[1], pl_[5], pl_[3]], specs,
                  xl, mod_l, w_out, ln_g, ln_b, consts, "col", 512)
    if need_ctx:
        xc = _outproj(_outproj_odd_kernel, [m_c[0], m_c[1], pc[2], s_c[0], s_c[1], pc[5], pc[3]], specs,
                      xc, mod_c, w_out, ln_g, ln_b, consts, "row", 512)
    return xc, xl


def kernel(x, c, ctx, c_ctx, w_ada, b_ada, ln_g, ln_b, w_ffn_in, w_ffn_out, w_in_even, w_out_even, hy_conv_w, hy_conv_b, hy_w1, hy_b1, hy_w2, hy_b2, hy_w3, hy_freq, hy_bias, gla_a_w, gla_a_b, gla_norm_w, w_in_odd, w_out_odd, ml_gate_b, ml_norm_w, ssd_conv_w, ssd_conv_b, ssd_dt_bias, ssd_a_log, ssd_d, ssd_norm_w):
    bsz, n, d = x.shape
    depth = w_ada.shape[0]
    assert bsz == 2 and d == D_MODEL
    cond8 = jnp.zeros((8, d), F32).at[:bsz].set(c).at[bsz].set(c_ctx)
    mod_all = _ada_all(cond8, w_ada, b_ada)
    xl, xc = x, ctx
    for l in range(depth):
        last = l == depth - 1
        mod_l = mod_all[l, :bsz].reshape(bsz, N_MOD, d)
        mod_c = jnp.broadcast_to(mod_all[l, bsz].reshape(1, N_MOD, d), (bsz, N_MOD, d))
        w1a, w2a = w_ffn_in[l, 0].astype(BF16), w_ffn_out[l, 0].astype(BF16)
        w1b, w2b = w_ffn_in[l, 1].astype(BF16), w_ffn_out[l, 1].astype(BF16)
        odd = l % 2 == 1
        xl = _ffn_step(xl, mod_l, 0, w1a, w2a, ln_g[l, 0], ln_b[l, 0], out_cv=odd)
        xc = _ffn_step(xc, mod_c, 0, w1a, w2a, ln_g[l, 0], ln_b[l, 0])
        i = l // 2
        if not odd:
            hy_p = (hy_conv_w[i], hy_conv_b[i],
                    (hy_w1[i], hy_b1[i], hy_w2[i], hy_b2[i], hy_w3[i], hy_freq[i]), hy_bias[i])
            gla_p = (gla_a_w[i], gla_a_b[i], gla_norm_w[i])
            xc, xl = _even_layer_mix(xc, xl, mod_c, mod_l, w_in_even[i], w_out_even[i].astype(BF16),
                                     hy_p, gla_p, ln_g[l, 1], ln_b[l, 1], not last)
        else:
            ml_p = (ml_gate_b[i], ml_norm_w[i])
            ssd_p = (ssd_conv_w[i], ssd_conv_b[i], ssd_dt_bias[i], ssd_a_log[i], ssd_d[i], ssd_norm_w[i])
            xc, xl = _odd_layer_mix(xc, xl, mod_c, mod_l, w_in_odd[i], w_out_odd[i].astype(BF16),
                                    ml_p, ssd_p, ln_g[l, 1], ln_b[l, 1], not last)
        xl = _ffn_step(xl, mod_l, 6, w1b, w2b, ln_g[l, 2], ln_b[l, 2], in_cv=odd)
        if not last:
            xc = _ffn_step(xc, mod_c, 6, w1b, w2b, ln_g[l, 2], ln_b[l, 2])
    return xl
```

```python
import functools
import math

import numpy as np
import jax
import jax.numpy as jnp
from jax import lax
from jax.experimental import pallas as pl
from jax.experimental.pallas import tpu as pltpu

F32 = jnp.float32
BF16 = jnp.bfloat16

D_MODEL = 1024
DEPTH = 4
GRID_W = 64
D_FF = 2816
N_MOD = 9
HALF = D_MODEL // 2
CHUNK = 64
LN_EPS = 1e-5
NORM_EPS = 1e-6

HY_CH = HALF
HY_ORDER = 2
HY_EMB = 33
HY_BANDS = (HY_EMB - 1) // 2
HY_FAST_DECAY = 0.3
HY_SLOW_DECAY = 1.5
HY_TARGET = 1e-2
HY_COLS = 3 * HY_CH
HY_ROWS = HY_ORDER * HY_CH

GLA_HEADS = 4
GLA_DK = HALF // 2 // GLA_HEADS
GLA_DV = HALF // GLA_HEADS
GLA_RANK = 16
GLA_TAU = 16.0
GLA_QK = GLA_HEADS * GLA_DK

ML_HEADS = 4
ML_DQK = HALF // 2 // ML_HEADS
ML_DV = HALF // ML_HEADS
ML_QK = ML_HEADS * ML_DQK

SSD_HEADDIM = 64
SSD_HEADS = HALF // SSD_HEADDIM
SSD_GROUPS = 2
SSD_HPG = SSD_HEADS // SSD_GROUPS
SSD_STATE = 128
SSD_GN = SSD_GROUPS * SSD_STATE
SSD_CONV_DIM = HALF + 2 * SSD_GN

DN_ALPHA = (2.0 * DEPTH) ** 0.25

SMALL = 32
LANES = 128
VMEM_LIMIT = 56 * 1024 * 1024


def _cparams(*sem):
    return pltpu.CompilerParams(dimension_semantics=sem, vmem_limit_bytes=VMEM_LIMIT)


def _mm(a, b):
    return jnp.dot(a.astype(BF16), b.astype(BF16), preferred_element_type=F32)


def _mm_nt(a, b):
    return lax.dot_general(a.astype(BF16), b.astype(BF16), (((1,), (1,)), ((), ())),
                           preferred_element_type=F32)


def _mm_tn(a, b):
    return lax.dot_general(a.astype(BF16), b.astype(BF16), (((0,), (0,)), ((), ())),
                           preferred_element_type=F32)


def _mm_hi(a, b):
    return jnp.dot(a, b, precision=lax.Precision.HIGHEST, preferred_element_type=F32)


def _mm_nt_hi(a, b):
    return lax.dot_general(a, b, (((1,), (1,)), ((), ())), precision=lax.Precision.HIGHEST,
                           preferred_element_type=F32)


def _split3(x):
    x1 = x.astype(BF16)
    r1 = x - x1.astype(F32)
    x2 = r1.astype(BF16)
    return x1, x2, (r1 - x2.astype(F32)).astype(BF16)


def _sel_rows(sel3, x):
    return jnp.dot(sel3.astype(BF16), jnp.concatenate(_split3(x), axis=0), preferred_element_type=F32)


def _sel_cols(x, sel):
    sel = sel.astype(BF16)
    return sum(jnp.dot(p, sel, preferred_element_type=F32) for p in _split3(x))


def _sel_nt(a, b, split_a):
    dn = (((1,), (1,)), ((), ()))
    if split_a:
        return sum(lax.dot_general(p, b.astype(BF16), dn, preferred_element_type=F32) for p in _split3(a))
    return sum(lax.dot_general(a.astype(BF16), p, dn, preferred_element_type=F32) for p in _split3(b))


def _tri3(reverse):
    row = lax.broadcasted_iota(jnp.int32, (CHUNK, 3 * CHUNK), 0)
    col = lax.broadcasted_iota(jnp.int32, (CHUNK, 3 * CHUNK), 1) % CHUNK
    return ((row <= col) if reverse else (row >= col)).astype(F32)


def _silu(x):
    return x * jax.nn.sigmoid(x)


def _log_sigmoid(x):
    return jnp.minimum(x, 0.0) - jnp.log1p(jnp.exp(-jnp.abs(x)))


def _softplus(x):
    return jnp.maximum(x, 0.0) + jnp.log1p(jnp.exp(-jnp.abs(x)))


def _layer_norm(y, g, b):
    yc = y - jnp.mean(y, axis=-1, keepdims=True)
    return yc * lax.rsqrt(jnp.mean(yc * yc, axis=-1, keepdims=True) + LN_EPS) * g + b


def _rms(y):
    return y * lax.rsqrt(jnp.mean(y * y, axis=-1, keepdims=True) + NORM_EPS)


def _tri(reverse):
    row = lax.broadcasted_iota(jnp.int32, (CHUNK, CHUNK), 0)
    col = lax.broadcasted_iota(jnp.int32, (CHUNK, CHUNK), 1)
    return (row <= col) if reverse else (row >= col)


def _eye(n):
    return (lax.broadcasted_iota(jnp.int32, (n, n), 0) == lax.broadcasted_iota(jnp.int32, (n, n), 1)).astype(F32)


def _resident(shape):
    nd = len(shape)
    return pl.BlockSpec(shape, lambda *_: (0,) * nd, pipeline_mode=pl.Buffered(1))


def _cplx_mm(ar, ai, w):
    u, v = _mm(ar, w), _mm(ai, w)
    h = w.shape[1] // 2
    return u[:, :h] - v[:, h:], u[:, h:] + v[:, :h]


def _cplx_mm_left(fr, fi, xr, xi):
    rhs = jnp.concatenate([xr, xi], axis=1)
    p, q = _mm(fr, rhs), _mm(fi, rhs)
    h = xr.shape[1]
    return p[:, :h] - q[:, h:], p[:, h:] + q[:, :h]


def _ada_kernel(c_ref, w_ref, b_ref, o_ref):
    o_ref[0] = _mm(_silu(c_ref[...]), w_ref[0]) + b_ref[0]


def _ada_all(cond8, w_ada, b_ada):
    depth, d, nm = w_ada.shape
    tn = nm // 8
    return pl.pallas_call(
        _ada_kernel,
        grid=(depth, nm // tn),
        in_specs=[pl.BlockSpec((8, d), lambda l, j: (0, 0)),
                  pl.BlockSpec((1, d, tn), lambda l, j: (l, 0, j)),
                  pl.BlockSpec((1, 1, tn), lambda l, j: (l, 0, j))],
        out_specs=pl.BlockSpec((1, 8, tn), lambda l, j: (l, 0, j)),
        out_shape=jax.ShapeDtypeStruct((depth, 8, nm), F32),
        name="ada",
        compiler_params=_cparams("arbitrary", "arbitrary"),
    )(cond8, w_ada, b_ada.reshape(depth, 1, nm))


def _stream_dims(x, layout):
    if layout == "col":
        return x.shape[0], x.shape[1] * GRID_W, x.shape[2] // GRID_W
    return x.shape


COL_TILE = 4


def _tile_rows(n, layout, tm):
    return COL_TILE * (n // GRID_W) if layout == "col" else min(tm, n)


def _main_spec(n, d, layout, tm):
    if layout == "col":
        return pl.BlockSpec((None, n // GRID_W, COL_TILE * d), lambda b, i: (b, 0, i))
    return pl.BlockSpec((None, tm, d), lambda b, i: (b, i, 0))


def _load_tile(x_ref, d):
    cols = x_ref.shape[1] // d
    if cols == 1:
        return x_ref[...]
    return jnp.concatenate([x_ref[:, c * d:(c + 1) * d] for c in range(cols)], axis=0)


def _store_tile(o_ref, y, d):
    cols = o_ref.shape[1] // d
    rows = o_ref.shape[0]
    if cols == 1:
        o_ref[...] = y
    else:
        for c in range(cols):
            o_ref[:, c * d:(c + 1) * d] = y[c * rows:(c + 1) * rows]


def _halo_specs(n, d, layout, tm):
    if layout == "col":
        r8 = n // GRID_W // 8
        prev = pl.BlockSpec((None, 8, d), lambda b, i: (b, r8 - 1, jnp.maximum(COL_TILE * i - 1, 0)))
        nxt = pl.BlockSpec((None, 8, d), lambda b, i: (b, 0, jnp.minimum(COL_TILE * (i + 1), GRID_W - 1)))
    else:
        t8, n8 = tm // 8, n // 8
        prev = pl.BlockSpec((None, 8, d), lambda b, i: (b, jnp.maximum(i * t8 - 1, 0), 0))
        nxt = pl.BlockSpec((None, 8, d), lambda b, i: (b, jnp.minimum((i + 1) * t8, n8 - 1), 0))
    return prev, nxt


def _seq_spec(tm, cols):
    return pl.BlockSpec((None, tm, cols), lambda b, i: (b, i, 0))


def _chan_spec(rows, tm):
    return pl.BlockSpec((None, rows, tm // LANES, LANES), lambda b, i: (b, 0, i, 0))


def _lanes_of(ref, lead, pieces):
    return jnp.concatenate([ref[lead + (slice(None), q, slice(None))] for q in range(pieces)], axis=1)


def _ffn_kernel(x_ref, mod_ref, w1_ref, w2_ref, g_ref, b_ref, o_ref, *, mo, nk, in_cv, out_cv):
    d = w2_ref.shape[1]
    pieces = in_cv or out_cv
    if in_cv:
        x = jnp.concatenate([x_ref[:, c * d:(c + 1) * d] for c in range(GRID_W)], axis=0)
    elif pieces:
        x = jnp.concatenate([x_ref[:, c, :] for c in range(GRID_W)], axis=0)
    else:
        x = x_ref[...]
    shift, scale, gate = mod_ref[mo:mo + 1, :], mod_ref[mo + 1:mo + 2, :], mod_ref[mo + 2:mo + 3, :]
    h = (x * (1.0 + scale) + shift).astype(BF16)
    tf = D_FF // nk
    acc = jnp.zeros(x.shape, F32)
    for k in range(nk):
        gt = jnp.dot(h, w1_ref[:, k * tf:(k + 1) * tf], preferred_element_type=F32)
        up = jnp.dot(h, w1_ref[:, D_FF + k * tf:D_FF + (k + 1) * tf], preferred_element_type=F32)
        acc = acc + _mm(_silu(gt) * up, w2_ref[k * tf:(k + 1) * tf, :])
    y = _layer_norm(DN_ALPHA * x + (0.5 * gate) * acc, g_ref[...], b_ref[...])
    if out_cv:
        for c in range(GRID_W):
            o_ref[:, c * d:(c + 1) * d] = y[8 * c:8 * c + 8]
    elif pieces:
        for c in range(GRID_W):
            o_ref[:, c, :] = y[8 * c:8 * c + 8]
    else:
        o_ref[...] = y


def _ffn_step(x, mod, mo, w1, w2, g, b, in_cv=False, out_cv=False, tm=512):
    bsz, n, d = _stream_dims(x, "col" if in_cv else "row")
    rows = n // GRID_W
    if in_cv or out_cv:
        tm = 8 * GRID_W
        cv_spec = pl.BlockSpec((None, 8, GRID_W * d), lambda bb, i: (bb, i, 0))
        tok_spec = pl.BlockSpec((None, 8, GRID_W, d), lambda bb, i: (bb, i, 0, 0))
        x = x if in_cv else x.reshape(bsz, rows, GRID_W, d)
        in_spec, out_spec = (cv_spec if in_cv else tok_spec), (cv_spec if out_cv else tok_spec)
        out_dims = (bsz, rows, GRID_W * d) if out_cv else (bsz, rows, GRID_W, d)
    else:
        tm = min(tm, n)
        in_spec = out_spec = _seq_spec(tm, d)
        out_dims = (bsz, n, d)
    out = pl.pallas_call(
        functools.partial(_ffn_kernel, mo=mo, nk=1, in_cv=in_cv, out_cv=out_cv),
        grid=(bsz, n // tm),
        in_specs=[in_spec,
                  pl.BlockSpec((None, N_MOD, d), lambda bb, i: (bb, 0, 0)),
                  _resident(w1.shape), _resident(w2.shape),
                  _resident((1, d)), _resident((1, d))],
        out_specs=out_spec,
        out_shape=jax.ShapeDtypeStruct(out_dims, F32),
        name="ffn",
        compiler_params=_cparams("arbitrary", "arbitrary"),
    )(x, mod, w1, w2, g.reshape(1, d), b.reshape(1, d))
    return out if out_cv else out.reshape(bsz, n, d)


def _inproj_kernel(x_ref, xp_ref, xn_ref, mod_ref, wp_ref, wc_ref, cw_ref, cb_ref, *out_refs,
                   psplit, csplit, act, chan_major):
    i, last = pl.program_id(1), pl.num_programs(1) - 1
    shift, scale = mod_ref[3:4, :], mod_ref[4:5, :]
    x = _load_tile(x_ref, mod_ref.shape[1])
    tm = x.shape[0]
    h = x * (1.0 + scale) + shift
    hb = h.astype(BF16)
    o = 0
    for k, w in enumerate(psplit):
        out_refs[k][...] = jnp.dot(hb, wp_ref[:, o:o + w], preferred_element_type=F32).astype(out_refs[k].dtype)
        o += w
    hp = (xp_ref[...] * (1.0 + scale) + shift) * jnp.where(i > 0, 1.0, 0.0)
    hn = (xn_ref[...] * (1.0 + scale) + shift) * jnp.where(i < last, 1.0, 0.0)
    couts = out_refs[len(psplit):]
    if chan_major:
        lane = lax.broadcasted_iota(jnp.int32, (1, tm), 1)
        halo = jnp.concatenate([hp, hn, jnp.zeros((LANES - 16, hp.shape[1]), F32)], axis=0).astype(BF16)
        o = 0
        for k, w in enumerate(csplit):
            wt = wc_ref[o:o + w, :]
            p = _mm_nt(wt, hb)
            p_halo = _mm_nt(wt, halo)
            up = jnp.where(lane == 0, p_halo[:, 7:8], pltpu.roll(p, 1, 1))
            dn = jnp.where(lane == tm - 1, p_halo[:, 8:9], pltpu.roll(p, tm - 1, 1))
            cw = cw_ref[o:o + w, :]
            y = cw[:, 0:1] * up + cw[:, 1:2] * p + cw[:, 2:3] * dn + cb_ref[o:o + w, :]
            couts[k][...] = y.reshape(w, tm // LANES, LANES)
            o += w
    else:
        pc = _mm(jnp.concatenate([hp, h, hn], axis=0), wc_ref[...])
        up = pltpu.roll(pc, 1, 0)
        dn = pltpu.roll(pc, tm + 15, 0)
        y = (cw_ref[0:1, :] * up + cw_ref[1:2, :] * pc + cw_ref[2:3, :] * dn + cb_ref[...])[8:8 + tm]
        if act:
            y = _silu(y)
        o = 0
        for k, w in enumerate(csplit):
            couts[k][...] = y[:, o:o + w].astype(couts[k].dtype)
            o += w


def _inproj(xv, mod, wp, wc, cw, cb, psplit, csplit, act, layout, chan_major, tm):
    bsz, n, d = _stream_dims(xv, layout)
    tm = _tile_rows(n, layout, tm)
    prev, nxt = _halo_specs(n, d, layout, tm)
    pdt = [BF16 if w >= LANES else F32 for w in psplit]
    out_specs = [_seq_spec(tm, w) for w in psplit]
    out_shape = [jax.ShapeDtypeStruct((bsz, n, w), dt) for w, dt in zip(psplit, pdt)]
    if chan_major:
        out_specs += [_chan_spec(w, tm) for w in csplit]
        out_shape += [jax.ShapeDtypeStruct((bsz, w, n // LANES, LANES), F32) for w in csplit]
    else:
        out_specs += [_seq_spec(tm, w) for w in csplit]
        out_shape += [jax.ShapeDtypeStruct((bsz, n, w), BF16) for w in csplit]
    return pl.pallas_call(
        functools.partial(_inproj_kernel, psplit=tuple(psplit), csplit=tuple(csplit), act=act,
                          chan_major=chan_major),
        grid=(bsz, n // tm),
        in_specs=[_main_spec(n, d, layout, tm), prev, nxt,
                  pl.BlockSpec((None, N_MOD, d), lambda bb, i: (bb, 0, 0)),
                  _resident(wp.shape), _resident(wc.shape), _resident(cw.shape), _resident(cb.shape)],
        out_specs=out_specs,
        out_shape=out_shape,
        name="inproj_chan" if chan_major else "inproj",
        compiler_params=_cparams("arbitrary", "arbitrary"),
    )(xv, xv, xv, mod, wp, wc, cw, cb)


def _outproj_even_kernel(hy_ref, of_ref, ob_ref, g_ref, x_ref, mod_ref, w_ref, lg_ref, lb_ref, nw_ref, o_ref):
    tm = x_ref.shape[0]
    y = _mm_tn(hy_ref[...].reshape(HY_CH, tm), w_ref[0:HALF, :])
    o = of_ref[...].astype(F32) + ob_ref[...].astype(F32)
    g = g_ref[...].astype(F32)
    heads = [_rms(o[:, h * GLA_DV:(h + 1) * GLA_DV]) * nw_ref[...] for h in range(GLA_HEADS)]
    y = y + _mm(_silu(g) * jnp.concatenate(heads, axis=1), w_ref[HALF:2 * HALF, :])
    o_ref[...] = _layer_norm(DN_ALPHA * x_ref[...] + mod_ref[5:6, :] * y, lg_ref[...], lb_ref[...])


def _outproj_odd_kernel(hf_ref, hb_ref, og_ref, yf_ref, yb_ref, xs_ref, z_ref, x_ref, mod_ref, w_ref,
                        lg_ref, lb_ref, mnw_ref, dsk_ref, snw_ref, o_ref):
    h = hf_ref[...].astype(F32) + hb_ref[...].astype(F32)
    heads = []
    for k in range(ML_HEADS):
        hk = h[:, k * ML_DV:(k + 1) * ML_DV]
        hc = hk - jnp.mean(hk, axis=-1, keepdims=True)
        heads.append(hc * lax.rsqrt(jnp.mean(hc * hc, axis=-1, keepdims=True) + NORM_EPS) * mnw_ref[...])
    m = jax.nn.sigmoid(og_ref[...].astype(F32)) * jnp.concatenate(heads, axis=1)
    s = yf_ref[...].astype(F32) + yb_ref[...].astype(F32) + dsk_ref[...] * xs_ref[...].astype(F32)
    s = _rms(s * _silu(z_ref[...].astype(F32))) * snw_ref[...]
    y = _mm(m, w_ref[0:HALF, :]) + _mm(s, w_ref[HALF:2 * HALF, :])
    d = mod_ref.shape[1]
    x = _load_tile(x_ref, d)
    _store_tile(o_ref, _layer_norm(DN_ALPHA * x + mod_ref[5:6, :] * y, lg_ref[...], lb_ref[...]), d)


def _outproj(kern, parts, part_specs, xv, mod, w, g, b, consts, layout, tm):
    bsz, n, d = _stream_dims(xv, layout)
    tm = _tile_rows(n, layout, tm)
    consts = [g.reshape(1, d), b.reshape(1, d)] + consts
    return pl.pallas_call(
        kern,
        grid=(bsz, n // tm),
        in_specs=[s(tm) for s in part_specs] + [_main_spec(n, d, layout, tm),
                  pl.BlockSpec((None, N_MOD, d), lambda bb, i: (bb, 0, 0)), _resident(w.shape)]
        + [_resident(c.shape) for c in consts],
        out_specs=_main_spec(n, d, layout, tm),
        out_shape=jax.ShapeDtypeStruct(xv.shape, F32),
        name=kern.__name__.strip("_"),
        compiler_params=_cparams("arbitrary", "arbitrary"),
    )(*parts, xv, mod, w, *consts)


def _scan_call(kernel, seqs, consts, states, out_cols, tb=512):
    bsz, n, _ = seqs[0].shape
    tb = min(tb, n)
    nblk = n // tb

    def fmap(i):
        return (0, i, 0)

    def bmap(i):
        return (0, nblk - 1 - i, 0)

    in_specs = [pl.BlockSpec((bsz, tb, s.shape[-1]), fmap) for s in seqs]
    in_specs += [pl.BlockSpec((bsz, tb, s.shape[-1]), bmap) for s in seqs]
    in_specs += [_resident(c.shape) for c in consts]
    st_specs = [pl.BlockSpec(s.shape, lambda i, nd=s.ndim: (0,) * nd) for s in states]
    outs = pl.pallas_call(
        functools.partial(kernel, nchunk=tb // CHUNK),
        grid=(nblk,),
        in_specs=in_specs + st_specs,
        out_specs=[pl.BlockSpec((bsz, tb, out_cols), fmap), pl.BlockSpec((bsz, tb, out_cols), bmap)] + st_specs,
        out_shape=[jax.ShapeDtypeStruct((bsz, n, out_cols), BF16)] * 2
        + [jax.ShapeDtypeStruct(s.shape, F32) for s in states],
        scratch_shapes=[pltpu.VMEM(s.shape, F32) for s in states],
        name=kernel.__name__.strip("_"),
        compiler_params=_cparams("arbitrary"),
    )(*seqs, *seqs, *consts, *states)
    return outs[0], outs[1], tuple(outs[2:])


def _chunk_rows(ci, nchunk, reverse):
    c = (nchunk - 1 - ci) if reverse else ci
    return pl.ds(pl.multiple_of(c * CHUNK, CHUNK), CHUNK)


def _chains(ci, nchunk, bsz, unroll):
    return [(d, b, _chunk_rows(ci * unroll + u, nchunk, d == 1))
            for u in range(unroll) for d in range(2) for b in range(bsz)]


def _interleave(gens):
    results = [None] * len(gens)
    live = list(enumerate(gens))
    while live:
        still = []
        for idx, g in live:
            try:
                next(g)
                still.append((idx, g))
            except StopIteration as stop:
                results[idx] = stop.value
        live = still
    return results


def _scan_states(i, s0_refs, sout_refs, scrs, body, trips):
    @pl.when(i == 0)
    def _():
        for s0, scr in zip(s0_refs, scrs):
            scr[...] = s0[...]

    lax.fori_loop(0, trips, body, 0)

    @pl.when(i == pl.num_programs(0) - 1)
    def _():
        for so, scr in zip(sout_refs, scrs):
            so[...] = scr[...]


def _gla_chunk(qk, v, a, aw, ab, state, tri, head):
    heads = range(GLA_HEADS)
    q = qk[:, 0:GLA_QK].astype(F32) * (GLA_DK ** -0.5)
    k = qk[:, GLA_QK:2 * GLA_QK].astype(F32)
    tri, tri3 = tri
    lr = _mm(a, aw)
    yield
    log_a = _log_sigmoid(lr + ab) * (1.0 / GLA_TAU)
    b = _sel_rows(tri3, log_a)
    yield
    b_last = jnp.sum(log_a, axis=0, keepdims=True)
    q_dec = q * jnp.exp(b)
    k_inv = k * jnp.exp(-b)
    k_dec = k * jnp.exp(b_last - b)
    vs = [v[:, h * GLA_DV:(h + 1) * GLA_DV] for h in heads]
    q4 = jnp.concatenate([jnp.where(head == h, q_dec, 0.0) for h in heads], axis=0).astype(BF16)
    att4 = _mm_nt(q4, k_inv)
    kv = _mm_tn(jnp.concatenate(vs, axis=0),
                jnp.concatenate([jnp.where(head == h, k_dec, 0.0) for h in heads], axis=0))
    yield
    s_old = state[0]
    inter4 = _mm_nt(q4, s_old)
    state[0] = s_old * jnp.exp(b_last) + kv
    outs = [_mm(jnp.where(tri, att4[h * CHUNK:(h + 1) * CHUNK], 0.0), vs[h]) for h in heads]
    yield
    return [outs[h] + inter4[h * CHUNK:(h + 1) * CHUNK] for h in heads]


def _gla_kernel(qkf_ref, vf_ref, af_ref, qkb_ref, vb_ref, ab_ref, aw_ref, abias_ref, s0_ref,
                of_ref, ob_ref, sout_ref, s_scr, *, nchunk):
    head = lax.broadcasted_iota(jnp.int32, (1, GLA_QK), 1) // GLA_DK
    tris = tuple((_tri(rev), _tri3(rev)) for rev in (False, True))
    dirs = ((qkf_ref, vf_ref, af_ref, of_ref), (qkb_ref, vb_ref, ab_ref, ob_ref))
    bsz, unroll = qkf_ref.shape[0], 4

    def body(ci, carry):
        chains = _chains(ci, nchunk, bsz, unroll)
        states = {(d, b): [s_scr[d, b]] for d in range(2) for b in range(bsz)}
        done = _interleave([_gla_chunk(dirs[d][0][b, rows, :], dirs[d][1][b, rows, :], dirs[d][2][b, rows, :],
                                       aw_ref[d], abias_ref[d], states[d, b], tris[d], head)
                            for d, b, rows in chains])
        for (d, b, rows), outs in zip(chains, done):
            for h, oh in enumerate(outs):
                dirs[d][3][b, rows, h * GLA_DV:(h + 1) * GLA_DV] = oh.astype(BF16)
        for (d, b), st in states.items():
            s_scr[d, b] = st[0]
        return carry

    _scan_states(pl.program_id(0), [s0_ref], [sout_ref], [s_scr], body, nchunk // unroll)


def _gla_mixer(ctx_p, lat_p, a_w, a_b):
    bsz = ctx_p[0].shape[0]
    aw = jnp.zeros((2, SMALL, GLA_QK), F32)
    for d in range(2):
        aw = aw.at[d, d * GLA_RANK:(d + 1) * GLA_RANK].set(a_w[d])
    ab = a_b.reshape(2, 1, GLA_QK)
    st = (jnp.zeros((2, bsz, GLA_DV, GLA_QK), F32),)
    cf, cb, st = _scan_call(_gla_kernel, list(ctx_p), [aw, ab], st, HALF)
    lf, lb, _ = _scan_call(_gla_kernel, list(lat_p), [aw, ab], st, HALF)
    return (cf, cb), (lf, lb)


def _mlstm_chunk(qk, v, gates, state, tri, head, eye, shift, spread, d):
    q = qk[:, 0:ML_QK].astype(F32)
    k = qk[:, ML_QK:2 * ML_QK].astype(F32) * (ML_DQK ** -0.5)
    tri, tri3 = tri
    heads = range(ML_HEADS)
    trif = tri.astype(F32)
    gates_t = _mm_nt_hi(eye, gates)
    lf = _log_sigmoid(gates)
    b_col = _mm_hi(trif, lf)
    yield
    b_row = _mm_nt_hi(_log_sigmoid(gates_t), trif)
    b_tot = jnp.sum(lf, axis=0, keepdims=True)
    vs = [v[:, h * ML_DV:(h + 1) * ML_DV] for h in heads]
    qms = [jnp.where(head == h, q, 0.0) for h in heads]
    s_raw = [_mm_nt(qm, k) for qm in qms]
    c_old, n_all, m_all = state
    n_old = n_all[0:1, :]
    inter = [_mm_nt(qm, c_old) for qm in qms]
    m_old = m_all[0:1, :]
    a = b_tot - b_col + _sel_cols(gates, shift)
    a_max = jnp.max(a, axis=0, keepdims=True)
    m_new = jnp.maximum(b_tot + m_old, a_max)
    w_old, w_new = jnp.exp(b_tot + m_old - m_new), jnp.exp(a_max - m_new)
    kw_all = k * _mm(jnp.exp(a - a_max) * w_new, spread)
    keep = _sel_cols(jnp.broadcast_to(w_old, (8, SMALL)), spread)[0:1, :]
    n_add = jnp.sum(kw_all, axis=0, keepdims=True)
    c_add = _mm_tn(jnp.concatenate(vs, axis=0),
                   jnp.concatenate([jnp.where(head == h, kw_all, 0.0) for h in heads], axis=0))
    state[:] = [c_old * keep + c_add, jnp.broadcast_to(n_old * keep + n_add, n_all.shape),
                jnp.broadcast_to(m_new, m_all.shape)]
    yield
    ms, es, ss = [], [], []
    for h in heads:
        ic, fc = d * 8 + h, d * 8 + 4 + h
        bc = b_col[:, fc:fc + 1]
        r_vis = jnp.where(tri, gates_t[ic:ic + 1, :] - b_row[fc:fc + 1, :], -jnp.inf)
        g = bc + m_all[0:1, fc:fc + 1]
        m = jnp.maximum(g, bc + jnp.max(r_vis, axis=1, keepdims=True))
        ms.append(m)
        es.append(jnp.exp(g - m))
        ss.append(s_raw[h] * jnp.exp(r_vis + (bc - m)))
    nums = [_mm(ss[h], vs[h]) for h in heads]
    yield
    outs = []
    for h in heads:
        num = nums[h] + es[h] * inter[h]
        den = jnp.sum(ss[h], axis=1, keepdims=True) + es[h] * jnp.sum(qms[h] * n_old, axis=1, keepdims=True)
        outs.append(num / jnp.maximum(jnp.abs(den), jnp.exp(-ms[h])))
    return outs


def _mlstm_kernel(qkf_ref, vf_ref, gf_ref, qkb_ref, vb_ref, gb_ref, bias_ref, c0_ref, n0_ref, m0_ref,
                  hf_ref, hb_ref, cout_ref, nout_ref, mout_ref, c_scr, n_scr, m_scr, *, nchunk):
    head = lax.broadcasted_iota(jnp.int32, (1, ML_QK), 1) // ML_DQK
    col_r = lax.broadcasted_iota(jnp.int32, (SMALL, SMALL), 0)
    col_c = lax.broadcasted_iota(jnp.int32, (SMALL, SMALL), 1)
    shift = ((col_c == col_r + ML_HEADS) & (col_r % 8 < ML_HEADS)).astype(F32)
    row = lax.broadcasted_iota(jnp.int32, (SMALL, ML_QK), 0)
    lane_head = lax.broadcasted_iota(jnp.int32, (SMALL, ML_QK), 1) // ML_DQK
    spreads = [(row == d * 8 + ML_HEADS + lane_head).astype(F32) for d in range(2)]
    eye = _eye(SMALL)
    tris = tuple((_tri(rev), _tri3(rev)) for rev in (False, True))
    dirs = ((qkf_ref, vf_ref, gf_ref, hf_ref), (qkb_ref, vb_ref, gb_ref, hb_ref))
    bsz, unroll = qkf_ref.shape[0], 2

    def body(ci, carry):
        chains = _chains(ci, nchunk, bsz, unroll)
        states = {(d, b): [c_scr[d, b], n_scr[d, b], m_scr[d, b]] for d in range(2) for b in range(bsz)}
        done = _interleave([_mlstm_chunk(
            dirs[d][0][b, rows, :], dirs[d][1][b, rows, :], dirs[d][2][b, rows, :] + bias_ref[...],
            states[d, b], tris[d], head, eye, shift, spreads[d], d) for d, b, rows in chains])
        for (d, b, rows), outs in zip(chains, done):
            for h, oh in enumerate(outs):
                dirs[d][3][b, rows, h * ML_DV:(h + 1) * ML_DV] = oh.astype(BF16)
        for (d, b), st in states.items():
            c_scr[d, b], n_scr[d, b], m_scr[d, b] = st
        return carry

    _scan_states(pl.program_id(0), [c0_ref, n0_ref, m0_ref], [cout_ref, nout_ref, mout_ref],
                 [c_scr, n_scr, m_scr], body, nchunk // unroll)


def _mlstm_mixer(ctx_p, lat_p, gate_b):
    bsz = ctx_p[0].shape[0]
    st = (jnp.zeros((2, bsz, ML_DV, ML_QK), F32), jnp.zeros((2, bsz, 8, ML_QK), F32),
          jnp.zeros((2, bsz, 8, SMALL), F32))
    gb = jnp.zeros((1, SMALL), F32).at[0, :16].set(gate_b.reshape(16))
    cf, cb, st = _scan_call(_mlstm_kernel, list(ctx_p), [gb], st, HALF)
    lf, lb, _ = _scan_call(_mlstm_kernel, list(lat_p), [gb], st, HALF)
    return (cf, cb), (lf, lb)


def _ssd_chunk(x, bm, cm, raw, alr, alc, expand, state, tri, sub, eye, d):
    x, bm, cm = x.astype(F32), bm.astype(F32), cm.astype(F32)
    gw = SSD_HPG * SSD_HEADDIM
    tri, tri3 = tri
    groups = range(SSD_GROUPS)
    lns = [slice(g * gw, (g + 1) * gw) for g in groups]
    cms = [cm[:, g * SSD_STATE:(g + 1) * SSD_STATE] for g in groups]
    bms = [bm[:, g * SSD_STATE:(g + 1) * SSD_STATE] for g in groups]
    raw_t = _sel_nt(eye, raw, False)
    dt = _softplus(raw)
    da = dt * (-jnp.exp(alr))
    cum = _sel_rows(tri3, da)
    cbs = [_mm_nt(cms[g], bms[g]) for g in groups]
    yield
    dt_t = _softplus(raw_t)
    cum_t = _sel_nt(dt_t * (-jnp.exp(alc)), tri, True)
    cl = jnp.sum(da, axis=0, keepdims=True)
    e_cum = _sel_cols(jnp.exp(cum), expand)
    wgt = _sel_cols(jnp.exp(cl - cum) * dt, expand)
    dec = _sel_cols(jnp.broadcast_to(jnp.exp(cl), (8, SMALL)), expand)[0:1, :]
    yield
    xw = x * wgt
    kvs = [_mm_tn(bms[g], xw[:, lns[g]]) for g in groups]
    diags = []
    for g in groups:
        xg = x[:, lns[g]]
        mixes = []
        for j in range(SSD_HPG):
            col = 16 + d * 8 + g * SSD_HPG + j
            seg = jnp.exp(jnp.where(tri, cum[:, col:col + 1] - cum_t[col:col + 1, :], -jnp.inf))
            mixes.append(cbs[g] * seg * dt_t[col:col + 1, :])
        diags.append(_mm(jnp.concatenate(mixes, axis=1),
                         jnp.concatenate([jnp.where(sub == j, xg, 0.0) for j in range(SSD_HPG)], axis=0)))
    yield
    offs = [_mm(cms[g], state[g]) for g in groups]
    state[:] = [state[g] * dec[:, lns[g]] + kvs[g] for g in groups]
    yield
    return [offs[g] * e_cum[:, lns[g]] + diags[g] for g in groups]


def _ssd_kernel(xf_ref, bmf_ref, cmf_ref, rf_ref, xb_ref, bmb_ref, cmb_ref, rb_ref,
                dtb_ref, alr_ref, alc_ref, ex_ref, s0_ref, yf_ref, yb_ref, sout_ref, s_scr, *, nchunk):
    gw = SSD_HPG * SSD_HEADDIM
    sub = lax.broadcasted_iota(jnp.int32, (1, gw), 1) // SSD_HEADDIM
    eye = _eye(SMALL)
    tris = tuple((_tri(rev), _tri3(rev)) for rev in (False, True))
    dirs = ((xf_ref, bmf_ref, cmf_ref, rf_ref, yf_ref), (xb_ref, bmb_ref, cmb_ref, rb_ref, yb_ref))

    bsz, unroll = xf_ref.shape[0], 4

    def body(ci, carry):
        chains = _chains(ci, nchunk, bsz, unroll)
        states = {(d, b): [s_scr[d, b, :, g * gw:(g + 1) * gw] for g in range(SSD_GROUPS)]
                  for d in range(2) for b in range(bsz)}
        done = _interleave([_ssd_chunk(
            dirs[d][0][b, rows, :], dirs[d][1][b, rows, :], dirs[d][2][b, rows, :],
            dirs[d][3][b, rows, :] + dtb_ref[...], alr_ref[...], alc_ref[...],
            ex_ref[d], states[d, b], tris[d], sub, eye, d) for d, b, rows in chains])
        for (d, b, rows), ys in zip(chains, done):
            for g in range(SSD_GROUPS):
                dirs[d][4][b, rows, g * gw:(g + 1) * gw] = ys[g].astype(BF16)
        for (d, b), st in states.items():
            for g in range(SSD_GROUPS):
                s_scr[d, b, :, g * gw:(g + 1) * gw] = st[g]
        return carry

    _scan_states(pl.program_id(0), [s0_ref], [sout_ref], [s_scr], body, nchunk // unroll)


def _ssd_mixer(ctx_p, lat_p, dt_bias, a_log):
    bsz = ctx_p[0].shape[0]
    st = (jnp.zeros((2, bsz, SSD_STATE, HALF), F32),)
    dtb = jnp.zeros((1, SMALL), F32).at[0, 16:].set(dt_bias.reshape(16))
    al = jnp.zeros((1, SMALL), F32).at[0, 16:].set(a_log.reshape(16).astype(F32))
    lane_head = np.arange(HALF) // SSD_HEADDIM
    ex = jnp.asarray(np.stack([(np.arange(SMALL)[:, None] == (16 + d * 8 + lane_head)[None, :])
                               for d in range(2)]).astype(np.float32))
    consts = [dtb, al, al.reshape(SMALL, 1), ex]
    cf, cb, st = _scan_call(_ssd_kernel, list(ctx_p), consts, st, HALF)
    lf, lb, _ = _scan_call(_ssd_kernel, list(lat_p), consts, st, HALF)
    return (cf, cb), (lf, lb)


def _hy_filter_kernel(f_ref, w1t_ref, w1c_ref, w1s_ref, b1_ref, w2_ref, b2_ref, fq1_ref, fq2_ref, w3s_ref, w3b_ref,
                      dl_ref, o_ref, s_ref, *, n, npiece):
    side, jt = pl.program_id(0), pl.program_id(1)
    tnorm = 1.0 / max(n - 1, 1)
    first = (side == 0) & (jt == 0)
    width = npiece * LANES

    @pl.when(first)
    def _():
        s_ref[...] = jnp.zeros(s_ref.shape, F32)

    def mlp(pos):
        ang = pos * f_ref[...] * (2.0 * math.pi / n)
        pre = (pos * tnorm) * w1t_ref[...] + _mm_hi(w1c_ref[...], jnp.cos(ang)) \
            + _mm_hi(w1s_ref[...], -jnp.sin(ang)) + b1_ref[...]
        hid = jnp.sin(fq1_ref[...] * pre)
        return jnp.sin(fq2_ref[...] * (_mm_hi(w2_ref[...], hid) + b2_ref[...]))

    h00 = _mm(w3b_ref[...], mlp(jnp.zeros((1, LANES), F32)))[:, 0:1]
    tau = jt * width + lax.broadcasted_iota(jnp.int32, (1, width), 1)
    pos = (side * n + (1 - 2 * side) * tau).astype(F32)
    ht = _mm(w3s_ref[...], mlp(pos))
    ht = ht * jnp.exp(-(pos * tnorm) * dl_ref[...])
    ht = jnp.where((tau == 0) & (side == 1), 0.0, ht)
    s_ref[...] += jnp.sum(jnp.abs(ht), axis=1, keepdims=True) + jnp.where(first, jnp.abs(h00), 0.0)
    ht = ht + jnp.where((tau == 0) & (side == 0), h00, 0.0)
    for q in range(npiece):
        o_ref[q] = ht[:, q * LANES:(q + 1) * LANES]


def _hyena_filter(n, w1, b1, w2, b2, w3, freq):
    kk = n // LANES
    npiece = min(8, kk)
    ff = w1.shape[1]
    pad = LANES - ff
    bands = jnp.linspace(1e-4, HY_BANDS - 1, HY_BANDS, dtype=F32).reshape(HY_BANDS, 1)
    deltas = jnp.abs(jnp.linspace(math.log(HY_TARGET) / HY_SLOW_DECAY, math.log(HY_TARGET) / HY_FAST_DECAY,
                                  HY_CH, dtype=F32))
    dl = jnp.tile(deltas, HY_ORDER).reshape(HY_ROWS, 1)
    w3t = jnp.transpose(w3.reshape(ff, HY_ORDER, 2, HY_CH), (2, 1, 3, 0)).reshape(2, HY_ROWS, ff)
    w3t = jnp.pad(w3t, ((0, 0), (0, 0), (0, pad)))
    args = (bands, w1[0].reshape(ff, 1), jnp.transpose(w1[1:1 + HY_BANDS]), jnp.transpose(w1[1 + HY_BANDS:]),
            b1.reshape(ff, 1), jnp.pad(jnp.transpose(w2), ((0, pad), (0, 0))),
            jnp.pad(b2.reshape(ff, 1), ((0, pad), (0, 0))),
            freq[0].reshape(ff, 1), jnp.pad(freq[1].reshape(ff, 1), ((0, pad), (0, 0))))
    nt = kk // npiece
    return pl.pallas_call(
        functools.partial(_hy_filter_kernel, n=n, npiece=npiece),
        grid=(2, nt),
        in_specs=[_resident(a.shape) for a in args]
        + [pl.BlockSpec((None, HY_ROWS, LANES), lambda sd, j: (sd, 0, 0)),
           pl.BlockSpec((None, HY_ROWS, LANES), lambda sd, j: (1, 0, 0)), _resident(dl.shape)],
        out_specs=[pl.BlockSpec((npiece, HY_ROWS, LANES), lambda sd, j: (sd * nt + j, 0, 0)),
                   pl.BlockSpec((HY_ROWS, 1), lambda sd, j: (0, 0))],
        out_shape=[jax.ShapeDtypeStruct((2 * kk, HY_ROWS, LANES), F32),
                   jax.ShapeDtypeStruct((HY_ROWS, 1), F32)],
        name="hy_filter",
        compiler_params=_cparams("arbitrary", "arbitrary"),
    )(*args, w3t, w3t, dl)


def _dft_tables(n):
    big = 2 * n
    n1 = big // LANES
    k = np.arange(n1)

    def cis(num, den):
        ang = 2.0 * np.pi * (num % den) / den
        return np.stack([np.cos(ang), -np.sin(ang)]).astype(np.float32)

    f1 = cis(k[:, None] * k[None, :], n1)
    k2 = np.arange(LANES)
    f2 = cis(k2[:, None] * k2[None, :], LANES)
    tw = cis(k[:, None] * k2[None, :], big)
    w2 = np.concatenate([f2[0], f2[1]], axis=1)
    w2c = np.concatenate([f2[0], -f2[1]], axis=1)
    ginv = np.stack([f1[0, :n1 // 2, :], -f1[1, :n1 // 2, :]]) / big
    return tuple(jnp.asarray(a) for a in (f1, tw, w2, w2c, ginv))


HY_CB = 8
HY_GRP = 4


def _hy_spec_kernel(f_ref, s_ref, f1_ref, tw_ref, w2_ref, o_ref):
    f1r, f1i = f1_ref[0], f1_ref[1]
    twr, twi = tw_ref[0], tw_ref[1]
    n1 = f1r.shape[0]
    flat = f_ref[...].reshape(n1, HY_CB * LANES)
    for c0 in range(0, HY_CB, HY_GRP):
        trs, tis = [], []
        for ca in range(c0, c0 + HY_GRP, 2):
            rhs = flat[:, ca * LANES:(ca + 2) * LANES]
            pr, pi = _mm(f1r, rhs), _mm(f1i, rhs)
            for e in range(2):
                ar, ai = pr[:, e * LANES:(e + 1) * LANES], pi[:, e * LANES:(e + 1) * LANES]
                trs.append(ar * twr - ai * twi)
                tis.append(ar * twi + ai * twr)
        xr, xi = _cplx_mm(jnp.concatenate(trs, axis=0), jnp.concatenate(tis, axis=0), w2_ref[...])
        for e in range(HY_GRP):
            inv = 1.0 / s_ref[c0 + e]
            o_ref[0, c0 + e] = xr[e * n1:(e + 1) * n1] * inv
            o_ref[1, c0 + e] = xi[e * n1:(e + 1) * n1] * inv


def _hy_spectrum(filt, norm, f1, tw, w2):
    n1, rows, _ = filt.shape
    return pl.pallas_call(
        _hy_spec_kernel,
        grid=(rows // HY_CB,),
        in_specs=[pl.BlockSpec((n1, HY_CB, LANES), lambda j: (0, j, 0)),
                  pl.BlockSpec((HY_CB, 1, 1), lambda j: (j, 0, 0)),
                  _resident(f1.shape), _resident(tw.shape), _resident(w2.shape)],
        out_specs=pl.BlockSpec((2, HY_CB, n1, LANES), lambda j: (0, j, 0, 0)),
        out_shape=jax.ShapeDtypeStruct((2, rows, n1, LANES), F32),
        name="hy_spectrum",
        compiler_params=_cparams("arbitrary"),
    )(filt, norm.reshape(rows, 1, 1), f1, tw, w2)


def _hy_conv_kernel(v_ref, x_ref, hf_ref, bias_ref, f1_ref, g_ref, tw_ref, w2_ref, w2c_ref, o_ref):
    f1r, f1i = f1_ref[0], f1_ref[1]
    gr, gi = g_ref[0], g_ref[1]
    twr, twi = tw_ref[0], tw_ref[1]
    n1 = f1r.shape[0]
    for c0 in range(0, HY_CB, HY_GRP):
        vs = [(v_ref[0, c], v_ref[1, c]) for c in range(c0, c0 + HY_GRP)]
        trs, tis = [], []
        for v0, v1 in vs:
            ar, ai = _cplx_mm_left(f1r, f1i, v0, v1)
            trs.append(ar * twr - ai * twi)
            tis.append(ar * twi + ai * twr)
        xr, xi = _cplx_mm(jnp.concatenate(trs, axis=0), jnp.concatenate(tis, axis=0), w2_ref[...])
        hr = hf_ref[0, c0:c0 + HY_GRP].reshape(HY_GRP * n1, LANES)
        hi = hf_ref[1, c0:c0 + HY_GRP].reshape(HY_GRP * n1, LANES)
        zr, zi = _cplx_mm(xr * hr - xi * hi, xr * hi + xi * hr, w2c_ref[...])
        for e, (v0, v1) in enumerate(vs):
            zre, zie = zr[e * n1:(e + 1) * n1], zi[e * n1:(e + 1) * n1]
            cr, ci = _cplx_mm_left(gr, gi, zre * twr + zie * twi, zie * twr - zre * twi)
            bz = bias_ref[c0 + e]
            o_ref[0, c0 + e] = x_ref[0, c0 + e] * (cr + bz * v0)
            o_ref[1, c0 + e] = x_ref[1, c0 + e] * (ci + bz * v1)


def _hy_conv(v, xg, hf, bias3, order, tabs):
    f1, tw, w2, w2c, ginv = tabs
    bsz, ch, k1, _ = v.shape
    n1 = 2 * k1
    nb = ch // HY_CB
    f1h = f1[:, :, :k1]
    dat = pl.BlockSpec((2, HY_CB, k1, LANES), lambda j: (0, j, 0, 0))
    return pl.pallas_call(
        _hy_conv_kernel,
        grid=(nb,),
        in_specs=[dat, dat,
                  pl.BlockSpec((2, HY_CB, n1, LANES), lambda j: (0, order * nb + j, 0, 0)),
                  pl.BlockSpec((HY_CB, 1, 1), lambda j: (order * nb + j, 0, 0)),
                  _resident(f1h.shape), _resident(ginv.shape), _resident(tw.shape),
                  _resident(w2.shape), _resident(w2c.shape)],
        out_specs=dat,
        out_shape=jax.ShapeDtypeStruct(v.shape, F32),
        name="hy_conv",
        compiler_params=_cparams("arbitrary"),
    )(v, xg, hf, bias3, f1h, ginv, tw, w2, w2c)


def _hy_ctx_kernel(v_ref, x1_ref, x2_ref, filt_ref, s_ref, bias_ref, f_ref, o_ref, *, n):
    k, big = n // LANES, 2 * n
    fr, fi = f_ref[0], f_ref[1]
    c0, c1 = _lanes_of(v_ref, (0,), k), _lanes_of(v_ref, (1,), k)
    for o, x_ref in enumerate((x1_ref, x2_ref)):
        rows = slice(o * HY_CH, (o + 1) * HY_CH)
        filt = jnp.concatenate([filt_ref[q, rows, :] for q in range(2 * k)], axis=1)
        inv = 1.0 / s_ref[rows, :]
        hr, hi = _mm(filt, fr) * inv, _mm(filt, fi) * inv
        xr = _mm(c0, fr[:n]) - _mm(c1, fi[:n])
        xi = _mm(c0, fi[:n]) + _mm(c1, fr[:n])
        yr, yi = xr * hr - xi * hi, xr * hi + xi * hr
        gr, gi = fr[:, :n] * (1.0 / big), fi[:, :n] * (-1.0 / big)
        cr = _mm(yr, gr) - _mm(yi, gi)
        ci = _mm(yr, gi) + _mm(yi, gr)
        b = bias_ref[rows, :]
        c0 = _lanes_of(x_ref, (0,), k) * (cr + b * c0)
        c1 = _lanes_of(x_ref, (1,), k) * (ci + b * c1)
    for q in range(k):
        o_ref[0, :, q, :] = c0[:, q * LANES:(q + 1) * LANES]
        o_ref[1, :, q, :] = c1[:, q * LANES:(q + 1) * LANES]


def _hyena_ctx(v, x1, x2, filt, norm, bias):
    bsz, ch, k, _ = v.shape
    n = k * LANES
    big = 2 * n
    idx = np.arange(big)
    ang = 2.0 * np.pi * ((idx[:, None] * idx[None, :]) % big) / big
    f = jnp.asarray(np.stack([np.cos(ang), -np.sin(ang)]).astype(np.float32))
    args = (v, x1, x2, filt, norm, bias.reshape(HY_ROWS, 1), f)
    return pl.pallas_call(
        functools.partial(_hy_ctx_kernel, n=n),
        grid=(1,),
        in_specs=[_resident(a.shape) for a in args],
        out_specs=pl.BlockSpec(v.shape, lambda j: (0, 0, 0, 0)),
        out_shape=jax.ShapeDtypeStruct(v.shape, F32),
        name="hy_ctx",
        compiler_params=_cparams("arbitrary"),
    )(*args)


def _hyena_lat(v, x1, x2, filt, norm, bias):
    assert v.shape[0] == 2
    n = v.shape[2] * LANES
    tabs = _dft_tables(n)
    hf = _hy_spectrum(filt, norm, tabs[0], tabs[1], tabs[2])
    bias3 = bias.reshape(HY_ROWS, 1, 1)
    z = _hy_conv(v, x1, hf, bias3, 0, tabs)
    return _hy_conv(z, x2, hf, bias3, 1, tabs)


def _even_layer_mix(xc, xl, mod_c, mod_l, w_in, w_out, hy_p, gla_p, ln_g, ln_b, need_ctx):
    conv_w, conv_b, filt_params, bias = hy_p
    a_w, a_b, gla_norm = gla_p
    wp = w_in[:, HY_COLS:].astype(BF16)
    wc = jnp.transpose(w_in[:, :HY_COLS]).astype(BF16)
    cw, cb = jnp.transpose(conv_w), conv_b.reshape(HY_COLS, 1)
    psplit = (2 * GLA_QK, HALF, HALF, SMALL)
    csplit = (HY_CH, HY_CH, HY_CH)
    pc = _inproj(xc, mod_c, wp, wc, cw, cb, psplit, csplit, False, "row", True, 1024)
    pl_ = _inproj(xl, mod_l, wp, wc, cw, cb, psplit, csplit, False, "row", True, 1024)
    g_c, g_l = _gla_mixer((pc[0], pc[1], pc[3]), (pl_[0], pl_[1], pl_[3]), a_w, a_b)
    filt, norm = _hyena_filter(xl.shape[1], *filt_params)
    h_l = _hyena_lat(pl_[4], pl_[5], pl_[6], filt, norm, bias)
    nw = [gla_norm.reshape(1, GLA_DV)]
    specs = [lambda tm: _chan_spec(HY_CH, tm)] + [lambda tm: _seq_spec(tm, HALF)] * 3
    xl = _outproj(_outproj_even_kernel, [h_l, g_l[0], g_l[1], pl_[2]], specs, xl, mod_l, w_out,
                  ln_g, ln_b, nw, "row", 1024)
    if need_ctx:
        filt_c, norm_c = _hyena_filter(xc.shape[1], *filt_params)
        h_c = _hyena_ctx(pc[4], pc[5], pc[6], filt_c, norm_c, bias)
        xc = _outproj(_outproj_even_kernel, [h_c, g_c[0], g_c[1], pc[2]], specs, xc, mod_c, w_out,
                      ln_g, ln_b, nw, "row", 1024)
    return xc, xl


def _odd_layer_mix(xc, xl, mod_c, mod_l, w_in, w_out, ml_p, ssd_p, ln_g, ln_b, need_ctx):
    gate_b, ml_norm = ml_p
    conv_w, conv_b, dt_bias, a_log, d_skip, ssd_norm = ssd_p
    ml_cols = 2 * ML_QK + 2 * HALF + 4 * ML_HEADS
    o0 = 2 * ML_QK + 2 * HALF
    s0 = ml_cols
    wp = jnp.concatenate([w_in[:, :o0], w_in[:, s0:s0 + HALF], w_in[:, o0:ml_cols],
                          w_in[:, s0 + HALF + SSD_CONV_DIM:]], axis=1).astype(BF16)
    wc = w_in[:, s0 + HALF:s0 + HALF + SSD_CONV_DIM].astype(BF16)
    cb = conv_b.reshape(1, SSD_CONV_DIM)
    psplit = (2 * ML_QK, HALF, HALF, HALF, SMALL)
    csplit = (HALF, SSD_GN, SSD_GN)
    pc = _inproj(xc, mod_c, wp, wc, conv_w, cb, psplit, csplit, True, "row", False, 512)
    pl_ = _inproj(xl, mod_l, wp, wc, conv_w, cb, psplit, csplit, True, "col", False, 512)
    m_c, m_l = _mlstm_mixer((pc[0], pc[1], pc[4]), (pl_[0], pl_[1], pl_[4]), gate_b)
    s_c, s_l = _ssd_mixer((pc[5], pc[6], pc[7], pc[4]), (pl_[5], pl_[6], pl_[7], pl_[4]), dt_bias, a_log)
    consts = [ml_norm.reshape(1, ML_DV), jnp.repeat(d_skip, SSD_HEADDIM).reshape(1, HALF),
              ssd_norm.reshape(1, HALF)]
    specs = [lambda tm: _seq_spec(tm, HALF)] * 7
    xl = _outproj(_outproj_odd_kernel, [m_l[0], m_l[1], pl_[2], s_l[0], s_l[1], pl_[5], pl_[3]], specs,
                  xl, mod_l, w_out, ln_g, ln_b, consts, "col", 512)
    if need_ctx:
        xc = _outproj(_outproj_odd_kernel, [m_c[0], m_c[1], pc[2], s_c[0], s_c[1], pc[5], pc[3]], specs,
                      xc, mod_c, w_out, ln_g, ln_b, consts, "row", 512)
    return xc, xl


def kernel(x, c, ctx, c_ctx, w_ada, b_ada, ln_g, ln_b, w_ffn_in, w_ffn_out, w_in_even, w_out_even, hy_conv_w, hy_conv_b, hy_w1, hy_b1, hy_w2, hy_b2, hy_w3, hy_freq, hy_bias, gla_a_w, gla_a_b, gla_norm_w, w_in_odd, w_out_odd, ml_gate_b, ml_norm_w, ssd_conv_w, ssd_conv_b, ssd_dt_bias, ssd_a_log, ssd_d, ssd_norm_w):
    bsz, n, d = x.shape
    depth = w_ada.shape[0]
    assert bsz == 2 and d == D_MODEL
    cond8 = jnp.zeros((8, d), F32).at[:bsz].set(c).at[bsz].set(c_ctx)
    mod_all = _ada_all(cond8, w_ada, b_ada)
    xl, xc = x, ctx
    for l in range(depth):
        last = l == depth - 1
        mod_l = mod_all[l, :bsz].reshape(bsz, N_MOD, d)
        mod_c = jnp.broadcast_to(mod_all[l, bsz].reshape(1, N_MOD, d), (bsz, N_MOD, d))
        w1a, w2a = w_ffn_in[l, 0].astype(BF16), w_ffn_out[l, 0].astype(BF16)
        w1b, w2b = w_ffn_in[l, 1].astype(BF16), w_ffn_out[l, 1].astype(BF16)
        odd = l % 2 == 1
        xl = _ffn_step(xl, mod_l, 0, w1a, w2a, ln_g[l, 0], ln_b[l, 0], out_cv=odd)
        xc = _ffn_step(xc, mod_c, 0, w1a, w2a, ln_g[l, 0], ln_b[l, 0])
        i = l // 2
        if not odd:
            hy_p = (hy_conv_w[i], hy_conv_b[i],
                    (hy_w1[i], hy_b1[i], hy_w2[i], hy_b2[i], hy_w3[i], hy_freq[i]), hy_bias[i])
            gla_p = (gla_a_w[i], gla_a_b[i], gla_norm_w[i])
            xc, xl = _even_layer_mix(xc, xl, mod_c, mod_l, w_in_even[i], w_out_even[i].astype(BF16),
                                     hy_p, gla_p, ln_g[l, 1], ln_b[l, 1], not last)
        else:
            ml_p = (ml_gate_b[i], ml_norm_w[i])
            ssd_p = (ssd_conv_w[i], ssd_conv_b[i], ssd_dt_bias[i], ssd_a_log[i], ssd_d[i], ssd_norm_w[i])
            xc, xl = _odd_layer_mix(xc, xl, mod_c, mod_l, w_in_odd[i], w_out_odd[i].astype(BF16),
                                    ml_p, ssd_p, ln_g[l, 1], ln_b[l, 1], not last)
        xl = _ffn_step(xl, mod_l, 6, w1b, w2b, ln_g[l, 2], ln_b[l, 2], in_cv=odd)
        if not last:
            xc = _ffn_step(xc, mod_c, 6, w1b, w2b, ln_g[l, 2], ln_b[l, 2])
    return xl
```

```python
import functools
import math

import numpy as np
import jax
import jax.numpy as jnp
from jax import lax
from jax.experimental import pallas as pl
from jax.experimental.pallas import tpu as pltpu

F32 = jnp.float32
BF16 = jnp.bfloat16

D_MODEL = 1024
DEPTH = 4
GRID_W = 64
D_FF = 2816
N_MOD = 9
HALF = D_MODEL // 2
CHUNK = 64
LN_EPS = 1e-5
NORM_EPS = 1e-6

HY_CH = HALF
HY_ORDER = 2
HY_EMB = 33
HY_BANDS = (HY_EMB - 1) // 2
HY_FAST_DECAY = 0.3
HY_SLOW_DECAY = 1.5
HY_TARGET = 1e-2
HY_COLS = 3 * HY_CH
HY_ROWS = HY_ORDER * HY_CH

GLA_HEADS = 4
GLA_DK = HALF // 2 // GLA_HEADS
GLA_DV = HALF // GLA_HEADS
GLA_RANK = 16
GLA_TAU = 16.0
GLA_QK = GLA_HEADS * GLA_DK

ML_HEADS = 4
ML_DQK = HALF // 2 // ML_HEADS
ML_DV = HALF // ML_HEADS
ML_QK = ML_HEADS * ML_DQK

SSD_HEADDIM = 64
SSD_HEADS = HALF // SSD_HEADDIM
SSD_GROUPS = 2
SSD_HPG = SSD_HEADS // SSD_GROUPS
SSD_STATE = 128
SSD_GN = SSD_GROUPS * SSD_STATE
SSD_CONV_DIM = HALF + 2 * SSD_GN

DN_ALPHA = (2.0 * DEPTH) ** 0.25

SMALL = 32
LANES = 128
VMEM_LIMIT = 56 * 1024 * 1024


def _cparams(*sem):
    return pltpu.CompilerParams(dimension_semantics=sem, vmem_limit_bytes=VMEM_LIMIT)


def _mm(a, b):
    return jnp.dot(a.astype(BF16), b.astype(BF16), preferred_element_type=F32)


def _mm_nt(a, b):
    return lax.dot_general(a.astype(BF16), b.astype(BF16), (((1,), (1,)), ((), ())),
                           preferred_element_type=F32)


def _mm_tn(a, b):
    return lax.dot_general(a.astype(BF16), b.astype(BF16), (((0,), (0,)), ((), ())),
                           preferred_element_type=F32)


def _mm_hi(a, b):
    return jnp.dot(a, b, precision=lax.Precision.HIGHEST, preferred_element_type=F32)


def _mm_nt_hi(a, b):
    return lax.dot_general(a, b, (((1,), (1,)), ((), ())), precision=lax.Precision.HIGHEST,
                           preferred_element_type=F32)


def _split3(x):
    x1 = x.astype(BF16)
    r1 = x - x1.astype(F32)
    x2 = r1.astype(BF16)
    return x1, x2, (r1 - x2.astype(F32)).astype(BF16)


def _sel_rows(sel3, x):
    return jnp.dot(sel3.astype(BF16), jnp.concatenate(_split3(x), axis=0), preferred_element_type=F32)


def _sel_cols(x, sel):
    sel = sel.astype(BF16)
    return sum(jnp.dot(p, sel, preferred_element_type=F32) for p in _split3(x))


def _sel_nt(a, b, split_a):
    dn = (((1,), (1,)), ((), ()))
    if split_a:
        return sum(lax.dot_general(p, b.astype(BF16), dn, preferred_element_type=F32) for p in _split3(a))
    return sum(lax.dot_general(a.astype(BF16), p, dn, preferred_element_type=F32) for p in _split3(b))


def _tri3(reverse):
    row = lax.broadcasted_iota(jnp.int32, (CHUNK, 3 * CHUNK), 0)
    col = lax.broadcasted_iota(jnp.int32, (CHUNK, 3 * CHUNK), 1) % CHUNK
    return ((row <= col) if reverse else (row >= col)).astype(F32)


def _silu(x):
    return x * jax.nn.sigmoid(x)


def _log_sigmoid(x):
    return jnp.minimum(x, 0.0) - jnp.log1p(jnp.exp(-jnp.abs(x)))


def _softplus(x):
    return jnp.maximum(x, 0.0) + jnp.log1p(jnp.exp(-jnp.abs(x)))


def _layer_norm(y, g, b):
    yc = y - jnp.mean(y, axis=-1, keepdims=True)
    return yc * lax.rsqrt(jnp.mean(yc * yc, axis=-1, keepdims=True) + LN_EPS) * g + b


def _rms(y):
    return y * lax.rsqrt(jnp.mean(y * y, axis=-1, keepdims=True) + NORM_EPS)


def _tri(reverse):
    row = lax.broadcasted_iota(jnp.int32, (CHUNK, CHUNK), 0)
    col = lax.broadcasted_iota(jnp.int32, (CHUNK, CHUNK), 1)
    return (row <= col) if reverse else (row >= col)


def _eye(n):
    return (lax.broadcasted_iota(jnp.int32, (n, n), 0) == lax.broadcasted_iota(jnp.int32, (n, n), 1)).astype(F32)


def _resident(shape):
    nd = len(shape)
    return pl.BlockSpec(shape, lambda *_: (0,) * nd, pipeline_mode=pl.Buffered(1))


def _cplx_mm(ar, ai, w):
    u, v = _mm(ar, w), _mm(ai, w)
    h = w.shape[1] // 2
    return u[:, :h] - v[:, h:], u[:, h:] + v[:, :h]


def _cplx_mm_left(fr, fi, xr, xi):
    rhs = jnp.concatenate([xr, xi], axis=1)
    p, q = _mm(fr, rhs), _mm(fi, rhs)
    h = xr.shape[1]
    return p[:, :h] - q[:, h:], p[:, h:] + q[:, :h]


def _ada_kernel(c_ref, w_ref, b_ref, o_ref):
    o_ref[0] = _mm(_silu(c_ref[...]), w_ref[0]) + b_ref[0]


def _ada_all(cond8, w_ada, b_ada):
    depth, d, nm = w_ada.shape
    tn = nm // 8
    return pl.pallas_call(
        _ada_kernel,
        grid=(depth, nm // tn),
        in_specs=[pl.BlockSpec((8, d), lambda l, j: (0, 0)),
                  pl.BlockSpec((1, d, tn), lambda l, j: (l, 0, j)),
                  pl.BlockSpec((1, 1, tn), lambda l, j: (l, 0, j))],
        out_specs=pl.BlockSpec((1, 8, tn), lambda l, j: (l, 0, j)),
        out_shape=jax.ShapeDtypeStruct((depth, 8, nm), F32),
        name="ada",
        compiler_params=_cparams("arbitrary", "arbitrary"),
    )(cond8, w_ada, b_ada.reshape(depth, 1, nm))


def _stream_dims(x, layout):
    if layout == "col":
        return x.shape[0], x.shape[1] * GRID_W, x.shape[2] // GRID_W
    return x.shape


COL_TILE = 4


def _tile_rows(n, layout, tm):
    return COL_TILE * (n // GRID_W) if layout == "col" else min(tm, n)


def _main_spec(n, d, layout, tm):
    if layout == "col":
        return pl.BlockSpec((None, n // GRID_W, COL_TILE * d), lambda b, i: (b, 0, i))
    return pl.BlockSpec((None, tm, d), lambda b, i: (b, i, 0))


def _load_tile(x_ref, d):
    cols = x_ref.shape[1] // d
    if cols == 1:
        return x_ref[...]
    return jnp.concatenate([x_ref[:, c * d:(c + 1) * d] for c in range(cols)], axis=0)


def _store_tile(o_ref, y, d):
    cols = o_ref.shape[1] // d
    rows = o_ref.shape[0]
    if cols == 1:
        o_ref[...] = y
    else:
        for c in range(cols):
            o_ref[:, c * d:(c + 1) * d] = y[c * rows:(c + 1) * rows]


def _halo_specs(n, d, layout, tm):
    if layout == "col":
        r8 = n // GRID_W // 8
        prev = pl.BlockSpec((None, 8, d), lambda b, i: (b, r8 - 1, jnp.maximum(COL_TILE * i - 1, 0)))
        nxt = pl.BlockSpec((None, 8, d), lambda b, i: (b, 0, jnp.minimum(COL_TILE * (i + 1), GRID_W - 1)))
    else:
        t8, n8 = tm // 8, n // 8
        prev = pl.BlockSpec((None, 8, d), lambda b, i: (b, jnp.maximum(i * t8 - 1, 0), 0))
        nxt = pl.BlockSpec((None, 8, d), lambda b, i: (b, jnp.minimum((i + 1) * t8, n8 - 1), 0))
    return prev, nxt


def _seq_spec(tm, cols):
    return pl.BlockSpec((None, tm, cols), lambda b, i: (b, i, 0))


def _chan_spec(rows, tm):
    return pl.BlockSpec((None, rows, tm // LANES, LANES), lambda b, i: (b, 0, i, 0))


def _lanes_of(ref, lead, pieces):
    return jnp.concatenate([ref[lead + (slice(None), q, slice(None))] for q in range(pieces)], axis=1)


def _ffn_kernel(x_ref, mod_ref, w1_ref, w2_ref, g_ref, b_ref, o_ref, *, mo, nk, in_cv, out_cv):
    d = w2_ref.shape[1]
    pieces = in_cv or out_cv
    if in_cv:
        x = jnp.concatenate([x_ref[:, c * d:(c + 1) * d] for c in range(GRID_W)], axis=0)
    elif pieces:
        x = pltpu.einshape("rcd->(cr)d", x_ref[...])
    else:
        x = x_ref[...]
    shift, scale, gate = mod_ref[mo:mo + 1, :], mod_ref[mo + 1:mo + 2, :], mod_ref[mo + 2:mo + 3, :]
    h = (x * (1.0 + scale) + shift).astype(BF16)
    tf = D_FF // nk
    acc = jnp.zeros(x.shape, F32)
    for k in range(nk):
        gt = jnp.dot(h, w1_ref[:, k * tf:(k + 1) * tf], preferred_element_type=F32)
        up = jnp.dot(h, w1_ref[:, D_FF + k * tf:D_FF + (k + 1) * tf], preferred_element_type=F32)
        acc = acc + _mm(_silu(gt) * up, w2_ref[k * tf:(k + 1) * tf, :])
    y = _layer_norm(DN_ALPHA * x + (0.5 * gate) * acc, g_ref[...], b_ref[...])
    if out_cv:
        for c in range(GRID_W):
            o_ref[:, c * d:(c + 1) * d] = y[8 * c:8 * c + 8]
    elif pieces:
        o_ref[...] = pltpu.einshape("(cr)d->rcd", y, c=GRID_W)
    else:
        o_ref[...] = y


def _ffn_step(x, mod, mo, w1, w2, g, b, in_cv=False, out_cv=False, tm=512):
    bsz, n, d = _stream_dims(x, "col" if in_cv else "row")
    rows = n // GRID_W
    if in_cv or out_cv:
        tm = 8 * GRID_W
        cv_spec = pl.BlockSpec((None, 8, GRID_W * d), lambda bb, i: (bb, i, 0))
        tok_spec = pl.BlockSpec((None, 8, GRID_W, d), lambda bb, i: (bb, i, 0, 0))
        x = x if in_cv else x.reshape(bsz, rows, GRID_W, d)
        in_spec, out_spec = (cv_spec if in_cv else tok_spec), (cv_spec if out_cv else tok_spec)
        out_dims = (bsz, rows, GRID_W * d) if out_cv else (bsz, rows, GRID_W, d)
    else:
        tm = min(tm, n)
        in_spec = out_spec = _seq_spec(tm, d)
        out_dims = (bsz, n, d)
    out = pl.pallas_call(
        functools.partial(_ffn_kernel, mo=mo, nk=1, in_cv=in_cv, out_cv=out_cv),
        grid=(bsz, n // tm),
        in_specs=[in_spec,
                  pl.BlockSpec((None, N_MOD, d), lambda bb, i: (bb, 0, 0)),
                  _resident(w1.shape), _resident(w2.shape),
                  _resident((1, d)), _resident((1, d))],
        out_specs=out_spec,
        out_shape=jax.ShapeDtypeStruct(out_dims, F32),
        name="ffn",
        compiler_params=_cparams("arbitrary", "arbitrary"),
    )(x, mod, w1, w2, g.reshape(1, d), b.reshape(1, d))
    return out if out_cv else out.reshape(bsz, n, d)


def _inproj_kernel(x_ref, xp_ref, xn_ref, mod_ref, wp_ref, wc_ref, cw_ref, cb_ref, *out_refs,
                   psplit, csplit, act, chan_major):
    i, last = pl.program_id(1), pl.num_programs(1) - 1
    shift, scale = mod_ref[3:4, :], mod_ref[4:5, :]
    x = _load_tile(x_ref, mod_ref.shape[1])
    tm = x.shape[0]
    h = x * (1.0 + scale) + shift
    hb = h.astype(BF16)
    o = 0
    for k, w in enumerate(psplit):
        out_refs[k][...] = jnp.dot(hb, wp_ref[:, o:o + w], preferred_element_type=F32).astype(out_refs[k].dtype)
        o += w
    hp = (xp_ref[...] * (1.0 + scale) + shift) * jnp.where(i > 0, 1.0, 0.0)
    hn = (xn_ref[...] * (1.0 + scale) + shift) * jnp.where(i < last, 1.0, 0.0)
    couts = out_refs[len(psplit):]
    if chan_major:
        lane = lax.broadcasted_iota(jnp.int32, (1, tm), 1)
        halo = jnp.concatenate([hp, hn, jnp.zeros((LANES - 16, hp.shape[1]), F32)], axis=0).astype(BF16)
        o = 0
        for k, w in enumerate(csplit):
            wt = wc_ref[o:o + w, :]
            p = _mm_nt(wt, hb)
            p_halo = _mm_nt(wt, halo)
            up = jnp.where(lane == 0, p_halo[:, 7:8], pltpu.roll(p, 1, 1))
            dn = jnp.where(lane == tm - 1, p_halo[:, 8:9], pltpu.roll(p, tm - 1, 1))
            cw = cw_ref[o:o + w, :]
            y = cw[:, 0:1] * up + cw[:, 1:2] * p + cw[:, 2:3] * dn + cb_ref[o:o + w, :]
            couts[k][...] = y.reshape(w, tm // LANES, LANES)
            o += w
    else:
        pc = _mm(jnp.concatenate([hp, h, hn], axis=0), wc_ref[...])
        up = pltpu.roll(pc, 1, 0)
        dn = pltpu.roll(pc, tm + 15, 0)
        y = (cw_ref[0:1, :] * up + cw_ref[1:2, :] * pc + cw_ref[2:3, :] * dn + cb_ref[...])[8:8 + tm]
        if act:
            y = _silu(y)
        o = 0
        for k, w in enumerate(csplit):
            couts[k][...] = y[:, o:o + w].astype(couts[k].dtype)
            o += w


def _inproj(xv, mod, wp, wc, cw, cb, psplit, csplit, act, layout, chan_major, tm):
    bsz, n, d = _stream_dims(xv, layout)
    tm = _tile_rows(n, layout, tm)
    prev, nxt = _halo_specs(n, d, layout, tm)
    pdt = [BF16 if w >= LANES else F32 for w in psplit]
    out_specs = [_seq_spec(tm, w) for w in psplit]
    out_shape = [jax.ShapeDtypeStruct((bsz, n, w), dt) for w, dt in zip(psplit, pdt)]
    if chan_major:
        out_specs += [_chan_spec(w, tm) for w in csplit]
        out_shape += [jax.ShapeDtypeStruct((bsz, w, n // LANES, LANES), F32) for w in csplit]
    else:
        out_specs += [_seq_spec(tm, w) for w in csplit]
        out_shape += [jax.ShapeDtypeStruct((bsz, n, w), BF16) for w in csplit]
    return pl.pallas_call(
        functools.partial(_inproj_kernel, psplit=tuple(psplit), csplit=tuple(csplit), act=act,
                          chan_major=chan_major),
        grid=(bsz, n // tm),
        in_specs=[_main_spec(n, d, layout, tm), prev, nxt,
                  pl.BlockSpec((None, N_MOD, d), lambda bb, i: (bb, 0, 0)),
                  _resident(wp.shape), _resident(wc.shape), _resident(cw.shape), _resident(cb.shape)],
        out_specs=out_specs,
        out_shape=out_shape,
        name="inproj_chan" if chan_major else "inproj",
        compiler_params=_cparams("arbitrary", "arbitrary"),
    )(xv, xv, xv, mod, wp, wc, cw, cb)


def _outproj_even_kernel(hy_ref, of_ref, ob_ref, g_ref, x_ref, mod_ref, w_ref, lg_ref, lb_ref, nw_ref, o_ref):
    tm = x_ref.shape[0]
    y = _mm_tn(hy_ref[...].reshape(HY_CH, tm), w_ref[0:HALF, :])
    o = of_ref[...].astype(F32) + ob_ref[...].astype(F32)
    g = g_ref[...].astype(F32)
    heads = [_rms(o[:, h * GLA_DV:(h + 1) * GLA_DV]) * nw_ref[...] for h in range(GLA_HEADS)]
    y = y + _mm(_silu(g) * jnp.concatenate(heads, axis=1), w_ref[HALF:2 * HALF, :])
    o_ref[...] = _layer_norm(DN_ALPHA * x_ref[...] + mod_ref[5:6, :] * y, lg_ref[...], lb_ref[...])


def _outproj_odd_kernel(hf_ref, hb_ref, og_ref, yf_ref, yb_ref, xs_ref, z_ref, x_ref, mod_ref, w_ref,
                        lg_ref, lb_ref, mnw_ref, dsk_ref, snw_ref, o_ref):
    h = hf_ref[...].astype(F32) + hb_ref[...].astype(F32)
    heads = []
    for k in range(ML_HEADS):
        hk = h[:, k * ML_DV:(k + 1) * ML_DV]
        hc = hk - jnp.mean(hk, axis=-1, keepdims=True)
        heads.append(hc * lax.rsqrt(jnp.mean(hc * hc, axis=-1, keepdims=True) + NORM_EPS) * mnw_ref[...])
    m = jax.nn.sigmoid(og_ref[...].astype(F32)) * jnp.concatenate(heads, axis=1)
    s = yf_ref[...].astype(F32) + yb_ref[...].astype(F32) + dsk_ref[...] * xs_ref[...].astype(F32)
    s = _rms(s * _silu(z_ref[...].astype(F32))) * snw_ref[...]
    y = _mm(m, w_ref[0:HALF, :]) + _mm(s, w_ref[HALF:2 * HALF, :])
    d = mod_ref.shape[1]
    x = _load_tile(x_ref, d)
    _store_tile(o_ref, _layer_norm(DN_ALPHA * x + mod_ref[5:6, :] * y, lg_ref[...], lb_ref[...]), d)


def _outproj(kern, parts, part_specs, xv, mod, w, g, b, consts, layout, tm):
    bsz, n, d = _stream_dims(xv, layout)
    tm = _tile_rows(n, layout, tm)
    consts = [g.reshape(1, d), b.reshape(1, d)] + consts
    return pl.pallas_call(
        kern,
        grid=(bsz, n // tm),
        in_specs=[s(tm) for s in part_specs] + [_main_spec(n, d, layout, tm),
                  pl.BlockSpec((None, N_MOD, d), lambda bb, i: (bb, 0, 0)), _resident(w.shape)]
        + [_resident(c.shape) for c in consts],
        out_specs=_main_spec(n, d, layout, tm),
        out_shape=jax.ShapeDtypeStruct(xv.shape, F32),
        name=kern.__name__.strip("_"),
        compiler_params=_cparams("arbitrary", "arbitrary"),
    )(*parts, xv, mod, w, *consts)


def _scan_call(kernel, seqs, consts, states, out_cols, tb=512):
    bsz, n, _ = seqs[0].shape
    tb = min(tb, n)
    nblk = n // tb

    def fmap(i):
        return (0, i, 0)

    def bmap(i):
        return (0, nblk - 1 - i, 0)

    in_specs = [pl.BlockSpec((bsz, tb, s.shape[-1]), fmap) for s in seqs]
    in_specs += [pl.BlockSpec((bsz, tb, s.shape[-1]), bmap) for s in seqs]
    in_specs += [_resident(c.shape) for c in consts]
    st_specs = [pl.BlockSpec(s.shape, lambda i, nd=s.ndim: (0,) * nd) for s in states]
    outs = pl.pallas_call(
        functools.partial(kernel, nchunk=tb // CHUNK),
        grid=(nblk,),
        in_specs=in_specs + st_specs,
        out_specs=[pl.BlockSpec((bsz, tb, out_cols), fmap), pl.BlockSpec((bsz, tb, out_cols), bmap)] + st_specs,
        out_shape=[jax.ShapeDtypeStruct((bsz, n, out_cols), BF16)] * 2
        + [jax.ShapeDtypeStruct(s.shape, F32) for s in states],
        scratch_shapes=[pltpu.VMEM(s.shape, F32) for s in states],
        name=kernel.__name__.strip("_"),
        compiler_params=_cparams("arbitrary"),
    )(*seqs, *seqs, *consts, *states)
    return outs[0], outs[1], tuple(outs[2:])


def _chunk_rows(ci, nchunk, reverse):
    c = (nchunk - 1 - ci) if reverse else ci
    return pl.ds(pl.multiple_of(c * CHUNK, CHUNK), CHUNK)


def _chains(ci, nchunk, bsz, unroll):
    return [(d, b, _chunk_rows(ci * unroll + u, nchunk, d == 1))
            for u in range(unroll) for d in range(2) for b in range(bsz)]


def _interleave(gens):
    results = [None] * len(gens)
    live = list(enumerate(gens))
    while live:
        still = []
        for idx, g in live:
            try:
                next(g)
                still.append((idx, g))
            except StopIteration as stop:
                results[idx] = stop.value
        live = still
    return results


def _scan_states(i, s0_refs, sout_refs, scrs, body, trips):
    @pl.when(i == 0)
    def _():
        for s0, scr in zip(s0_refs, scrs):
            scr[...] = s0[...]

    lax.fori_loop(0, trips, body, 0)

    @pl.when(i == pl.num_programs(0) - 1)
    def _():
        for so, scr in zip(sout_refs, scrs):
            so[...] = scr[...]


def _gla_chunk(qk, v, a, aw, ab, state, tri, head):
    heads = range(GLA_HEADS)
    q = qk[:, 0:GLA_QK].astype(F32) * (GLA_DK ** -0.5)
    k = qk[:, GLA_QK:2 * GLA_QK].astype(F32)
    tri, tri3 = tri
    lr = _mm(a, aw)
    yield
    log_a = _log_sigmoid(lr + ab) * (1.0 / GLA_TAU)
    b = _sel_rows(tri3, log_a)
    yield
    b_last = jnp.sum(log_a, axis=0, keepdims=True)
    q_dec = q * jnp.exp(b)
    k_inv = k * jnp.exp(-b)
    k_dec = k * jnp.exp(b_last - b)
    vs = [v[:, h * GLA_DV:(h + 1) * GLA_DV] for h in heads]
    q4 = jnp.concatenate([jnp.where(head == h, q_dec, 0.0) for h in heads], axis=0).astype(BF16)
    att4 = _mm_nt(q4, k_inv)
    kv = _mm_tn(jnp.concatenate(vs, axis=0),
                jnp.concatenate([jnp.where(head == h, k_dec, 0.0) for h in heads], axis=0))
    yield
    s_old = state[0]
    inter4 = _mm_nt(q4, s_old)
    state[0] = s_old * jnp.exp(b_last) + kv
    outs = [_mm(jnp.where(tri, att4[h * CHUNK:(h + 1) * CHUNK], 0.0), vs[h]) for h in heads]
    yield
    return [outs[h] + inter4[h * CHUNK:(h + 1) * CHUNK] for h in heads]


def _gla_kernel(qkf_ref, vf_ref, af_ref, qkb_ref, vb_ref, ab_ref, aw_ref, abias_ref, s0_ref,
                of_ref, ob_ref, sout_ref, s_scr, *, nchunk):
    head = lax.broadcasted_iota(jnp.int32, (1, GLA_QK), 1) // GLA_DK
    tris = tuple((_tri(rev), _tri3(rev)) for rev in (False, True))
    dirs = ((qkf_ref, vf_ref, af_ref, of_ref), (qkb_ref, vb_ref, ab_ref, ob_ref))
    bsz, unroll = qkf_ref.shape[0], 4

    def body(ci, carry):
        chains = _chains(ci, nchunk, bsz, unroll)
        states = {(d, b): [s_scr[d, b]] for d in range(2) for b in range(bsz)}
        done = _interleave([_gla_chunk(dirs[d][0][b, rows, :], dirs[d][1][b, rows, :], dirs[d][2][b, rows, :],
                                       aw_ref[d], abias_ref[d], states[d, b], tris[d], head)
                            for d, b, rows in chains])
        for (d, b, rows), outs in zip(chains, done):
            for h, oh in enumerate(outs):
                dirs[d][3][b, rows, h * GLA_DV:(h + 1) * GLA_DV] = oh.astype(BF16)
        for (d, b), st in states.items():
            s_scr[d, b] = st[0]
        return carry

    _scan_states(pl.program_id(0), [s0_ref], [sout_ref], [s_scr], body, nchunk // unroll)


def _gla_mixer(ctx_p, lat_p, a_w, a_b):
    bsz = ctx_p[0].shape[0]
    aw = jnp.zeros((2, SMALL, GLA_QK), F32)
    for d in range(2):
        aw = aw.at[d, d * GLA_RANK:(d + 1) * GLA_RANK].set(a_w[d])
    ab = a_b.reshape(2, 1, GLA_QK)
    st = (jnp.zeros((2, bsz, GLA_DV, GLA_QK), F32),)
    cf, cb, st = _scan_call(_gla_kernel, list(ctx_p), [aw, ab], st, HALF)
    lf, lb, _ = _scan_call(_gla_kernel, list(lat_p), [aw, ab], st, HALF)
    return (cf, cb), (lf, lb)


def _mlstm_chunk(qk, v, gates, state, tri, head, eye, shift, spread, d):
    q = qk[:, 0:ML_QK].astype(F32)
    k = qk[:, ML_QK:2 * ML_QK].astype(F32) * (ML_DQK ** -0.5)
    tri, tri3 = tri
    heads = range(ML_HEADS)
    trif = tri.astype(F32)
    gates_t = _mm_nt_hi(eye, gates)
    lf = _log_sigmoid(gates)
    b_col = _mm_hi(trif, lf)
    yield
    b_row = _mm_nt_hi(_log_sigmoid(gates_t), trif)
    b_tot = jnp.sum(lf, axis=0, keepdims=True)
    vs = [v[:, h * ML_DV:(h + 1) * ML_DV] for h in heads]
    qms = [jnp.where(head == h, q, 0.0) for h in heads]
    s_raw = [_mm_nt(qm, k) for qm in qms]
    c_old, n_all, m_all = state
    n_old = n_all[0:1, :]
    inter = [_mm_nt(qm, c_old) for qm in qms]
    m_old = m_all[0:1, :]
    a = b_tot - b_col + _sel_cols(gates, shift)
    a_max = jnp.max(a, axis=0, keepdims=True)
    m_new = jnp.maximum(b_tot + m_old, a_max)
    w_old, w_new = jnp.exp(b_tot + m_old - m_new), jnp.exp(a_max - m_new)
    kw_all = k * _mm(jnp.exp(a - a_max) * w_new, spread)
    keep = _sel_cols(jnp.broadcast_to(w_old, (8, SMALL)), spread)[0:1, :]
    n_add = jnp.sum(kw_all, axis=0, keepdims=True)
    c_add = _mm_tn(jnp.concatenate(vs, axis=0),
                   jnp.concatenate([jnp.where(head == h, kw_all, 0.0) for h in heads], axis=0))
    state[:] = [c_old * keep + c_add, jnp.broadcast_to(n_old * keep + n_add, n_all.shape),
                jnp.broadcast_to(m_new, m_all.shape)]
    yield
    ms, es, ss = [], [], []
    for h in heads:
        ic, fc = d * 8 + h, d * 8 + 4 + h
        bc = b_col[:, fc:fc + 1]
        r_vis = jnp.where(tri, gates_t[ic:ic + 1, :] - b_row[fc:fc + 1, :], -jnp.inf)
        g = bc + m_all[0:1, fc:fc + 1]
        m = jnp.maximum(g, bc + jnp.max(r_vis, axis=1, keepdims=True))
        ms.append(m)
        es.append(jnp.exp(g - m))
        ss.append(s_raw[h] * jnp.exp(r_vis + (bc - m)))
    nums = [_mm(ss[h], vs[h]) for h in heads]
    yield
    outs = []
    for h in heads:
        num = nums[h] + es[h] * inter[h]
        den = jnp.sum(ss[h], axis=1, keepdims=True) + es[h] * jnp.sum(qms[h] * n_old, axis=1, keepdims=True)
        outs.append(num / jnp.maximum(jnp.abs(den), jnp.exp(-ms[h])))
    return outs


def _mlstm_kernel(qkf_ref, vf_ref, gf_ref, qkb_ref, vb_ref, gb_ref, bias_ref, c0_ref, n0_ref, m0_ref,
                  hf_ref, hb_ref, cout_ref, nout_ref, mout_ref, c_scr, n_scr, m_scr, *, nchunk):
    head = lax.broadcasted_iota(jnp.int32, (1, ML_QK), 1) // ML_DQK
    col_r = lax.broadcasted_iota(jnp.int32, (SMALL, SMALL), 0)
    col_c = lax.broadcasted_iota(jnp.int32, (SMALL, SMALL), 1)
    shift = ((col_c == col_r + ML_HEADS) & (col_r % 8 < ML_HEADS)).astype(F32)
    row = lax.broadcasted_iota(jnp.int32, (SMALL, ML_QK), 0)
    lane_head = lax.broadcasted_iota(jnp.int32, (SMALL, ML_QK), 1) // ML_DQK
    spreads = [(row == d * 8 + ML_HEADS + lane_head).astype(F32) for d in range(2)]
    eye = _eye(SMALL)
    tris = tuple((_tri(rev), _tri3(rev)) for rev in (False, True))
    dirs = ((qkf_ref, vf_ref, gf_ref, hf_ref), (qkb_ref, vb_ref, gb_ref, hb_ref))
    bsz, unroll = qkf_ref.shape[0], 2

    def body(ci, carry):
        chains = _chains(ci, nchunk, bsz, unroll)
        states = {(d, b): [c_scr[d, b], n_scr[d, b], m_scr[d, b]] for d in range(2) for b in range(bsz)}
        done = _interleave([_mlstm_chunk(
            dirs[d][0][b, rows, :], dirs[d][1][b, rows, :], dirs[d][2][b, rows, :] + bias_ref[...],
            states[d, b], tris[d], head, eye, shift, spreads[d], d) for d, b, rows in chains])
        for (d, b, rows), outs in zip(chains, done):
            for h, oh in enumerate(outs):
                dirs[d][3][b, rows, h * ML_DV:(h + 1) * ML_DV] = oh.astype(BF16)
        for (d, b), st in states.items():
            c_scr[d, b], n_scr[d, b], m_scr[d, b] = st
        return carry

    _scan_states(pl.program_id(0), [c0_ref, n0_ref, m0_ref], [cout_ref, nout_ref, mout_ref],
                 [c_scr, n_scr, m_scr], body, nchunk // unroll)


def _mlstm_mixer(ctx_p, lat_p, gate_b):
    bsz = ctx_p[0].shape[0]
    st = (jnp.zeros((2, bsz, ML_DV, ML_QK), F32), jnp.zeros((2, bsz, 8, ML_QK), F32),
          jnp.zeros((2, bsz, 8, SMALL), F32))
    gb = jnp.zeros((1, SMALL), F32).at[0, :16].set(gate_b.reshape(16))
    cf, cb, st = _scan_call(_mlstm_kernel, list(ctx_p), [gb], st, HALF)
    lf, lb, _ = _scan_call(_mlstm_kernel, list(lat_p), [gb], st, HALF)
    return (cf, cb), (lf, lb)


def _ssd_chunk(x, bm, cm, raw, alr, alc, expand, state, tri, sub, eye, d):
    x, bm, cm = x.astype(F32), bm.astype(F32), cm.astype(F32)
    gw = SSD_HPG * SSD_HEADDIM
    tri, tri3 = tri
    groups = range(SSD_GROUPS)
    lns = [slice(g * gw, (g + 1) * gw) for g in groups]
    cms = [cm[:, g * SSD_STATE:(g + 1) * SSD_STATE] for g in groups]
    bms = [bm[:, g * SSD_STATE:(g + 1) * SSD_STATE] for g in groups]
    raw_t = _sel_nt(eye, raw, False)
    dt = _softplus(raw)
    da = dt * (-jnp.exp(alr))
    cum = _sel_rows(tri3, da)
    cbs = [_mm_nt(cms[g], bms[g]) for g in groups]
    yield
    dt_t = _softplus(raw_t)
    cum_t = _sel_nt(dt_t * (-jnp.exp(alc)), tri, True)
    cl = jnp.sum(da, axis=0, keepdims=True)
    e_cum = _sel_cols(jnp.exp(cum), expand)
    wgt = _sel_cols(jnp.exp(cl - cum) * dt, expand)
    dec = _sel_cols(jnp.broadcast_to(jnp.exp(cl), (8, SMALL)), expand)[0:1, :]
    yield
    xw = x * wgt
    kvs = [_mm_tn(bms[g], xw[:, lns[g]]) for g in groups]
    diags = []
    for g in groups:
        xg = x[:, lns[g]]
        mixes = []
        for j in range(SSD_HPG):
            col = 16 + d * 8 + g * SSD_HPG + j
            seg = jnp.exp(jnp.where(tri, cum[:, col:col + 1] - cum_t[col:col + 1, :], -jnp.inf))
            mixes.append(cbs[g] * seg * dt_t[col:col + 1, :])
        diags.append(_mm(jnp.concatenate(mixes, axis=1),
                         jnp.concatenate([jnp.where(sub == j, xg, 0.0) for j in range(SSD_HPG)], axis=0)))
    yield
    offs = [_mm(cms[g], state[g]) for g in groups]
    state[:] = [state[g] * dec[:, lns[g]] + kvs[g] for g in groups]
    yield
    return [offs[g] * e_cum[:, lns[g]] + diags[g] for g in groups]


def _ssd_kernel(xf_ref, bmf_ref, cmf_ref, rf_ref, xb_ref, bmb_ref, cmb_ref, rb_ref,
                dtb_ref, alr_ref, alc_ref, ex_ref, s0_ref, yf_ref, yb_ref, sout_ref, s_scr, *, nchunk):
    gw = SSD_HPG * SSD_HEADDIM
    sub = lax.broadcasted_iota(jnp.int32, (1, gw), 1) // SSD_HEADDIM
    eye = _eye(SMALL)
    tris = tuple((_tri(rev), _tri3(rev)) for rev in (False, True))
    dirs = ((xf_ref, bmf_ref, cmf_ref, rf_ref, yf_ref), (xb_ref, bmb_ref, cmb_ref, rb_ref, yb_ref))

    bsz, unroll = xf_ref.shape[0], 4

    def body(ci, carry):
        chains = _chains(ci, nchunk, bsz, unroll)
        states = {(d, b): [s_scr[d, b, :, g * gw:(g + 1) * gw] for g in range(SSD_GROUPS)]
                  for d in range(2) for b in range(bsz)}
        done = _interleave([_ssd_chunk(
            dirs[d][0][b, rows, :], dirs[d][1][b, rows, :], dirs[d][2][b, rows, :],
            dirs[d][3][b, rows, :] + dtb_ref[...], alr_ref[...], alc_ref[...],
            ex_ref[d], states[d, b], tris[d], sub, eye, d) for d, b, rows in chains])
        for (d, b, rows), ys in zip(chains, done):
            for g in range(SSD_GROUPS):
                dirs[d][4][b, rows, g * gw:(g + 1) * gw] = ys[g].astype(BF16)
        for (d, b), st in states.items():
            for g in range(SSD_GROUPS):
                s_scr[d, b, :, g * gw:(g + 1) * gw] = st[g]
        return carry

    _scan_states(pl.program_id(0), [s0_ref], [sout_ref], [s_scr], body, nchunk // unroll)


def _ssd_mixer(ctx_p, lat_p, dt_bias, a_log):
    bsz = ctx_p[0].shape[0]
    st = (jnp.zeros((2, bsz, SSD_STATE, HALF), F32),)
    dtb = jnp.zeros((1, SMALL), F32).at[0, 16:].set(dt_bias.reshape(16))
    al = jnp.zeros((1, SMALL), F32).at[0, 16:].set(a_log.reshape(16).astype(F32))
    lane_head = np.arange(HALF) // SSD_HEADDIM
    ex = jnp.asarray(np.stack([(np.arange(SMALL)[:, None] == (16 + d * 8 + lane_head)[None, :])
                               for d in range(2)]).astype(np.float32))
    consts = [dtb, al, al.reshape(SMALL, 1), ex]
    cf, cb, st = _scan_call(_ssd_kernel, list(ctx_p), consts, st, HALF)
    lf, lb, _ = _scan_call(_ssd_kernel, list(lat_p), consts, st, HALF)
    return (cf, cb), (lf, lb)


def _hy_filter_kernel(f_ref, w1t_ref, w1c_ref, w1s_ref, b1_ref, w2_ref, b2_ref, fq1_ref, fq2_ref, w3s_ref, w3b_ref,
                      dl_ref, o_ref, s_ref, *, n, npiece):
    side, jt = pl.program_id(0), pl.program_id(1)
    tnorm = 1.0 / max(n - 1, 1)
    first = (side == 0) & (jt == 0)
    width = npiece * LANES

    @pl.when(first)
    def _():
        s_ref[...] = jnp.zeros(s_ref.shape, F32)

    def mlp(pos):
        ang = pos * f_ref[...] * (2.0 * math.pi / n)
        pre = (pos * tnorm) * w1t_ref[...] + _mm_hi(w1c_ref[...], jnp.cos(ang)) \
            + _mm_hi(w1s_ref[...], -jnp.sin(ang)) + b1_ref[...]
        hid = jnp.sin(fq1_ref[...] * pre)
        return jnp.sin(fq2_ref[...] * (_mm_hi(w2_ref[...], hid) + b2_ref[...]))

    h00 = _mm(w3b_ref[...], mlp(jnp.zeros((1, LANES), F32)))[:, 0:1]
    tau = jt * width + lax.broadcasted_iota(jnp.int32, (1, width), 1)
    pos = (side * n + (1 - 2 * side) * tau).astype(F32)
    ht = _mm(w3s_ref[...], mlp(pos))
    ht = ht * jnp.exp(-(pos * tnorm) * dl_ref[...])
    ht = jnp.where((tau == 0) & (side == 1), 0.0, ht)
    s_ref[...] += jnp.sum(jnp.abs(ht), axis=1, keepdims=True) + jnp.where(first, jnp.abs(h00), 0.0)
    ht = ht + jnp.where((tau == 0) & (side == 0), h00, 0.0)
    for q in range(npiece):
        o_ref[q] = ht[:, q * LANES:(q + 1) * LANES]


def _hyena_filter(n, w1, b1, w2, b2, w3, freq):
    kk = n // LANES
    npiece = min(8, kk)
    ff = w1.shape[1]
    pad = LANES - ff
    bands = jnp.linspace(1e-4, HY_BANDS - 1, HY_BANDS, dtype=F32).reshape(HY_BANDS, 1)
    deltas = jnp.abs(jnp.linspace(math.log(HY_TARGET) / HY_SLOW_DECAY, math.log(HY_TARGET) / HY_FAST_DECAY,
                                  HY_CH, dtype=F32))
    dl = jnp.tile(deltas, HY_ORDER).reshape(HY_ROWS, 1)
    w3t = jnp.transpose(w3.reshape(ff, HY_ORDER, 2, HY_CH), (2, 1, 3, 0)).reshape(2, HY_ROWS, ff)
    w3t = jnp.pad(w3t, ((0, 0), (0, 0), (0, pad)))
    args = (bands, w1[0].reshape(ff, 1), jnp.transpose(w1[1:1 + HY_BANDS]), jnp.transpose(w1[1 + HY_BANDS:]),
            b1.reshape(ff, 1), jnp.pad(jnp.transpose(w2), ((0, pad), (0, 0))),
            jnp.pad(b2.reshape(ff, 1), ((0, pad), (0, 0))),
            freq[0].reshape(ff, 1), jnp.pad(freq[1].reshape(ff, 1), ((0, pad), (0, 0))))
    nt = kk // npiece
    return pl.pallas_call(
        functools.partial(_hy_filter_kernel, n=n, npiece=npiece),
        grid=(2, nt),
        in_specs=[_resident(a.shape) for a in args]
        + [pl.BlockSpec((None, HY_ROWS, LANES), lambda sd, j: (sd, 0, 0)),
           pl.BlockSpec((None, HY_ROWS, LANES), lambda sd, j: (1, 0, 0)), _resident(dl.shape)],
        out_specs=[pl.BlockSpec((npiece, HY_ROWS, LANES), lambda sd, j: (sd * nt + j, 0, 0)),
                   pl.BlockSpec((HY_ROWS, 1), lambda sd, j: (0, 0))],
        out_shape=[jax.ShapeDtypeStruct((2 * kk, HY_ROWS, LANES), F32),
                   jax.ShapeDtypeStruct((HY_ROWS, 1), F32)],
        name="hy_filter",
        compiler_params=_cparams("arbitrary", "arbitrary"),
    )(*args, w3t, w3t, dl)


def _dft_tables(n):
    big = 2 * n
    n1 = big // LANES
    k = np.arange(n1)

    def cis(num, den):
        ang = 2.0 * np.pi * (num % den) / den
        return np.stack([np.cos(ang), -np.sin(ang)]).astype(np.float32)

    f1 = cis(k[:, None] * k[None, :], n1)
    k2 = np.arange(LANES)
    f2 = cis(k2[:, None] * k2[None, :], LANES)
    tw = cis(k[:, None] * k2[None, :], big)
    w2 = np.concatenate([f2[0], f2[1]], axis=1)
    w2c = np.concatenate([f2[0], -f2[1]], axis=1)
    ginv = np.stack([f1[0, :n1 // 2, :], -f1[1, :n1 // 2, :]]) / big
    return tuple(jnp.asarray(a) for a in (f1, tw, w2, w2c, ginv))


HY_CB = 8
HY_GRP = 4


def _hy_spec_kernel(f_ref, s_ref, f1_ref, tw_ref, w2_ref, o_ref):
    f1r, f1i = f1_ref[0], f1_ref[1]
    twr, twi = tw_ref[0], tw_ref[1]
    n1 = f1r.shape[0]
    flat = f_ref[...].reshape(n1, HY_CB * LANES)
    for c0 in range(0, HY_CB, HY_GRP):
        trs, tis = [], []
        for ca in range(c0, c0 + HY_GRP, 2):
            rhs = flat[:, ca * LANES:(ca + 2) * LANES]
            pr, pi = _mm(f1r, rhs), _mm(f1i, rhs)
            for e in range(2):
                ar, ai = pr[:, e * LANES:(e + 1) * LANES], pi[:, e * LANES:(e + 1) * LANES]
                trs.append(ar * twr - ai * twi)
                tis.append(ar * twi + ai * twr)
        xr, xi = _cplx_mm(jnp.concatenate(trs, axis=0), jnp.concatenate(tis, axis=0), w2_ref[...])
        for e in range(HY_GRP):
            inv = 1.0 / s_ref[c0 + e]
            o_ref[0, c0 + e] = xr[e * n1:(e + 1) * n1] * inv
            o_ref[1, c0 + e] = xi[e * n1:(e + 1) * n1] * inv


def _hy_spectrum(filt, norm, f1, tw, w2):
    n1, rows, _ = filt.shape
    return pl.pallas_call(
        _hy_spec_kernel,
        grid=(rows // HY_CB,),
        in_specs=[pl.BlockSpec((n1, HY_CB, LANES), lambda j: (0, j, 0)),
                  pl.BlockSpec((HY_CB, 1, 1), lambda j: (j, 0, 0)),
                  _resident(f1.shape), _resident(tw.shape), _resident(w2.shape)],
        out_specs=pl.BlockSpec((2, HY_CB, n1, LANES), lambda j: (0, j, 0, 0)),
        out_shape=jax.ShapeDtypeStruct((2, rows, n1, LANES), F32),
        name="hy_spectrum",
        compiler_params=_cparams("arbitrary"),
    )(filt, norm.reshape(rows, 1, 1), f1, tw, w2)


def _hy_conv_kernel(v_ref, x_ref, hf_ref, bias_ref, f1_ref, g_ref, tw_ref, w2_ref, w2c_ref, o_ref):
    f1r, f1i = f1_ref[0], f1_ref[1]
    gr, gi = g_ref[0], g_ref[1]
    twr, twi = tw_ref[0], tw_ref[1]
    n1 = f1r.shape[0]
    for c0 in range(0, HY_CB, HY_GRP):
        vs = [(v_ref[0, c], v_ref[1, c]) for c in range(c0, c0 + HY_GRP)]
        trs, tis = [], []
        for v0, v1 in vs:
            ar, ai = _cplx_mm_left(f1r, f1i, v0, v1)
            trs.append(ar * twr - ai * twi)
            tis.append(ar * twi + ai * twr)
        xr, xi = _cplx_mm(jnp.concatenate(trs, axis=0), jnp.concatenate(tis, axis=0), w2_ref[...])
        hr = hf_ref[0, c0:c0 + HY_GRP].reshape(HY_GRP * n1, LANES)
        hi = hf_ref[1, c0:c0 + HY_GRP].reshape(HY_GRP * n1, LANES)
        zr, zi = _cplx_mm(xr * hr - xi * hi, xr * hi + xi * hr, w2c_ref[...])
        for e, (v0, v1) in enumerate(vs):
            zre, zie = zr[e * n1:(e + 1) * n1], zi[e * n1:(e + 1) * n1]
            cr, ci = _cplx_mm_left(gr, gi, zre * twr + zie * twi, zie * twr - zre * twi)
            bz = bias_ref[c0 + e]
            o_ref[0, c0 + e] = x_ref[0, c0 + e] * (cr + bz * v0)
            o_ref[1, c0 + e] = x_ref[1, c0 + e] * (ci + bz * v1)


def _hy_conv(v, xg, hf, bias3, order, tabs):
    f1, tw, w2, w2c, ginv = tabs
    bsz, ch, k1, _ = v.shape
    n1 = 2 * k1
    nb = ch // HY_CB
    f1h = f1[:, :, :k1]
    dat = pl.BlockSpec((2, HY_CB, k1, LANES), lambda j: (0, j, 0, 0))
    return pl.pallas_call(
        _hy_conv_kernel,
        grid=(nb,),
        in_specs=[dat, dat,
                  pl.BlockSpec((2, HY_CB, n1, LANES), lambda j: (0, order * nb + j, 0, 0)),
                  pl.BlockSpec((HY_CB, 1, 1), lambda j: (order * nb + j, 0, 0)),
                  _resident(f1h.shape), _resident(ginv.shape), _resident(tw.shape),
                  _resident(w2.shape), _resident(w2c.shape)],
        out_specs=dat,
        out_shape=jax.ShapeDtypeStruct(v.shape, F32),
        name="hy_conv",
        compiler_params=_cparams("arbitrary"),
    )(v, xg, hf, bias3, f1h, ginv, tw, w2, w2c)


def _hy_ctx_kernel(v_ref, x1_ref, x2_ref, filt_ref, s_ref, bias_ref, f_ref, o_ref, *, n):
    k, big = n // LANES, 2 * n
    fr, fi = f_ref[0], f_ref[1]
    c0, c1 = _lanes_of(v_ref, (0,), k), _lanes_of(v_ref, (1,), k)
    for o, x_ref in enumerate((x1_ref, x2_ref)):
        rows = slice(o * HY_CH, (o + 1) * HY_CH)
        filt = jnp.concatenate([filt_ref[q, rows, :] for q in range(2 * k)], axis=1)
        inv = 1.0 / s_ref[rows, :]
        hr, hi = _mm(filt, fr) * inv, _mm(filt, fi) * inv
        xr = _mm(c0, fr[:n]) - _mm(c1, fi[:n])
        xi = _mm(c0, fi[:n]) + _mm(c1, fr[:n])
        yr, yi = xr * hr - xi * hi, xr * hi + xi * hr
        gr, gi = fr[:, :n] * (1.0 / big), fi[:, :n] * (-1.0 / big)
        cr = _mm(yr, gr) - _mm(yi, gi)
        ci = _mm(yr, gi) + _mm(yi, gr)
        b = bias_ref[rows, :]
        c0 = _lanes_of(x_ref, (0,), k) * (cr + b * c0)
        c1 = _lanes_of(x_ref, (1,), k) * (ci + b * c1)
    for q in range(k):
        o_ref[0, :, q, :] = c0[:, q * LANES:(q + 1) * LANES]
        o_ref[1, :, q, :] = c1[:, q * LANES:(q + 1) * LANES]


def _hyena_ctx(v, x1, x2, filt, norm, bias):
    bsz, ch, k, _ = v.shape
    n = k * LANES
    big = 2 * n
    idx = np.arange(big)
    ang = 2.0 * np.pi * ((idx[:, None] * idx[None, :]) % big) / big
    f = jnp.asarray(np.stack([np.cos(ang), -np.sin(ang)]).astype(np.float32))
    args = (v, x1, x2, filt, norm, bias.reshape(HY_ROWS, 1), f)
    return pl.pallas_call(
        functools.partial(_hy_ctx_kernel, n=n),
        grid=(1,),
        in_specs=[_resident(a.shape) for a in args],
        out_specs=pl.BlockSpec(v.shape, lambda j: (0, 0, 0, 0)),
        out_shape=jax.ShapeDtypeStruct(v.shape, F32),
        name="hy_ctx",
        compiler_params=_cparams("arbitrary"),
    )(*args)


def _hyena_lat(v, x1, x2, filt, norm, bias):
    assert v.shape[0] == 2
    n = v.shape[2] * LANES
    tabs = _dft_tables(n)
    hf = _hy_spectrum(filt, norm, tabs[0], tabs[1], tabs[2])
    bias3 = bias.reshape(HY_ROWS, 1, 1)
    z = _hy_conv(v, x1, hf, bias3, 0, tabs)
    return _hy_conv(z, x2, hf, bias3, 1, tabs)


def _even_layer_mix(xc, xl, mod_c, mod_l, w_in, w_out, hy_p, gla_p, ln_g, ln_b, need_ctx):
    conv_w, conv_b, filt_params, bias = hy_p
    a_w, a_b, gla_norm = gla_p
    wp = w_in[:, HY_COLS:].astype(BF16)
    wc = jnp.transpose(w_in[:, :HY_COLS]).astype(BF16)
    cw, cb = jnp.transpose(conv_w), conv_b.reshape(HY_COLS, 1)
    psplit = (2 * GLA_QK, HALF, HALF, SMALL)
    csplit = (HY_CH, HY_CH, HY_CH)
    pc = _inproj(xc, mod_c, wp, wc, cw, cb, psplit, csplit, False, "row", True, 1024)
    pl_ = _inproj(xl, mod_l, wp, wc, cw, cb, psplit, csplit, False, "row", True, 1024)
    g_c, g_l = _gla_mixer((pc[0], pc[1], pc[3]), (pl_[0], pl_[1], pl_[3]), a_w, a_b)
    filt, norm = _hyena_filter(xl.shape[1], *filt_params)
    h_l = _hyena_lat(pl_[4], pl_[5], pl_[6], filt, norm, bias)
    nw = [gla_norm.reshape(1, GLA_DV)]
    specs = [lambda tm: _chan_spec(HY_CH, tm)] + [lambda tm: _seq_spec(tm, HALF)] * 3
    xl = _outproj(_outproj_even_kernel, [h_l, g_l[0], g_l[1], pl_[2]], specs, xl, mod_l, w_out,
                  ln_g, ln_b, nw, "row", 1024)
    if need_ctx:
        filt_c, norm_c = _hyena_filter(xc.shape[1], *filt_params)
        h_c = _hyena_ctx(pc[4], pc[5], pc[6], filt_c, norm_c, bias)
        xc = _outproj(_outproj_even_kernel, [h_c, g_c[0], g_c[1], pc[2]], specs, xc, mod_c, w_out,
                      ln_g, ln_b, nw, "row", 1024)
    return xc, xl


def _odd_layer_mix(xc, xl, mod_c, mod_l, w_in, w_out, ml_p, ssd_p, ln_g, ln_b, need_ctx):
    gate_b, ml_norm = ml_p
    conv_w, conv_b, dt_bias, a_log, d_skip, ssd_norm = ssd_p
    ml_cols = 2 * ML_QK + 2 * HALF + 4 * ML_HEADS
    o0 = 2 * ML_QK + 2 * HALF
    s0 = ml_cols
    wp = jnp.concatenate([w_in[:, :o0], w_in[:, s0:s0 + HALF], w_in[:, o0:ml_cols],
                          w_in[:, s0 + HALF + SSD_CONV_DIM:]], axis=1).astype(BF16)
    wc = w_in[:, s0 + HALF:s0 + HALF + SSD_CONV_DIM].astype(BF16)
    cb = conv_b.reshape(1, SSD_CONV_DIM)
    psplit = (2 * ML_QK, HALF, HALF, HALF, SMALL)
    csplit = (HALF, SSD_GN, SSD_GN)
    pc = _inproj(xc, mod_c, wp, wc, conv_w, cb, psplit, csplit, True, "row", False, 512)
    pl_ = _inproj(xl, mod_l, wp, wc, conv_w, cb, psplit, csplit, True, "col", False, 512)
    m_c, m_l = _mlstm_mixer((pc[0], pc[1], pc[4]), (pl_[0], pl_[1], pl_[4]), gate_b)
    s_c, s_l = _ssd_mixer((pc[5], pc[6], pc[7], pc[4]), (pl_[5], pl_[6], pl_[7], pl_[4]), dt_bias, a_log)
    consts = [ml_norm.reshape(1, ML_DV), jnp.repeat(d_skip, SSD_HEADDIM).reshape(1, HALF),
              ssd_norm.reshape(1, HALF)]
    specs = [lambda tm: _seq_spec(tm, HALF)] * 7
    xl = _outproj(_outproj_odd_kernel, [m_l[0], m_l[1], pl_[2], s_l[0], s_l[1], pl_[5], pl_[3]], specs,
                  xl, mod_l, w_out, ln_g, ln_b, consts, "col", 512)
    if need_ctx:
        xc = _outproj(_outproj_odd_kernel, [m_c[0], m_c[1], pc[2], s_c[0], s_c[1], pc[5], pc[3]], specs,
                      xc, mod_c, w_out, ln_g, ln_b, consts, "row", 512)
    return xc, xl


def kernel(x, c, ctx, c_ctx, w_ada, b_ada, ln_g, ln_b, w_ffn_in, w_ffn_out, w_in_even, w_out_even, hy_conv_w, hy_conv_b, hy_w1, hy_b1, hy_w2, hy_b2, hy_w3, hy_freq, hy_bias, gla_a_w, gla_a_b, gla_norm_w, w_in_odd, w_out_odd, ml_gate_b, ml_norm_w, ssd_conv_w, ssd_conv_b, ssd_dt_bias, ssd_a_log, ssd_d, ssd_norm_w):
    bsz, n, d = x.shape
    depth = w_ada.shape[0]
    assert bsz == 2 and d == D_MODEL
    cond8 = jnp.zeros((8, d), F32).at[:bsz].set(c).at[bsz].set(c_ctx)
    mod_all = _ada_all(cond8, w_ada, b_ada)
    xl, xc = x, ctx
    for l in range(depth):
        last = l == depth - 1
        mod_l = mod_all[l, :bsz].reshape(bsz, N_MOD, d)
        mod_c = jnp.broadcast_to(mod_all[l, bsz].reshape(1, N_MOD, d), (bsz, N_MOD, d))
        w1a, w2a = w_ffn_in[l, 0].astype(BF16), w_ffn_out[l, 0].astype(BF16)
        w1b, w2b = w_ffn_in[l, 1].astype(BF16), w_ffn_out[l, 1].astype(BF16)
        odd = l % 2 == 1
        xl = _ffn_step(xl, mod_l, 0, w1a, w2a, ln_g[l, 0], ln_b[l, 0], out_cv=odd)
        xc = _ffn_step(xc, mod_c, 0, w1a, w2a, ln_g[l, 0], ln_b[l, 0])
        i = l // 2
        if not odd:
            hy_p = (hy_conv_w[i], hy_conv_b[i],
                    (hy_w1[i], hy_b1[i], hy_w2[i], hy_b2[i], hy_w3[i], hy_freq[i]), hy_bias[i])
            gla_p = (gla_a_w[i], gla_a_b[i], gla_norm_w[i])
            xc, xl = _even_layer_mix(xc, xl, mod_c, mod_l, w_in_even[i], w_out_even[i].astype(BF16),
                                     hy_p, gla_p, ln_g[l, 1], ln_b[l, 1], not last)
        else:
            ml_p = (ml_gate_b[i], ml_norm_w[i])
            ssd_p = (ssd_conv_w[i], ssd_conv_b[i], ssd_dt_bias[i], ssd_a_log[i], ssd_d[i], ssd_norm_w[i])
            xc, xl = _odd_layer_mix(xc, xl, mod_c, mod_l, w_in_odd[i], w_out_odd[i].astype(BF16),
                                    ml_p, ssd_p, ln_g[l, 1], ln_b[l, 1], not last)
        xl = _ffn_step(xl, mod_l, 6, w1b, w2b, ln_g[l, 2], ln_b[l, 2], in_cv=odd)
        if not last:
            xc = _ffn_step(xc, mod_c, 6, w1b, w2b, ln_g[l, 2], ln_b[l, 2])
    return xl
```

```python
import functools
import math

import numpy as np
import jax
import jax.numpy as jnp
from jax import lax
from jax.experimental import pallas as pl
from jax.experimental.pallas import tpu as pltpu

F32 = jnp.float32
BF16 = jnp.bfloat16

D_MODEL = 1024
DEPTH = 4
GRID_W = 64
D_FF = 2816
N_MOD = 9
HALF = D_MODEL // 2
CHUNK = 64
LN_EPS = 1e-5
NORM_EPS = 1e-6

HY_CH = HALF
HY_ORDER = 2
HY_EMB = 33
HY_BANDS = (HY_EMB - 1) // 2
HY_FAST_DECAY = 0.3
HY_SLOW_DECAY = 1.5
HY_TARGET = 1e-2
HY_COLS = 3 * HY_CH
HY_ROWS = HY_ORDER * HY_CH

GLA_HEADS = 4
GLA_DK = HALF // 2 // GLA_HEADS
GLA_DV = HALF // GLA_HEADS
GLA_RANK = 16
GLA_TAU = 16.0
GLA_QK = GLA_HEADS * GLA_DK

ML_HEADS = 4
ML_DQK = HALF // 2 // ML_HEADS
ML_DV = HALF // ML_HEADS
ML_QK = ML_HEADS * ML_DQK

SSD_HEADDIM = 64
SSD_HEADS = HALF // SSD_HEADDIM
SSD_GROUPS = 2
SSD_HPG = SSD_HEADS // SSD_GROUPS
SSD_STATE = 128
SSD_GN = SSD_GROUPS * SSD_STATE
SSD_CONV_DIM = HALF + 2 * SSD_GN

DN_ALPHA = (2.0 * DEPTH) ** 0.25

SMALL = 32
LANES = 128
VMEM_LIMIT = 56 * 1024 * 1024


def _cparams(*sem):
    return pltpu.CompilerParams(dimension_semantics=sem, vmem_limit_bytes=VMEM_LIMIT)


def _mm(a, b):
    return jnp.dot(a.astype(BF16), b.astype(BF16), preferred_element_type=F32)


def _mm_nt(a, b):
    return lax.dot_general(a.astype(BF16), b.astype(BF16), (((1,), (1,)), ((), ())),
                           preferred_element_type=F32)


def _mm_tn(a, b):
    return lax.dot_general(a.astype(BF16), b.astype(BF16), (((0,), (0,)), ((), ())),
                           preferred_element_type=F32)


def _mm_hi(a, b):
    return jnp.dot(a, b, precision=lax.Precision.HIGHEST, preferred_element_type=F32)


def _mm_nt_hi(a, b):
    return lax.dot_general(a, b, (((1,), (1,)), ((), ())), precision=lax.Precision.HIGHEST,
                           preferred_element_type=F32)


def _split3(x):
    x1 = x.astype(BF16)
    r1 = x - x1.astype(F32)
    x2 = r1.astype(BF16)
    return x1, x2, (r1 - x2.astype(F32)).astype(BF16)


def _sel_rows(sel3, x):
    return jnp.dot(sel3.astype(BF16), jnp.concatenate(_split3(x), axis=0), preferred_element_type=F32)


def _sel_cols(x, sel):
    sel = sel.astype(BF16)
    return sum(jnp.dot(p, sel, preferred_element_type=F32) for p in _split3(x))


def _sel_nt(a, b, split_a):
    dn = (((1,), (1,)), ((), ()))
    if split_a:
        return sum(lax.dot_general(p, b.astype(BF16), dn, preferred_element_type=F32) for p in _split3(a))
    return sum(lax.dot_general(a.astype(BF16), p, dn, preferred_element_type=F32) for p in _split3(b))


def _tri3(reverse):
    row = lax.broadcasted_iota(jnp.int32, (CHUNK, 3 * CHUNK), 0)
    col = lax.broadcasted_iota(jnp.int32, (CHUNK, 3 * CHUNK), 1) % CHUNK
    return ((row <= col) if reverse else (row >= col)).astype(F32)


def _silu(x):
    return x * jax.nn.sigmoid(x)


def _log_sigmoid(x):
    return jnp.minimum(x, 0.0) - jnp.log1p(jnp.exp(-jnp.abs(x)))


def _softplus(x):
    return jnp.maximum(x, 0.0) + jnp.log1p(jnp.exp(-jnp.abs(x)))


def _layer_norm(y, g, b):
    yc = y - jnp.mean(y, axis=-1, keepdims=True)
    return yc * lax.rsqrt(jnp.mean(yc * yc, axis=-1, keepdims=True) + LN_EPS) * g + b


def _rms(y):
    return y * lax.rsqrt(jnp.mean(y * y, axis=-1, keepdims=True) + NORM_EPS)


def _tri(reverse):
    row = lax.broadcasted_iota(jnp.int32, (CHUNK, CHUNK), 0)
    col = lax.broadcasted_iota(jnp.int32, (CHUNK, CHUNK), 1)
    return (row <= col) if reverse else (row >= col)


def _eye(n):
    return (lax.broadcasted_iota(jnp.int32, (n, n), 0) == lax.broadcasted_iota(jnp.int32, (n, n), 1)).astype(F32)


def _resident(shape):
    nd = len(shape)
    return pl.BlockSpec(shape, lambda *_: (0,) * nd, pipeline_mode=pl.Buffered(1))


def _cplx_mm(ar, ai, w):
    u, v = _mm(ar, w), _mm(ai, w)
    h = w.shape[1] // 2
    return u[:, :h] - v[:, h:], u[:, h:] + v[:, :h]


def _cplx_mm_left(fr, fi, xr, xi):
    rhs = jnp.concatenate([xr, xi], axis=1)
    p, q = _mm(fr, rhs), _mm(fi, rhs)
    h = xr.shape[1]
    return p[:, :h] - q[:, h:], p[:, h:] + q[:, :h]


def _ada_kernel(c_ref, w_ref, b_ref, o_ref):
    o_ref[0] = _mm(_silu(c_ref[...]), w_ref[0]) + b_ref[0]


def _ada_all(cond8, w_ada, b_ada):
    depth, d, nm = w_ada.shape
    tn = nm // 8
    return pl.pallas_call(
        _ada_kernel,
        grid=(depth, nm // tn),
        in_specs=[pl.BlockSpec((8, d), lambda l, j: (0, 0)),
                  pl.BlockSpec((1, d, tn), lambda l, j: (l, 0, j)),
                  pl.BlockSpec((1, 1, tn), lambda l, j: (l, 0, j))],
        out_specs=pl.BlockSpec((1, 8, tn), lambda l, j: (l, 0, j)),
        out_shape=jax.ShapeDtypeStruct((depth, 8, nm), F32),
        name="ada",
        compiler_params=_cparams("arbitrary", "arbitrary"),
    )(cond8, w_ada, b_ada.reshape(depth, 1, nm))


def _stream_dims(x, layout):
    if layout == "col":
        return x.shape[0], x.shape[1] * GRID_W, x.shape[2] // GRID_W
    return x.shape


COL_TILE = 4


def _tile_rows(n, layout, tm):
    return COL_TILE * (n // GRID_W) if layout == "col" else min(tm, n)


def _main_spec(n, d, layout, tm):
    if layout == "col":
        return pl.BlockSpec((None, n // GRID_W, COL_TILE * d), lambda b, i: (b, 0, i))
    return pl.BlockSpec((None, tm, d), lambda b, i: (b, i, 0))


def _load_tile(x_ref, d):
    cols = x_ref.shape[1] // d
    if cols == 1:
        return x_ref[...]
    return jnp.concatenate([x_ref[:, c * d:(c + 1) * d] for c in range(cols)], axis=0)


def _store_tile(o_ref, y, d):
    cols = o_ref.shape[1] // d
    rows = o_ref.shape[0]
    if cols == 1:
        o_ref[...] = y
    else:
        for c in range(cols):
            o_ref[:, c * d:(c + 1) * d] = y[c * rows:(c + 1) * rows]


def _halo_specs(n, d, layout, tm):
    if layout == "col":
        r8 = n // GRID_W // 8
        prev = pl.BlockSpec((None, 8, d), lambda b, i: (b, r8 - 1, jnp.maximum(COL_TILE * i - 1, 0)))
        nxt = pl.BlockSpec((None, 8, d), lambda b, i: (b, 0, jnp.minimum(COL_TILE * (i + 1), GRID_W - 1)))
    else:
        t8, n8 = tm // 8, n // 8
        prev = pl.BlockSpec((None, 8, d), lambda b, i: (b, jnp.maximum(i * t8 - 1, 0), 0))
        nxt = pl.BlockSpec((None, 8, d), lambda b, i: (b, jnp.minimum((i + 1) * t8, n8 - 1), 0))
    return prev, nxt


def _seq_spec(tm, cols):
    return pl.BlockSpec((None, tm, cols), lambda b, i: (b, i, 0))


def _chan_spec(rows, tm):
    return pl.BlockSpec((None, tm // LANES, rows, LANES), lambda b, i: (b, i, 0, 0))


def _lanes_of(ref, lead, pieces):
    return jnp.concatenate([ref[lead + (q,)] for q in range(pieces)], axis=1)


def _ffn_kernel(x_ref, mod_ref, w1_ref, w2_ref, g_ref, b_ref, o_ref, *, mo, nk, in_cv, out_cv):
    d = w2_ref.shape[1]
    pieces = in_cv or out_cv
    if in_cv:
        x = jnp.concatenate([x_ref[:, c * d:(c + 1) * d] for c in range(GRID_W)], axis=0)
    elif pieces:
        x = jnp.concatenate([x_ref[:, c, :] for c in range(GRID_W)], axis=0)
    else:
        x = x_ref[...]
    shift, scale, gate = mod_ref[mo:mo + 1, :], mod_ref[mo + 1:mo + 2, :], mod_ref[mo + 2:mo + 3, :]
    h = (x * (1.0 + scale) + shift).astype(BF16)
    tf = D_FF // nk
    acc = jnp.zeros(x.shape, F32)
    for k in range(nk):
        gt = jnp.dot(h, w1_ref[:, k * tf:(k + 1) * tf], preferred_element_type=F32)
        up = jnp.dot(h, w1_ref[:, D_FF + k * tf:D_FF + (k + 1) * tf], preferred_element_type=F32)
        acc = acc + _mm(_silu(gt) * up, w2_ref[k * tf:(k + 1) * tf, :])
    y = _layer_norm(DN_ALPHA * x + (0.5 * gate) * acc, g_ref[...], b_ref[...])
    if out_cv:
        for c in range(GRID_W):
            o_ref[:, c * d:(c + 1) * d] = y[8 * c:8 * c + 8]
    elif pieces:
        for c in range(GRID_W):
            o_ref[:, c, :] = y[8 * c:8 * c + 8]
    else:
        o_ref[...] = y


def _ffn_step(x, mod, mo, w1, w2, g, b, in_cv=False, out_cv=False, tm=512):
    bsz, n, d = _stream_dims(x, "col" if in_cv else "row")
    rows = n // GRID_W
    if in_cv or out_cv:
        tm = 8 * GRID_W
        cv_spec = pl.BlockSpec((None, 8, GRID_W * d), lambda bb, i: (bb, i, 0))
        tok_spec = pl.BlockSpec((None, 8, GRID_W, d), lambda bb, i: (bb, i, 0, 0))
        x = x if in_cv else x.reshape(bsz, rows, GRID_W, d)
        in_spec, out_spec = (cv_spec if in_cv else tok_spec), (cv_spec if out_cv else tok_spec)
        out_dims = (bsz, rows, GRID_W * d) if out_cv else (bsz, rows, GRID_W, d)
    else:
        tm = min(tm, n)
        in_spec = out_spec = _seq_spec(tm, d)
        out_dims = (bsz, n, d)
    out = pl.pallas_call(
        functools.partial(_ffn_kernel, mo=mo, nk=1, in_cv=in_cv, out_cv=out_cv),
        grid=(bsz, n // tm),
        in_specs=[in_spec,
                  pl.BlockSpec((None, N_MOD, d), lambda bb, i: (bb, 0, 0)),
                  _resident(w1.shape), _resident(w2.shape),
                  _resident((1, d)), _resident((1, d))],
        out_specs=out_spec,
        out_shape=jax.ShapeDtypeStruct(out_dims, F32),
        name="ffn",
        compiler_params=_cparams("arbitrary", "arbitrary"),
    )(x, mod, w1, w2, g.reshape(1, d), b.reshape(1, d))
    return out if out_cv else out.reshape(bsz, n, d)


def _inproj_kernel(x_ref, xp_ref, xn_ref, mod_ref, wp_ref, wc_ref, cw_ref, cb_ref, *out_refs,
                   psplit, csplit, act, chan_major):
    i, last = pl.program_id(1), pl.num_programs(1) - 1
    shift, scale = mod_ref[3:4, :], mod_ref[4:5, :]
    x = _load_tile(x_ref, mod_ref.shape[1])
    tm = x.shape[0]
    h = x * (1.0 + scale) + shift
    hb = h.astype(BF16)
    o = 0
    for k, w in enumerate(psplit):
        out_refs[k][...] = jnp.dot(hb, wp_ref[:, o:o + w], preferred_element_type=F32).astype(out_refs[k].dtype)
        o += w
    hp = (xp_ref[...] * (1.0 + scale) + shift) * jnp.where(i > 0, 1.0, 0.0)
    hn = (xn_ref[...] * (1.0 + scale) + shift) * jnp.where(i < last, 1.0, 0.0)
    couts = out_refs[len(psplit):]
    if chan_major:
        lane = lax.broadcasted_iota(jnp.int32, (1, tm), 1)
        halo = jnp.concatenate([hp, hn, jnp.zeros((LANES - 16, hp.shape[1]), F32)], axis=0).astype(BF16)
        o = 0
        for k, w in enumerate(csplit):
            wt = wc_ref[o:o + w, :]
            p = _mm_nt(wt, hb)
            p_halo = _mm_nt(wt, halo)
            up = jnp.where(lane == 0, p_halo[:, 7:8], pltpu.roll(p, 1, 1))
            dn = jnp.where(lane == tm - 1, p_halo[:, 8:9], pltpu.roll(p, tm - 1, 1))
            cw = cw_ref[o:o + w, :]
            y = cw[:, 0:1] * up + cw[:, 1:2] * p + cw[:, 2:3] * dn + cb_ref[o:o + w, :]
            for q in range(tm // LANES):
                couts[k][q] = y[:, q * LANES:(q + 1) * LANES]
            o += w
    else:
        pc = _mm(jnp.concatenate([hp, h, hn], axis=0), wc_ref[...])
        up = pltpu.roll(pc, 1, 0)
        dn = pltpu.roll(pc, tm + 15, 0)
        y = (cw_ref[0:1, :] * up + cw_ref[1:2, :] * pc + cw_ref[2:3, :] * dn + cb_ref[...])[8:8 + tm]
        if act:
            y = _silu(y)
        o = 0
        for k, w in enumerate(csplit):
            couts[k][...] = y[:, o:o + w].astype(couts[k].dtype)
            o += w


def _inproj(xv, mod, wp, wc, cw, cb, psplit, csplit, act, layout, chan_major, tm):
    bsz, n, d = _stream_dims(xv, layout)
    tm = _tile_rows(n, layout, tm)
    prev, nxt = _halo_specs(n, d, layout, tm)
    pdt = [BF16 if w >= LANES else F32 for w in psplit]
    out_specs = [_seq_spec(tm, w) for w in psplit]
    out_shape = [jax.ShapeDtypeStruct((bsz, n, w), dt) for w, dt in zip(psplit, pdt)]
    if chan_major:
        out_specs += [_chan_spec(w, tm) for w in csplit]
        out_shape += [jax.ShapeDtypeStruct((bsz, n // LANES, w, LANES), F32) for w in csplit]
    else:
        out_specs += [_seq_spec(tm, w) for w in csplit]
        out_shape += [jax.ShapeDtypeStruct((bsz, n, w), BF16) for w in csplit]
    return pl.pallas_call(
        functools.partial(_inproj_kernel, psplit=tuple(psplit), csplit=tuple(csplit), act=act,
                          chan_major=chan_major),
        grid=(bsz, n // tm),
        in_specs=[_main_spec(n, d, layout, tm), prev, nxt,
                  pl.BlockSpec((None, N_MOD, d), lambda bb, i: (bb, 0, 0)),
                  _resident(wp.shape), _resident(wc.shape), _resident(cw.shape), _resident(cb.shape)],
        out_specs=out_specs,
        out_shape=out_shape,
        name="inproj_chan" if chan_major else "inproj",
        compiler_params=_cparams("arbitrary", "arbitrary"),
    )(xv, xv, xv, mod, wp, wc, cw, cb)


def _outproj_even_kernel(hy_ref, of_ref, ob_ref, g_ref, x_ref, mod_ref, w_ref, lg_ref, lb_ref, nw_ref, o_ref):
    tm = x_ref.shape[0]
    y = _mm_tn(_lanes_of(hy_ref, (), tm // LANES), w_ref[0:HALF, :])
    o = of_ref[...].astype(F32) + ob_ref[...].astype(F32)
    g = g_ref[...].astype(F32)
    heads = [_rms(o[:, h * GLA_DV:(h + 1) * GLA_DV]) * nw_ref[...] for h in range(GLA_HEADS)]
    y = y + _mm(_silu(g) * jnp.concatenate(heads, axis=1), w_ref[HALF:2 * HALF, :])
    o_ref[...] = _layer_norm(DN_ALPHA * x_ref[...] + mod_ref[5:6, :] * y, lg_ref[...], lb_ref[...])


def _outproj_odd_kernel(hf_ref, hb_ref, og_ref, yf_ref, yb_ref, xs_ref, z_ref, x_ref, mod_ref, w_ref,
                        lg_ref, lb_ref, mnw_ref, dsk_ref, snw_ref, o_ref):
    h = hf_ref[...].astype(F32) + hb_ref[...].astype(F32)
    heads = []
    for k in range(ML_HEADS):
        hk = h[:, k * ML_DV:(k + 1) * ML_DV]
        hc = hk - jnp.mean(hk, axis=-1, keepdims=True)
        heads.append(hc * lax.rsqrt(jnp.mean(hc * hc, axis=-1, keepdims=True) + NORM_EPS) * mnw_ref[...])
    m = jax.nn.sigmoid(og_ref[...].astype(F32)) * jnp.concatenate(heads, axis=1)
    s = yf_ref[...].astype(F32) + yb_ref[...].astype(F32) + dsk_ref[...] * xs_ref[...].astype(F32)
    s = _rms(s * _silu(z_ref[...].astype(F32))) * snw_ref[...]
    y = _mm(m, w_ref[0:HALF, :]) + _mm(s, w_ref[HALF:2 * HALF, :])
    d = mod_ref.shape[1]
    x = _load_tile(x_ref, d)
    _store_tile(o_ref, _layer_norm(DN_ALPHA * x + mod_ref[5:6, :] * y, lg_ref[...], lb_ref[...]), d)


def _outproj(kern, parts, part_specs, xv, mod, w, g, b, consts, layout, tm):
    bsz, n, d = _stream_dims(xv, layout)
    tm = _tile_rows(n, layout, tm)
    consts = [g.reshape(1, d), b.reshape(1, d)] + consts
    return pl.pallas_call(
        kern,
        grid=(bsz, n // tm),
        in_specs=[s(tm) for s in part_specs] + [_main_spec(n, d, layout, tm),
                  pl.BlockSpec((None, N_MOD, d), lambda bb, i: (bb, 0, 0)), _resident(w.shape)]
        + [_resident(c.shape) for c in consts],
        out_specs=_main_spec(n, d, layout, tm),
        out_shape=jax.ShapeDtypeStruct(xv.shape, F32),
        name=kern.__name__.strip("_"),
        compiler_params=_cparams("arbitrary", "arbitrary"),
    )(*parts, xv, mod, w, *consts)


def _scan_call(kernel, seqs, consts, states, out_cols, tb=512):
    bsz, n, _ = seqs[0].shape
    tb = min(tb, n)
    nblk = n // tb

    def fmap(i):
        return (0, i, 0)

    def bmap(i):
        return (0, nblk - 1 - i, 0)

    in_specs = [pl.BlockSpec((bsz, tb, s.shape[-1]), fmap) for s in seqs]
    in_specs += [pl.BlockSpec((bsz, tb, s.shape[-1]), bmap) for s in seqs]
    in_specs += [_resident(c.shape) for c in consts]
    st_specs = [pl.BlockSpec(s.shape, lambda i, nd=s.ndim: (0,) * nd) for s in states]
    outs = pl.pallas_call(
        functools.partial(kernel, nchunk=tb // CHUNK),
        grid=(nblk,),
        in_specs=in_specs + st_specs,
        out_specs=[pl.BlockSpec((bsz, tb, out_cols), fmap), pl.BlockSpec((bsz, tb, out_cols), bmap)] + st_specs,
        out_shape=[jax.ShapeDtypeStruct((bsz, n, out_cols), BF16)] * 2
        + [jax.ShapeDtypeStruct(s.shape, F32) for s in states],
        scratch_shapes=[pltpu.VMEM(s.shape, F32) for s in states],
        name=kernel.__name__.strip("_"),
        compiler_params=_cparams("arbitrary"),
    )(*seqs, *seqs, *consts, *states)
    return outs[0], outs[1], tuple(outs[2:])


def _chunk_rows(ci, nchunk, reverse):
    c = (nchunk - 1 - ci) if reverse else ci
    return pl.ds(pl.multiple_of(c * CHUNK, CHUNK), CHUNK)


def _chains(ci, nchunk, bsz, unroll):
    return [(d, b, _chunk_rows(ci * unroll + u, nchunk, d == 1))
            for u in range(unroll) for d in range(2) for b in range(bsz)]


def _interleave(gens):
    results = [None] * len(gens)
    live = list(enumerate(gens))
    while live:
        still = []
        for idx, g in live:
            try:
                next(g)
                still.append((idx, g))
            except StopIteration as stop:
                results[idx] = stop.value
        live = still
    return results


def _scan_states(i, s0_refs, sout_refs, scrs, body, trips):
    @pl.when(i == 0)
    def _():
        for s0, scr in zip(s0_refs, scrs):
            scr[...] = s0[...]

    lax.fori_loop(0, trips, body, 0)

    @pl.when(i == pl.num_programs(0) - 1)
    def _():
        for so, scr in zip(sout_refs, scrs):
            so[...] = scr[...]


def _gla_chunk(qk, v, a, aw, ab, state, tri, head):
    heads = range(GLA_HEADS)
    q = qk[:, 0:GLA_QK].astype(F32) * (GLA_DK ** -0.5)
    k = qk[:, GLA_QK:2 * GLA_QK].astype(F32)
    tri, tri3 = tri
    lr = _mm(a, aw)
    yield
    log_a = _log_sigmoid(lr + ab) * (1.0 / GLA_TAU)
    b = _sel_rows(tri3, log_a)
    yield
    b_last = jnp.sum(log_a, axis=0, keepdims=True)
    q_dec = q * jnp.exp(b)
    k_inv = k * jnp.exp(-b)
    k_dec = k * jnp.exp(b_last - b)
    vs = [v[:, h * GLA_DV:(h + 1) * GLA_DV] for h in heads]
    q4 = jnp.concatenate([jnp.where(head == h, q_dec, 0.0) for h in heads], axis=0).astype(BF16)
    att4 = _mm_nt(q4, k_inv)
    kv = _mm_tn(jnp.concatenate(vs, axis=0),
                jnp.concatenate([jnp.where(head == h, k_dec, 0.0) for h in heads], axis=0))
    yield
    s_old = state[0]
    inter4 = _mm_nt(q4, s_old)
    state[0] = s_old * jnp.exp(b_last) + kv
    outs = [_mm(jnp.where(tri, att4[h * CHUNK:(h + 1) * CHUNK], 0.0), vs[h]) for h in heads]
    yield
    return [outs[h] + inter4[h * CHUNK:(h + 1) * CHUNK] for h in heads]


def _gla_kernel(qkf_ref, vf_ref, af_ref, qkb_ref, vb_ref, ab_ref, aw_ref, abias_ref, s0_ref,
                of_ref, ob_ref, sout_ref, s_scr, *, nchunk):
    head = lax.broadcasted_iota(jnp.int32, (1, GLA_QK), 1) // GLA_DK
    tris = tuple((_tri(rev), _tri3(rev)) for rev in (False, True))
    dirs = ((qkf_ref, vf_ref, af_ref, of_ref), (qkb_ref, vb_ref, ab_ref, ob_ref))
    bsz, unroll = qkf_ref.shape[0], 4

    def body(ci, carry):
        chains = _chains(ci, nchunk, bsz, unroll)
        states = {(d, b): [s_scr[d, b]] for d in range(2) for b in range(bsz)}
        done = _interleave([_gla_chunk(dirs[d][0][b, rows, :], dirs[d][1][b, rows, :], dirs[d][2][b, rows, :],
                                       aw_ref[d], abias_ref[d], states[d, b], tris[d], head)
                            for d, b, rows in chains])
        for (d, b, rows), outs in zip(chains, done):
            for h, oh in enumerate(outs):
                dirs[d][3][b, rows, h * GLA_DV:(h + 1) * GLA_DV] = oh.astype(BF16)
        for (d, b), st in states.items():
            s_scr[d, b] = st[0]
        return carry

    _scan_states(pl.program_id(0), [s0_ref], [sout_ref], [s_scr], body, nchunk // unroll)


def _gla_mixer(ctx_p, lat_p, a_w, a_b):
    bsz = ctx_p[0].shape[0]
    aw = jnp.zeros((2, SMALL, GLA_QK), F32)
    for d in range(2):
        aw = aw.at[d, d * GLA_RANK:(d + 1) * GLA_RANK].set(a_w[d])
    ab = a_b.reshape(2, 1, GLA_QK)
    st = (jnp.zeros((2, bsz, GLA_DV, GLA_QK), F32),)
    cf, cb, st = _scan_call(_gla_kernel, list(ctx_p), [aw, ab], st, HALF)
    lf, lb, _ = _scan_call(_gla_kernel, list(lat_p), [aw, ab], st, HALF)
    return (cf, cb), (lf, lb)


def _mlstm_chunk(qk, v, gates, state, tri, head, eye, shift, spread, d):
    q = qk[:, 0:ML_QK].astype(F32)
    k = qk[:, ML_QK:2 * ML_QK].astype(F32) * (ML_DQK ** -0.5)
    tri, tri3 = tri
    heads = range(ML_HEADS)
    trif = tri.astype(F32)
    gates_t = _mm_nt_hi(eye, gates)
    lf = _log_sigmoid(gates)
    b_col = _mm_hi(trif, lf)
    yield
    b_row = _mm_nt_hi(_log_sigmoid(gates_t), trif)
    b_tot = jnp.sum(lf, axis=0, keepdims=True)
    vs = [v[:, h * ML_DV:(h + 1) * ML_DV] for h in heads]
    qms = [jnp.where(head == h, q, 0.0) for h in heads]
    s_raw = [_mm_nt(qm, k) for qm in qms]
    c_old, n_all, m_all = state
    n_old = n_all[0:1, :]
    inter = [_mm_nt(qm, c_old) for qm in qms]
    m_old = m_all[0:1, :]
    a = b_tot - b_col + _sel_cols(gates, shift)
    a_max = jnp.max(a, axis=0, keepdims=True)
    m_new = jnp.maximum(b_tot + m_old, a_max)
    w_old, w_new = jnp.exp(b_tot + m_old - m_new), jnp.exp(a_max - m_new)
    kw_all = k * _mm(jnp.exp(a - a_max) * w_new, spread)
    keep = _sel_cols(jnp.broadcast_to(w_old, (8, SMALL)), spread)[0:1, :]
    n_add = jnp.sum(kw_all, axis=0, keepdims=True)
    c_add = _mm_tn(jnp.concatenate(vs, axis=0),
                   jnp.concatenate([jnp.where(head == h, kw_all, 0.0) for h in heads], axis=0))
    state[:] = [c_old * keep + c_add, jnp.broadcast_to(n_old * keep + n_add, n_all.shape),
                jnp.broadcast_to(m_new, m_all.shape)]
    yield
    ms, es, ss = [], [], []
    for h in heads:
        ic, fc = d * 8 + h, d * 8 + 4 + h
        bc = b_col[:, fc:fc + 1]
        r_vis = jnp.where(tri, gates_t[ic:ic + 1, :] - b_row[fc:fc + 1, :], -jnp.inf)
        g = bc + m_all[0:1, fc:fc + 1]
        m = jnp.maximum(g, bc + jnp.max(r_vis, axis=1, keepdims=True))
        ms.append(m)
        es.append(jnp.exp(g - m))
        ss.append(s_raw[h] * jnp.exp(r_vis + (bc - m)))
    nums = [_mm(ss[h], vs[h]) for h in heads]
    yield
    outs = []
    for h in heads:
        num = nums[h] + es[h] * inter[h]
        den = jnp.sum(ss[h], axis=1, keepdims=True) + es[h] * jnp.sum(qms[h] * n_old, axis=1, keepdims=True)
        outs.append(num / jnp.maximum(jnp.abs(den), jnp.exp(-ms[h])))
    return outs


def _mlstm_kernel(qkf_ref, vf_ref, gf_ref, qkb_ref, vb_ref, gb_ref, bias_ref, c0_ref, n0_ref, m0_ref,
                  hf_ref, hb_ref, cout_ref, nout_ref, mout_ref, c_scr, n_scr, m_scr, *, nchunk):
    head = lax.broadcasted_iota(jnp.int32, (1, ML_QK), 1) // ML_DQK
    col_r = lax.broadcasted_iota(jnp.int32, (SMALL, SMALL), 0)
    col_c = lax.broadcasted_iota(jnp.int32, (SMALL, SMALL), 1)
    shift = ((col_c == col_r + ML_HEADS) & (col_r % 8 < ML_HEADS)).astype(F32)
    row = lax.broadcasted_iota(jnp.int32, (SMALL, ML_QK), 0)
    lane_head = lax.broadcasted_iota(jnp.int32, (SMALL, ML_QK), 1) // ML_DQK
    spreads = [(row == d * 8 + ML_HEADS + lane_head).astype(F32) for d in range(2)]
    eye = _eye(SMALL)
    tris = tuple((_tri(rev), _tri3(rev)) for rev in (False, True))
    dirs = ((qkf_ref, vf_ref, gf_ref, hf_ref), (qkb_ref, vb_ref, gb_ref, hb_ref))
    bsz, unroll = qkf_ref.shape[0], 2

    def body(ci, carry):
        chains = _chains(ci, nchunk, bsz, unroll)
        states = {(d, b): [c_scr[d, b], n_scr[d, b], m_scr[d, b]] for d in range(2) for b in range(bsz)}
        done = _interleave([_mlstm_chunk(
            dirs[d][0][b, rows, :], dirs[d][1][b, rows, :], dirs[d][2][b, rows, :] + bias_ref[...],
            states[d, b], tris[d], head, eye, shift, spreads[d], d) for d, b, rows in chains])
        for (d, b, rows), outs in zip(chains, done):
            for h, oh in enumerate(outs):
                dirs[d][3][b, rows, h * ML_DV:(h + 1) * ML_DV] = oh.astype(BF16)
        for (d, b), st in states.items():
            c_scr[d, b], n_scr[d, b], m_scr[d, b] = st
        return carry

    _scan_states(pl.program_id(0), [c0_ref, n0_ref, m0_ref], [cout_ref, nout_ref, mout_ref],
                 [c_scr, n_scr, m_scr], body, nchunk // unroll)


def _mlstm_mixer(ctx_p, lat_p, gate_b):
    bsz = ctx_p[0].shape[0]
    st = (jnp.zeros((2, bsz, ML_DV, ML_QK), F32), jnp.zeros((2, bsz, 8, ML_QK), F32),
          jnp.zeros((2, bsz, 8, SMALL), F32))
    gb = jnp.zeros((1, SMALL), F32).at[0, :16].set(gate_b.reshape(16))
    cf, cb, st = _scan_call(_mlstm_kernel, list(ctx_p), [gb], st, HALF)
    lf, lb, _ = _scan_call(_mlstm_kernel, list(lat_p), [gb], st, HALF)
    return (cf, cb), (lf, lb)


def _ssd_chunk(x, bm, cm, raw, alr, alc, expand, state, tri, sub, eye, d):
    x, bm, cm = x.astype(F32), bm.astype(F32), cm.astype(F32)
    gw = SSD_HPG * SSD_HEADDIM
    tri, tri3 = tri
    groups = range(SSD_GROUPS)
    lns = [slice(g * gw, (g + 1) * gw) for g in groups]
    cms = [cm[:, g * SSD_STATE:(g + 1) * SSD_STATE] for g in groups]
    bms = [bm[:, g * SSD_STATE:(g + 1) * SSD_STATE] for g in groups]
    raw_t = _sel_nt(eye, raw, False)
    dt = _softplus(raw)
    da = dt * (-jnp.exp(alr))
    cum = _sel_rows(tri3, da)
    cbs = [_mm_nt(cms[g], bms[g]) for g in groups]
    yield
    dt_t = _softplus(raw_t)
    cum_t = _sel_nt(dt_t * (-jnp.exp(alc)), tri, True)
    cl = jnp.sum(da, axis=0, keepdims=True)
    e_cum = _sel_cols(jnp.exp(cum), expand)
    wgt = _sel_cols(jnp.exp(cl - cum) * dt, expand)
    dec = _sel_cols(jnp.broadcast_to(jnp.exp(cl), (8, SMALL)), expand)[0:1, :]
    yield
    xw = x * wgt
    kvs = [_mm_tn(bms[g], xw[:, lns[g]]) for g in groups]
    diags = []
    for g in groups:
        xg = x[:, lns[g]]
        mixes = []
        for j in range(SSD_HPG):
            col = 16 + d * 8 + g * SSD_HPG + j
            seg = jnp.exp(jnp.where(tri, cum[:, col:col + 1] - cum_t[col:col + 1, :], -jnp.inf))
            mixes.append(cbs[g] * seg * dt_t[col:col + 1, :])
        diags.append(_mm(jnp.concatenate(mixes, axis=1),
                         jnp.concatenate([jnp.where(sub == j, xg, 0.0) for j in range(SSD_HPG)], axis=0)))
    yield
    offs = [_mm(cms[g], state[g]) for g in groups]
    state[:] = [state[g] * dec[:, lns[g]] + kvs[g] for g in groups]
    yield
    return [offs[g] * e_cum[:, lns[g]] + diags[g] for g in groups]


def _ssd_kernel(xf_ref, bmf_ref, cmf_ref, rf_ref, xb_ref, bmb_ref, cmb_ref, rb_ref,
                dtb_ref, alr_ref, alc_ref, ex_ref, s0_ref, yf_ref, yb_ref, sout_ref, s_scr, *, nchunk):
    gw = SSD_HPG * SSD_HEADDIM
    sub = lax.broadcasted_iota(jnp.int32, (1, gw), 1) // SSD_HEADDIM
    eye = _eye(SMALL)
    tris = tuple((_tri(rev), _tri3(rev)) for rev in (False, True))
    dirs = ((xf_ref, bmf_ref, cmf_ref, rf_ref, yf_ref), (xb_ref, bmb_ref, cmb_ref, rb_ref, yb_ref))

    bsz, unroll = xf_ref.shape[0], 4

    def body(ci, carry):
        chains = _chains(ci, nchunk, bsz, unroll)
        states = {(d, b): [s_scr[d, b, :, g * gw:(g + 1) * gw] for g in range(SSD_GROUPS)]
                  for d in range(2) for b in range(bsz)}
        done = _interleave([_ssd_chunk(
            dirs[d][0][b, rows, :], dirs[d][1][b, rows, :], dirs[d][2][b, rows, :],
            dirs[d][3][b, rows, :] + dtb_ref[...], alr_ref[...], alc_ref[...],
            ex_ref[d], states[d, b], tris[d], sub, eye, d) for d, b, rows in chains])
        for (d, b, rows), ys in zip(chains, done):
            for g in range(SSD_GROUPS):
                dirs[d][4][b, rows, g * gw:(g + 1) * gw] = ys[g].astype(BF16)
        for (d, b), st in states.items():
            for g in range(SSD_GROUPS):
                s_scr[d, b, :, g * gw:(g + 1) * gw] = st[g]
        return carry

    _scan_states(pl.program_id(0), [s0_ref], [sout_ref], [s_scr], body, nchunk // unroll)


def _ssd_mixer(ctx_p, lat_p, dt_bias, a_log):
    bsz = ctx_p[0].shape[0]
    st = (jnp.zeros((2, bsz, SSD_STATE, HALF), F32),)
    dtb = jnp.zeros((1, SMALL), F32).at[0, 16:].set(dt_bias.reshape(16))
    al = jnp.zeros((1, SMALL), F32).at[0, 16:].set(a_log.reshape(16).astype(F32))
    lane_head = np.arange(HALF) // SSD_HEADDIM
    ex = jnp.asarray(np.stack([(np.arange(SMALL)[:, None] == (16 + d * 8 + lane_head)[None, :])
                               for d in range(2)]).astype(np.float32))
    consts = [dtb, al, al.reshape(SMALL, 1), ex]
    cf, cb, st = _scan_call(_ssd_kernel, list(ctx_p), consts, st, HALF)
    lf, lb, _ = _scan_call(_ssd_kernel, list(lat_p), consts, st, HALF)
    return (cf, cb), (lf, lb)


def _hy_filter_kernel(f_ref, w1t_ref, w1c_ref, w1s_ref, b1_ref, w2_ref, b2_ref, fq1_ref, fq2_ref, w3s_ref, w3b_ref,
                      dl_ref, o_ref, s_ref, *, n, npiece):
    side, jt = pl.program_id(0), pl.program_id(1)
    tnorm = 1.0 / max(n - 1, 1)
    first = (side == 0) & (jt == 0)
    width = npiece * LANES

    @pl.when(first)
    def _():
        s_ref[...] = jnp.zeros(s_ref.shape, F32)

    def mlp(pos):
        ang = pos * f_ref[...] * (2.0 * math.pi / n)
        pre = (pos * tnorm) * w1t_ref[...] + _mm_hi(w1c_ref[...], jnp.cos(ang)) \
            + _mm_hi(w1s_ref[...], -jnp.sin(ang)) + b1_ref[...]
        hid = jnp.sin(fq1_ref[...] * pre)
        return jnp.sin(fq2_ref[...] * (_mm_hi(w2_ref[...], hid) + b2_ref[...]))

    h00 = _mm(w3b_ref[...], mlp(jnp.zeros((1, LANES), F32)))[:, 0:1]
    tau = jt * width + lax.broadcasted_iota(jnp.int32, (1, width), 1)
    pos = (side * n + (1 - 2 * side) * tau).astype(F32)
    ht = _mm(w3s_ref[...], mlp(pos))
    ht = ht * jnp.exp(-(pos * tnorm) * dl_ref[...])
    ht = jnp.where((tau == 0) & (side == 1), 0.0, ht)
    s_ref[...] += jnp.sum(jnp.abs(ht), axis=1, keepdims=True) + jnp.where(first, jnp.abs(h00), 0.0)
    ht = ht + jnp.where((tau == 0) & (side == 0), h00, 0.0)
    for q in range(npiece):
        o_ref[q] = ht[:, q * LANES:(q + 1) * LANES]


def _hyena_filter(n, w1, b1, w2, b2, w3, freq):
    kk = n // LANES
    npiece = min(8, kk)
    ff = w1.shape[1]
    pad = LANES - ff
    bands = jnp.linspace(1e-4, HY_BANDS - 1, HY_BANDS, dtype=F32).reshape(HY_BANDS, 1)
    deltas = jnp.abs(jnp.linspace(math.log(HY_TARGET) / HY_SLOW_DECAY, math.log(HY_TARGET) / HY_FAST_DECAY,
                                  HY_CH, dtype=F32))
    dl = jnp.tile(deltas, HY_ORDER).reshape(HY_ROWS, 1)
    w3t = jnp.transpose(w3.reshape(ff, HY_ORDER, 2, HY_CH), (2, 1, 3, 0)).reshape(2, HY_ROWS, ff)
    w3t = jnp.pad(w3t, ((0, 0), (0, 0), (0, pad)))
    args = (bands, w1[0].reshape(ff, 1), jnp.transpose(w1[1:1 + HY_BANDS]), jnp.transpose(w1[1 + HY_BANDS:]),
            b1.reshape(ff, 1), jnp.pad(jnp.transpose(w2), ((0, pad), (0, 0))),
            jnp.pad(b2.reshape(ff, 1), ((0, pad), (0, 0))),
            freq[0].reshape(ff, 1), jnp.pad(freq[1].reshape(ff, 1), ((0, pad), (0, 0))))
    nt = kk // npiece
    return pl.pallas_call(
        functools.partial(_hy_filter_kernel, n=n, npiece=npiece),
        grid=(2, nt),
        in_specs=[_resident(a.shape) for a in args]
        + [pl.BlockSpec((None, HY_ROWS, LANES), lambda sd, j: (sd, 0, 0)),
           pl.BlockSpec((None, HY_ROWS, LANES), lambda sd, j: (1, 0, 0)), _resident(dl.shape)],
        out_specs=[pl.BlockSpec((npiece, HY_ROWS, LANES), lambda sd, j: (sd * nt + j, 0, 0)),
                   pl.BlockSpec((HY_ROWS, 1), lambda sd, j: (0, 0))],
        out_shape=[jax.ShapeDtypeStruct((2 * kk, HY_ROWS, LANES), F32),
                   jax.ShapeDtypeStruct((HY_ROWS, 1), F32)],
        name="hy_filter",
        compiler_params=_cparams("arbitrary", "arbitrary"),
    )(*args, w3t, w3t, dl)


def _dft_tables(n):
    big = 2 * n
    n1 = big // LANES
    k = np.arange(n1)

    def cis(num, den):
        ang = 2.0 * np.pi * (num % den) / den
        return np.stack([np.cos(ang), -np.sin(ang)]).astype(np.float32)

    f1 = cis(k[:, None] * k[None, :], n1)
    k2 = np.arange(LANES)
    f2 = cis(k2[:, None] * k2[None, :], LANES)
    tw = cis(k[:, None] * k2[None, :], big)
    w2 = np.concatenate([f2[0], f2[1]], axis=1)
    w2c = np.concatenate([f2[0], -f2[1]], axis=1)
    ginv = np.stack([f1[0, :n1 // 2, :], -f1[1, :n1 // 2, :]]) / big
    return tuple(jnp.asarray(a) for a in (f1, tw, w2, w2c, ginv))


HY_CB = 8
HY_GRP = 4


def _hy_spec_kernel(f_ref, s_ref, f1_ref, tw_ref, w2_ref, o_ref):
    f1r, f1i = f1_ref[0], f1_ref[1]
    twr, twi = tw_ref[0], tw_ref[1]
    n1 = f1r.shape[0]
    flat = f_ref[...].reshape(n1, HY_CB * LANES)
    for c0 in range(0, HY_CB, HY_GRP):
        trs, tis = [], []
        for ca in range(c0, c0 + HY_GRP, 2):
            rhs = flat[:, ca * LANES:(ca + 2) * LANES]
            pr, pi = _mm(f1r, rhs), _mm(f1i, rhs)
            for e in range(2):
                ar, ai = pr[:, e * LANES:(e + 1) * LANES], pi[:, e * LANES:(e + 1) * LANES]
                trs.append(ar * twr - ai * twi)
                tis.append(ar * twi + ai * twr)
        xr, xi = _cplx_mm(jnp.concatenate(trs, axis=0), jnp.concatenate(tis, axis=0), w2_ref[...])
        for e in range(HY_GRP):
            inv = 1.0 / s_ref[c0 + e]
            o_ref[0, c0 + e] = xr[e * n1:(e + 1) * n1] * inv
            o_ref[1, c0 + e] = xi[e * n1:(e + 1) * n1] * inv


def _hy_spectrum(filt, norm, f1, tw, w2):
    n1, rows, _ = filt.shape
    return pl.pallas_call(
        _hy_spec_kernel,
        grid=(rows // HY_CB,),
        in_specs=[pl.BlockSpec((n1, HY_CB, LANES), lambda j: (0, j, 0)),
                  pl.BlockSpec((HY_CB, 1, 1), lambda j: (j, 0, 0)),
                  _resident(f1.shape), _resident(tw.shape), _resident(w2.shape)],
        out_specs=pl.BlockSpec((2, HY_CB, n1, LANES), lambda j: (0, j, 0, 0)),
        out_shape=jax.ShapeDtypeStruct((2, rows, n1, LANES), F32),
        name="hy_spectrum",
        compiler_params=_cparams("arbitrary"),
    )(filt, norm.reshape(rows, 1, 1), f1, tw, w2)


def _hy_conv_kernel(v_ref, x_ref, hf_ref, bias_ref, f1_ref, g_ref, tw_ref, w2_ref, w2c_ref, o_ref):
    f1r, f1i = f1_ref[0], f1_ref[1]
    gr, gi = g_ref[0], g_ref[1]
    twr, twi = tw_ref[0], tw_ref[1]
    n1 = f1r.shape[0]
    k1 = v_ref.shape[1]
    vf = [v_ref[b].reshape(k1, HY_CB * LANES) for b in range(2)]
    xf = [x_ref[b].reshape(k1, HY_CB * LANES) for b in range(2)]
    outs = [[], []]
    for c0 in range(0, HY_CB, HY_GRP):
        vs = [(vf[0][:, c * LANES:(c + 1) * LANES], vf[1][:, c * LANES:(c + 1) * LANES])
              for c in range(c0, c0 + HY_GRP)]
        trs, tis = [], []
        for v0, v1 in vs:
            ar, ai = _cplx_mm_left(f1r, f1i, v0, v1)
            trs.append(ar * twr - ai * twi)
            tis.append(ar * twi + ai * twr)
        xr, xi = _cplx_mm(jnp.concatenate(trs, axis=0), jnp.concatenate(tis, axis=0), w2_ref[...])
        hr = hf_ref[0, c0:c0 + HY_GRP].reshape(HY_GRP * n1, LANES)
        hi = hf_ref[1, c0:c0 + HY_GRP].reshape(HY_GRP * n1, LANES)
        zr, zi = _cplx_mm(xr * hr - xi * hi, xr * hi + xi * hr, w2c_ref[...])
        for e, (v0, v1) in enumerate(vs):
            zre, zie = zr[e * n1:(e + 1) * n1], zi[e * n1:(e + 1) * n1]
            cr, ci = _cplx_mm_left(gr, gi, zre * twr + zie * twi, zie * twr - zre * twi)
            bz = bias_ref[c0 + e]
            ln = slice((c0 + e) * LANES, (c0 + e + 1) * LANES)
            outs[0].append(xf[0][:, ln] * (cr + bz * v0))
            outs[1].append(xf[1][:, ln] * (ci + bz * v1))
    for b in range(2):
        o_ref[b] = jnp.concatenate(outs[b], axis=1).reshape(k1, HY_CB, LANES)


def _hy_conv(v, xg, hf, bias3, order, tabs):
    f1, tw, w2, w2c, ginv = tabs
    bsz, k1, ch, _ = v.shape
    n1 = 2 * k1
    nb = ch // HY_CB
    f1h = f1[:, :, :k1]
    dat = pl.BlockSpec((2, k1, HY_CB, LANES), lambda j: (0, 0, j, 0))
    return pl.pallas_call(
        _hy_conv_kernel,
        grid=(nb,),
        in_specs=[dat, dat,
                  pl.BlockSpec((2, HY_CB, n1, LANES), lambda j: (0, order * nb + j, 0, 0)),
                  pl.BlockSpec((HY_CB, 1, 1), lambda j: (order * nb + j, 0, 0)),
                  _resident(f1h.shape), _resident(ginv.shape), _resident(tw.shape),
                  _resident(w2.shape), _resident(w2c.shape)],
        out_specs=dat,
        out_shape=jax.ShapeDtypeStruct(v.shape, F32),
        name="hy_conv",
        compiler_params=_cparams("arbitrary"),
    )(v, xg, hf, bias3, f1h, ginv, tw, w2, w2c)


def _hy_ctx_kernel(v_ref, x1_ref, x2_ref, filt_ref, s_ref, bias_ref, f_ref, o_ref, *, n):
    k, big = n // LANES, 2 * n
    fr, fi = f_ref[0], f_ref[1]
    c0, c1 = _lanes_of(v_ref, (0,), k), _lanes_of(v_ref, (1,), k)
    for o, x_ref in enumerate((x1_ref, x2_ref)):
        rows = slice(o * HY_CH, (o + 1) * HY_CH)
        filt = jnp.concatenate([filt_ref[q, rows, :] for q in range(2 * k)], axis=1)
        inv = 1.0 / s_ref[rows, :]
        hr, hi = _mm(filt, fr) * inv, _mm(filt, fi) * inv
        xr = _mm(c0, fr[:n]) - _mm(c1, fi[:n])
        xi = _mm(c0, fi[:n]) + _mm(c1, fr[:n])
        yr, yi = xr * hr - xi * hi, xr * hi + xi * hr
        gr, gi = fr[:, :n] * (1.0 / big), fi[:, :n] * (-1.0 / big)
        cr = _mm(yr, gr) - _mm(yi, gi)
        ci = _mm(yr, gi) + _mm(yi, gr)
        b = bias_ref[rows, :]
        c0 = _lanes_of(x_ref, (0,), k) * (cr + b * c0)
        c1 = _lanes_of(x_ref, (1,), k) * (ci + b * c1)
    for q in range(k):
        o_ref[0, q] = c0[:, q * LANES:(q + 1) * LANES]
        o_ref[1, q] = c1[:, q * LANES:(q + 1) * LANES]


def _hyena_ctx(v, x1, x2, filt, norm, bias):
    bsz, k, ch, _ = v.shape
    n = k * LANES
    big = 2 * n
    idx = np.arange(big)
    ang = 2.0 * np.pi * ((idx[:, None] * idx[None, :]) % big) / big
    f = jnp.asarray(np.stack([np.cos(ang), -np.sin(ang)]).astype(np.float32))
    args = (v, x1, x2, filt, norm, bias.reshape(HY_ROWS, 1), f)
    return pl.pallas_call(
        functools.partial(_hy_ctx_kernel, n=n),
        grid=(1,),
        in_specs=[_resident(a.shape) for a in args],
        out_specs=pl.BlockSpec(v.shape, lambda j: (0, 0, 0, 0)),
        out_shape=jax.ShapeDtypeStruct(v.shape, F32),
        name="hy_ctx",
        compiler_params=_cparams("arbitrary"),
    )(*args)


def _hyena_lat(v, x1, x2, filt, norm, bias):
    assert v.shape[0] == 2
    n = v.shape[1] * LANES
    tabs = _dft_tables(n)
    hf = _hy_spectrum(filt, norm, tabs[0], tabs[1], tabs[2])
    bias3 = bias.reshape(HY_ROWS, 1, 1)
    z = _hy_conv(v, x1, hf, bias3, 0, tabs)
    return _hy_conv(z, x2, hf, bias3, 1, tabs)


def _even_layer_mix(xc, xl, mod_c, mod_l, w_in, w_out, hy_p, gla_p, ln_g, ln_b, need_ctx):
    conv_w, conv_b, filt_params, bias = hy_p
    a_w, a_b, gla_norm = gla_p
    wp = w_in[:, HY_COLS:].astype(BF16)
    wc = jnp.transpose(w_in[:, :HY_COLS]).astype(BF16)
    cw, cb = jnp.transpose(conv_w), conv_b.reshape(HY_COLS, 1)
    psplit = (2 * GLA_QK, HALF, HALF, SMALL)
    csplit = (HY_CH, HY_CH, HY_CH)
    pc = _inproj(xc, mod_c, wp, wc, cw, cb, psplit, csplit, False, "row", True, 1024)
    pl_ = _inproj(xl, mod_l, wp, wc, cw, cb, psplit, csplit, False, "row", True, 1024)
    g_c, g_l = _gla_mixer((pc[0], pc[1], pc[3]), (pl_[0], pl_[1], pl_[3]), a_w, a_b)
    filt, norm = _hyena_filter(xl.shape[1], *filt_params)
    h_l = _hyena_lat(pl_[4], pl_[5], pl_[6], filt, norm, bias)
    nw = [gla_norm.reshape(1, GLA_DV)]
    specs = [lambda tm: _chan_spec(HY_CH, tm)] + [lambda tm: _seq_spec(tm, HALF)] * 3
    xl = _outproj(_outproj_even_kernel, [h_l, g_l[0], g_l[1], pl_[2]], specs, xl, mod_l, w_out,
                  ln_g, ln_b, nw, "row", 1024)
    if need_ctx:
        filt_c, norm_c = _hyena_filter(xc.shape[1], *filt_params)
        h_c = _hyena_ctx(pc[4], pc[5], pc[6], filt_c, norm_c, bias)
        xc = _outproj(_outproj_even_kernel, [h_c, g_c[0], g_c[1], pc[2]], specs, xc, mod_c, w_out,
                      ln_g, ln_b, nw, "row", 1024)
    return xc, xl


def _odd_layer_mix(xc, xl, mod_c, mod_l, w_in, w_out, ml_p, ssd_p, ln_g, ln_b, need_ctx):
    gate_b, ml_norm = ml_p
    conv_w, conv_b, dt_bias, a_log, d_skip, ssd_norm = ssd_p
    ml_cols = 2 * ML_QK + 2 * HALF + 4 * ML_HEADS
    o0 = 2 * ML_QK + 2 * HALF
    s0 = ml_cols
    wp = jnp.concatenate([w_in[:, :o0], w_in[:, s0:s0 + HALF], w_in[:, o0:ml_cols],
                          w_in[:, s0 + HALF + SSD_CONV_DIM:]], axis=1).astype(BF16)
    wc = w_in[:, s0 + HALF:s0 + HALF + SSD_CONV_DIM].astype(BF16)
    cb = conv_b.reshape(1, SSD_CONV_DIM)
    psplit = (2 * ML_QK, HALF, HALF, HALF, SMALL)
    csplit = (HALF, SSD_GN, SSD_GN)
    pc = _inproj(xc, mod_c, wp, wc, conv_w, cb, psplit, csplit, True, "row", False, 512)
    pl_ = _inproj(xl, mod_l, wp, wc, conv_w, cb, psplit, csplit, True, "col", False, 512)
    m_c, m_l = _mlstm_mixer((pc[0], pc[1], pc[4]), (pl_[0], pl_[1], pl_[4]), gate_b)
    s_c, s_l = _ssd_mixer((pc[5], pc[6], pc[7], pc[4]), (pl_[5], pl_[6], pl_[7], pl_[4]), dt_bias, a_log)
    consts = [ml_norm.reshape(1, ML_DV), jnp.repeat(d_skip, SSD_HEADDIM).reshape(1, HALF),
              ssd_norm.reshape(1, HALF)]
    specs = [lambda tm: _seq_spec(tm, HALF)] * 7
    xl = _outproj(_outproj_odd_kernel, [m_l[0], m_l[1], pl_[2], s_l[0], s_l[1], pl_[5], pl_[3]], specs,
                  xl, mod_l, w_out, ln_g, ln_b, consts, "col", 512)
    if need_ctx:
        xc = _outproj(_outproj_odd_kernel, [m_c[0], m_c[1], pc[2], s_c[0], s_c[1], pc[5], pc[3]], specs,
                      xc, mod_c, w_out, ln_g, ln_b, consts, "row", 512)
    return xc, xl


def kernel(x, c, ctx, c_ctx, w_ada, b_ada, ln_g, ln_b, w_ffn_in, w_ffn_out, w_in_even, w_out_even, hy_conv_w, hy_conv_b, hy_w1, hy_b1, hy_w2, hy_b2, hy_w3, hy_freq, hy_bias, gla_a_w, gla_a_b, gla_norm_w, w_in_odd, w_out_odd, ml_gate_b, ml_norm_w, ssd_conv_w, ssd_conv_b, ssd_dt_bias, ssd_a_log, ssd_d, ssd_norm_w):
    bsz, n, d = x.shape
    depth = w_ada.shape[0]
    assert bsz == 2 and d == D_MODEL
    cond8 = jnp.zeros((8, d), F32).at[:bsz].set(c).at[bsz].set(c_ctx)
    mod_all = _ada_all(cond8, w_ada, b_ada)
    xl, xc = x, ctx
    for l in range(depth):
        last = l == depth - 1
        mod_l = mod_all[l, :bsz].reshape(bsz, N_MOD, d)
        mod_c = jnp.broadcast_to(mod_all[l, bsz].reshape(1, N_MOD, d), (bsz, N_MOD, d))
        w1a, w2a = w_ffn_in[l, 0].astype(BF16), w_ffn_out[l, 0].astype(BF16)
        w1b, w2b = w_ffn_in[l, 1].astype(BF16), w_ffn_out[l, 1].astype(BF16)
        odd = l % 2 == 1
        xl = _ffn_step(xl, mod_l, 0, w1a, w2a, ln_g[l, 0], ln_b[l, 0], out_cv=odd)
        xc = _ffn_step(xc, mod_c, 0, w1a, w2a, ln_g[l, 0], ln_b[l, 0])
        i = l // 2
        if not odd:
            hy_p = (hy_conv_w[i], hy_conv_b[i],
                    (hy_w1[i], hy_b1[i], hy_w2[i], hy_b2[i], hy_w3[i], hy_freq[i]), hy_bias[i])
            gla_p = (gla_a_w[i], gla_a_b[i], gla_norm_w[i])
            xc, xl = _even_layer_mix(xc, xl, mod_c, mod_l, w_in_even[i], w_out_even[i].astype(BF16),
                                     hy_p, gla_p, ln_g[l, 1], ln_b[l, 1], not last)
        else:
            ml_p = (ml_gate_b[i], ml_norm_w[i])
            ssd_p = (ssd_conv_w[i], ssd_conv_b[i], ssd_dt_bias[i], ssd_a_log[i], ssd_d[i], ssd_norm_w[i])
            xc, xl = _odd_layer_mix(xc, xl, mod_c, mod_l, w_in_odd[i], w_out_odd[i].astype(BF16),
                                    ml_p, ssd_p, ln_g[l, 1], ln_b[l, 1], not last)
        xl = _ffn_step(xl, mod_l, 6, w1b, w2b, ln_g[l, 2], ln_b[l, 2], in_cv=odd)
        if not last:
            xc = _ffn_step(xc, mod_c, 6, w1b, w2b, ln_g[l, 2], ln_b[l, 2])
    return xl
```
